```python
import math
import jax, jax.numpy as jnp
from jax import lax
import numpy as np

D_MODEL = 2048
BATCH = 4
SEQ = 4096
DEPTH = 2

DN_HEAD_DIM = 128
DN_HEADS = (D_MODEL // 2) // DN_HEAD_DIM
DN_WIDTH = DN_HEADS * DN_HEAD_DIM
CONV_WIDTH = 4
CHUNK = 64
POOL_WINDOWS = (2, 4, 8, 16)
POOL_GROUPS = len(POOL_WINDOWS)
POOL_WIDTH = D_MODEL - DN_WIDTH
POOL_GROUP_DIM = POOL_WIDTH // POOL_GROUPS
MIX_WIDTH = DN_WIDTH + POOL_WIDTH
IN_COLS = 4 * DN_WIDTH + 2 * DN_HEADS + POOL_WIDTH
N_EXPERTS = 16
N_GROUPS = 4
EXPERTS_PER_GROUP = N_EXPERTS // N_GROUPS
TOP_K = 2
D_EXPERT = 1024
DISPATCH_BLOCK = 128
PLE_DIM = 256
ALPHA = (2.0 * DEPTH) ** 0.25
BETA_INIT = (8.0 * DEPTH) ** -0.25
LN_EPS = 1e-5
RMS_EPS = 1e-6

kernel_name = "hybrid_deltanet_pool_groupmoe_deepnorm"


def layer_norm(x, g, b):
    xf = x.astype(jnp.float32)
    mu = jnp.mean(xf, -1, keepdims=True)
    var = jnp.mean(jnp.square(xf - mu), -1, keepdims=True)
    y = (xf - mu) * lax.rsqrt(var + LN_EPS) * g.astype(jnp.float32) + b.astype(jnp.float32)
    return y.astype(x.dtype)


def l2norm(t):
    return t * lax.rsqrt(jnp.sum(t * t, -1, keepdims=True) + RMS_EPS)


def causal_conv_silu(x, w):
    S = x.shape[1]
    xp = jnp.pad(x, ((0, 0), (CONV_WIDTH - 1, 0), (0, 0)))
    y = xp[:, 0:S] * w[0]
    for j in range(1, CONV_WIDTH):
        y = y + xp[:, j:j + S] * w[j]
    return jax.nn.silu(y)


def chunk_gated_delta_rule(q, k, v, beta, g):
    B, S, H, Dk = q.shape
    Dv = v.shape[-1]
    N = S // CHUNK

    def to_chunks(t):
        t = jnp.moveaxis(t, 2, 1)
        return t.reshape((B, H, N, CHUNK) + t.shape[3:])

    q = to_chunks(l2norm(q) * (Dk ** -0.5))
    k = to_chunks(l2norm(k))
    v = to_chunks(v)
    beta = to_chunks(beta)
    g = to_chunks(g)
    gam = jnp.cumsum(g, -1)
    ci = jnp.arange(CHUNK)
    incl = ci[:, None] >= ci[None, :]
    strict = ci[:, None] > ci[None, :]
    decay = jnp.exp(jnp.where(incl, gam[..., :, None] - gam[..., None, :], -jnp.inf))
    kb = k * beta[..., None]
    a = jnp.where(strict, jnp.einsum('bhnid,bhnjd->bhnij', kb, k) * decay, 0.0)
    lhs = a + jnp.eye(CHUNK, dtype=a.dtype)
    rhs = jnp.concatenate([v * beta[..., None], kb * jnp.exp(gam)[..., None]], -1)
    sol = lax.linalg.triangular_solve(lhs, rhs, left_side=True, lower=True, unit_diagonal=True)
    u, w = sol[..., :Dv], sol[..., Dv:]
    attn_qk = jnp.einsum('bhnid,bhnjd->bhnij', q, k) * decay
    q_dec = q * jnp.exp(gam)[..., None]
    k_dec = k * jnp.exp(gam[..., -1:] - gam)[..., None]
    chunk_decay = jnp.exp(gam[..., -1])

    def step(state, inp):
        qd, kd, wc, uc, aqk, cd = inp
        v_new = uc - jnp.einsum('bhck,bhkv->bhcv', wc, state)
        o = jnp.einsum('bhck,bhkv->bhcv', qd, state) + jnp.einsum('bhij,bhjv->bhiv', aqk, v_new)
        state = state * cd[..., None, None] + jnp.einsum('bhck,bhcv->bhkv', kd, v_new)
        return state, o

    xs = (jnp.moveaxis(q_dec, 2, 0), jnp.moveaxis(k_dec, 2, 0), jnp.moveaxis(w, 2, 0),
          jnp.moveaxis(u, 2, 0), jnp.moveaxis(attn_qk, 2, 0), jnp.moveaxis(chunk_decay, 2, 0))
    state0 = jnp.zeros((B, H, Dk, Dv), jnp.float32)
    _, o = lax.scan(step, state0, xs)
    o = jnp.moveaxis(o, 0, 2).reshape(B, H, S, Dv)
    return jnp.moveaxis(o, 1, 2)


def gated_deltanet_group(qkv_raw, z, b_raw, a_raw, conv_w, a_log, dt_bias, norm_w):
    B, S, _ = qkv_raw.shape
    qkv = causal_conv_silu(qkv_raw, conv_w).astype(jnp.float32)
    q, k, v = jnp.split(qkv, 3, axis=-1)
    q = q.reshape(B, S, DN_HEADS, DN_HEAD_DIM)
    k = k.reshape(B, S, DN_HEADS, DN_HEAD_DIM)
    v = v.reshape(B, S, DN_HEADS, DN_HEAD_DIM)
    beta = jax.nn.sigmoid(b_raw.astype(jnp.float32))
    g = -jnp.exp(a_log.astype(jnp.float32)) * jax.nn.softplus(a_raw.astype(jnp.float32) + dt_bias.astype(jnp.float32))
    o = chunk_gated_delta_rule(q, k, v, beta, g)
    o = o * lax.rsqrt(jnp.mean(o * o, -1, keepdims=True) + RMS_EPS) * norm_w.astype(jnp.float32)
    o = o * jax.nn.silu(z.astype(jnp.float32).reshape(B, S, DN_HEADS, DN_HEAD_DIM))
    return o.reshape(B, S, DN_WIDTH).astype(qkv_raw.dtype)


def multiscale_pool_group(u, w_pool, scale):
    B, S, _ = u.shape
    ug = u.astype(jnp.float32).reshape(B, S, POOL_GROUPS, POOL_GROUP_DIM)
    cs = jnp.cumsum(ug, axis=1)
    t = jnp.arange(S)
    outs = []
    for gi, win in enumerate(POOL_WINDOWS):
        csg = cs[:, :, gi]
        shifted = jnp.pad(csg[:, :S - win], ((0, 0), (win, 0), (0, 0)))
        cnt = jnp.minimum(t + 1, win).astype(jnp.float32)
        outs.append((csg - shifted) / cnt[None, :, None] - ug[:, :, gi])
    d = jnp.stack(outs, axis=2)
    y = jnp.einsum('bsgc,gcd->bsgd', d, w_pool.astype(jnp.float32))
    return (y.reshape(B, S, POOL_WIDTH) * scale.astype(jnp.float32)).astype(u.dtype)


def grouped_top2_moe(h, w_router, b_router, w_gate, w_up, w_down):
    T, D = h.shape
    TK = T * TOP_K
    logits = h.astype(jnp.float32) @ w_router.astype(jnp.float32) + b_router.astype(jnp.float32)
    probs = jax.nn.softmax(logits, -1).reshape(T, N_GROUPS, EXPERTS_PER_GROUP)
    group_score = jnp.sum(lax.top_k(probs, TOP_K)[0], -1)
    g_sel = jnp.argmax(group_score, -1)
    in_group = jnp.take_along_axis(probs, g_sel[:, None, None], axis=1)[:, 0]
    top_p, top_i = lax.top_k(in_group, TOP_K)
    expert_idx = g_sel[:, None] * EXPERTS_PER_GROUP + top_i
    gate = top_p / jnp.sum(top_p, -1, keepdims=True)

    flat_e = expert_idx.reshape(-1).astype(jnp.int32)
    flat_tok = jnp.repeat(jnp.arange(T, dtype=jnp.int32), TOP_K)
    flat_w = gate.reshape(-1)
    order = jnp.argsort(flat_e)
    se = flat_e[order]
    counts = jnp.bincount(flat_e, length=N_EXPERTS).astype(jnp.int32)
    start = jnp.cumsum(counts) - counts
    pcounts = (counts + DISPATCH_BLOCK - 1) // DISPATCH_BLOCK * DISPATCH_BLOCK
    pend = jnp.cumsum(pcounts)
    pstart = pend - pcounts
    dest = pstart[se] + jnp.arange(TK, dtype=jnp.int32) - start[se]
    n_rows = TK + N_EXPERTS * DISPATCH_BLOCK
    n_blocks = n_rows // DISPATCH_BLOCK
    src_tok = jnp.full((n_rows,), T, jnp.int32).at[dest].set(flat_tok[order])
    src_w = jnp.zeros((n_rows,), jnp.float32).at[dest].set(flat_w[order])
    block_start = jnp.arange(n_blocks, dtype=jnp.int32) * DISPATCH_BLOCK
    block_e = jnp.minimum(jnp.searchsorted(pend, block_start, side='right'), N_EXPERTS - 1)

    h_pad = jnp.concatenate([h, jnp.zeros((1, D), h.dtype)], 0)
    x_rows = h_pad[src_tok].reshape(n_blocks, DISPATCH_BLOCK, D)

    def expert_rows(args):
        xb, e = args
        hid = jax.nn.silu(xb @ w_gate[e]) * (xb @ w_up[e])
        return hid @ w_down[e]

    y_rows = lax.map(expert_rows, (x_rows, block_e)).reshape(n_rows, D)
    y_rows = y_rows * src_w[:, None].astype(y_rows.dtype)
    out = jax.ops.segment_sum(y_rows, src_tok, num_segments=T + 1)
    return out[:T]


def setup_inputs(seed: int = 0) -> dict:
    key = jax.random.key(seed)
    ks = jax.random.split(key, 24)
    f32 = jnp.float32
    nrm = lambda k, shape: jax.random.normal(k, shape, f32)
    dt = jnp.exp(jax.random.uniform(ks[5], (DEPTH, DN_HEADS), f32, math.log(1e-3), math.log(1e-1)))
    return {
        "x": nrm(ks[0], (BATCH, SEQ, D_MODEL)),
        "p": nrm(ks[1], (DEPTH, BATCH, SEQ, PLE_DIM)),
        "w_in": nrm(ks[2], (DEPTH, D_MODEL, IN_COLS)) * D_MODEL ** -0.5,
        "conv_w": nrm(ks[3], (DEPTH, CONV_WIDTH, 3 * DN_WIDTH)) * CONV_WIDTH ** -0.5,
        "a_log": jnp.log(jax.random.uniform(ks[4], (DEPTH, DN_HEADS), f32, 1.0, 16.0)),
        "dt_bias": dt + jnp.log(-jnp.expm1(-dt)),
        "dn_norm_w": 1.0 + 0.02 * nrm(ks[6], (DEPTH, DN_HEAD_DIM)),
        "pool_w": nrm(ks[7], (DEPTH, POOL_GROUPS, POOL_GROUP_DIM, POOL_GROUP_DIM)) * POOL_GROUP_DIM ** -0.5,
        "pool_scale": 1.0 + 0.1 * nrm(ks[8], (DEPTH, POOL_WIDTH)),
        "w_out": nrm(ks[9], (DEPTH, MIX_WIDTH, D_MODEL)) * (MIX_WIDTH ** -0.5 * BETA_INIT),
        "ln1_g": 1.0 + 0.02 * nrm(ks[10], (DEPTH, D_MODEL)),
        "ln1_b": 0.02 * nrm(ks[11], (DEPTH, D_MODEL)),
        "w_router": nrm(ks[12], (D_MODEL, N_EXPERTS)) * D_MODEL ** -0.5,
        "b_router": 0.01 * nrm(ks[13], (N_EXPERTS,)),
        "w_e_gate": nrm(ks[14], (DEPTH, N_EXPERTS, D_MODEL, D_EXPERT)) * D_MODEL ** -0.5,
        "w_e_up": nrm(ks[15], (DEPTH, N_EXPERTS, D_MODEL, D_EXPERT)) * D_MODEL ** -0.5,
        "w_e_down": nrm(ks[16], (DEPTH, N_EXPERTS, D_EXPERT, D_MODEL)) * (D_EXPERT ** -0.5 * BETA_INIT),
        "ln2_g": 1.0 + 0.02 * nrm(ks[17], (DEPTH, D_MODEL)),
        "ln2_b": 0.02 * nrm(ks[18], (DEPTH, D_MODEL)),
        "w_ple_proj": nrm(ks[19], (DEPTH, PLE_DIM, D_MODEL)) * (PLE_DIM ** -0.5 * BETA_INIT),
        "w_ple_gate": nrm(ks[20], (DEPTH, D_MODEL, D_MODEL)) * D_MODEL ** -0.5,
    }


def reference(x, p, w_in, conv_w, a_log, dt_bias, dn_norm_w, pool_w, pool_scale, w_out,
              ln1_g, ln1_b, w_router, b_router, w_e_gate, w_e_up, w_e_down, ln2_g, ln2_b,
              w_ple_proj, w_ple_gate):
    B, S, D = x.shape
    splits = [3 * DN_WIDTH, 4 * DN_WIDTH, 4 * DN_WIDTH + DN_HEADS, 4 * DN_WIDTH + 2 * DN_HEADS]
    for i in range(DEPTH):
        proj = x @ w_in[i]
        qkv_raw, z, b_raw, a_raw, u = jnp.split(proj, splits, axis=-1)
        y_dn = gated_deltanet_group(qkv_raw, z, b_raw, a_raw, conv_w[i], a_log[i], dt_bias[i], dn_norm_w[i])
        y_pool = multiscale_pool_group(u, pool_w[i], pool_scale[i])
        mix = jnp.concatenate([y_dn, y_pool], -1) @ w_out[i]
        x = layer_norm(ALPHA * x + mix, ln1_g[i], ln1_b[i])
        ffn = grouped_top2_moe(x.reshape(B * S, D), w_router, b_router,
                               w_e_gate[i], w_e_up[i], w_e_down[i]).reshape(B, S, D)
        x = layer_norm(ALPHA * x + ffn, ln2_g[i], ln2_b[i])
        x = x + jax.nn.sigmoid(x @ w_ple_gate[i]) * (p[i] @ w_ple_proj[i])
    return x
```

```python
import functools

import jax
import jax.numpy as jnp
from jax import lax
from jax.experimental import pallas as pl
from jax.experimental.pallas import tpu as pltpu

F32 = jnp.float32
BF16 = jnp.bfloat16
I32 = jnp.int32
HIGHEST = lax.Precision.HIGHEST

D_MODEL = 2048
DN_HEADS = 8
HEAD_DIM = 128
DN_WIDTH = DN_HEADS * HEAD_DIM
CONV_WIDTH = 4
CHUNK = 64
POOL_WINDOWS = (2, 4, 8, 16)
POOL_GROUP_DIM = 256
POOL_WIDTH = 1024
N_EXPERTS = 16
N_GROUPS = 4
EXPERTS_PER_GROUP = 4
D_EXPERT = 1024
PLE_DIM = 256
DEPTH = 2
ALPHA = (2.0 * DEPTH) ** 0.25
LN_EPS = 1e-5
RMS_EPS = 1e-6

LANES = 128
MAIN_COLS = 4 * DN_WIDTH + POOL_WIDTH
CONV_HALO = 8
POOL_HALO = 16
EXPERT_BLOCK = 256
VMEM_LIMIT = 56 * 1024 * 1024


def _cparams(sem):
    return pltpu.CompilerParams(dimension_semantics=sem, vmem_limit_bytes=VMEM_LIMIT)


def _dot(a, b):
    return jnp.dot(a, b, preferred_element_type=F32)


def _dot_hi(a, b):
    return jnp.dot(a, b, preferred_element_type=F32, precision=HIGHEST)


def _layer_norm(h, g, b):
    mu = jnp.mean(h, -1, keepdims=True)
    d = h - mu
    var = jnp.mean(d * d, -1, keepdims=True)
    return d * lax.rsqrt(var + LN_EPS) * g + b


def _proj_body(x_ref, w_ref, wba_ref, o_ref, ba_ref):
    x = x_ref[...]
    o_ref[...] = _dot(x, w_ref[...])

    @pl.when(pl.program_id(1) == 0)
    def _():
        ba_ref[...] = _dot(x, wba_ref[...])


def _proj(xb, w_main, w_ba):
    T = xb.shape[0]
    tm = min(1024, T)
    tn = 1024
    return pl.pallas_call(
        _proj_body,
        grid=(T // tm, MAIN_COLS // tn),
        in_specs=[
            pl.BlockSpec((tm, D_MODEL), lambda i, j: (i, 0)),
            pl.BlockSpec((D_MODEL, tn), lambda i, j: (0, j)),
            pl.BlockSpec((D_MODEL, LANES), lambda i, j: (0, 0)),
        ],
        out_specs=[
            pl.BlockSpec((tm, tn), lambda i, j: (i, j)),
            pl.BlockSpec((tm, LANES), lambda i, j: (i, 0)),
        ],
        out_shape=[
            jax.ShapeDtypeStruct((T, MAIN_COLS), F32),
            jax.ShapeDtypeStruct((T, LANES), F32),
        ],
        compiler_params=_cparams(("parallel", "arbitrary")),
        name="proj",
    )(xb, w_main, w_ba)


def _prep_body(tiles_per_seq, tm, qkv_ref, halo_ref, ba_ref, cw_ref, gp_ref, tri_ref,
               q_ref, k_ref, v_ref, bg_ref, xs_ref):
    first = (pl.program_id(0) % tiles_per_seq) == 0
    xs_ref[0:CONV_HALO, :] = jnp.where(first, 0.0, halo_ref[...])
    xs_ref[CONV_HALO:CONV_HALO + tm, :] = qkv_ref[...]
    base = CONV_HALO - (CONV_WIDTH - 1)
    for cb in range(3 * DN_HEADS):
        cs = slice(cb * HEAD_DIM, (cb + 1) * HEAD_DIM)
        acc = xs_ref[base:base + tm, cs] * cw_ref[0:1, cs]
        for j in range(1, CONV_WIDTH):
            acc = acc + xs_ref[base + j:base + j + tm, cs] * cw_ref[j:j + 1, cs]
        y = acc * jax.nn.sigmoid(acc)
        if cb < 2 * DN_HEADS:
            y = y * lax.rsqrt(jnp.sum(y * y, -1, keepdims=True) + RMS_EPS)
        if cb < DN_HEADS:
            q_ref[:, cs] = y * (HEAD_DIM ** -0.5)
        elif cb < 2 * DN_HEADS:
            k_ref[:, slice((cb - DN_HEADS) * HEAD_DIM, (cb - DN_HEADS + 1) * HEAD_DIM)] = y
        else:
            v_ref[:, slice((cb - 2 * DN_HEADS) * HEAD_DIM, (cb - 2 * DN_HEADS + 1) * HEAD_DIM)] = y
    ba = ba_ref[...]
    beta = jax.nn.sigmoid(ba)
    xx = ba + gp_ref[1:2, :]
    softplus = jnp.maximum(xx, 0.0) + jnp.log1p(jnp.exp(-jnp.abs(xx)))
    g = -jnp.exp(gp_ref[0:1, :]) * softplus
    gam = _dot_hi(tri_ref[...], g)
    lane = lax.broadcasted_iota(I32, ba.shape, 1)
    bg_ref[...] = jnp.where(lane < DN_HEADS, beta, gam)


def _prep(proj, ba, conv_w, a_log, dt_bias, S):
    T = proj.shape[0]
    tm = min(256, S)
    pad = LANES - 2 * DN_HEADS
    gp = jnp.stack([
        jnp.pad(a_log.astype(F32), (DN_HEADS, pad)),
        jnp.pad(dt_bias.astype(F32), (DN_HEADS, pad)),
    ])
    r = jnp.arange(tm)
    tri = ((r[:, None] >= r[None, :]) & (r[:, None] // CHUNK == r[None, :] // CHUNK)).astype(F32)
    halo_blocks = tm // CONV_HALO
    out = jax.ShapeDtypeStruct((T, DN_WIDTH), F32)
    return pl.pallas_call(
        functools.partial(_prep_body, S // tm, tm),
        grid=(T // tm,),
        in_specs=[
            pl.BlockSpec((tm, 3 * DN_WIDTH), lambda i: (i, 0)),
            pl.BlockSpec((CONV_HALO, 3 * DN_WIDTH), lambda i: (jnp.maximum(i * halo_blocks - 1, 0), 0)),
            pl.BlockSpec((tm, LANES), lambda i: (i, 0)),
            pl.BlockSpec((CONV_WIDTH, 3 * DN_WIDTH), lambda i: (0, 0)),
            pl.BlockSpec((2, LANES), lambda i: (0, 0)),
            pl.BlockSpec((tm, tm), lambda i: (0, 0)),
        ],
        out_specs=[
            pl.BlockSpec((tm, DN_WIDTH), lambda i: (i, 0)),
            pl.BlockSpec((tm, DN_WIDTH), lambda i: (i, 0)),
            pl.BlockSpec((tm, DN_WIDTH), lambda i: (i, 0)),
            pl.BlockSpec((tm, LANES), lambda i: (i, 0)),
        ],
        out_shape=[out, out, out, jax.ShapeDtypeStruct((T, LANES), F32)],
        scratch_shapes=[pltpu.VMEM((tm + CONV_HALO, 3 * DN_WIDTH), F32)],
        compiler_params=_cparams(("parallel",)),
        name="prep",
    )(proj, proj, ba, conv_w.astype(F32), gp, tri)


def _delta_body(nc, q_ref, k_ref, v_ref, z_ref, bg_ref, gt_ref, nw_ref, o_ref, state_ref):
    @pl.when(pl.program_id(1) == 0)
    def _():
        state_ref[...] = jnp.zeros(state_ref.shape, F32)

    ii = lax.broadcasted_iota(I32, (CHUNK, CHUNK), 0)
    jj = lax.broadcasted_iota(I32, (CHUNK, CHUNK), 1)
    incl = ii >= jj
    strict = ii > jj
    eye = (ii == jj).astype(F32)
    nt = (((1,), (1,)), ((), ()))
    tn = (((0,), (0,)), ((), ()))

    def chunk(c, carry):
        r0 = pl.multiple_of(c * CHUNK, CHUNK)
        rows = pl.ds(r0, CHUNK)
        bg = bg_ref[rows, :]
        gt = gt_ref[0, c]
        for h in range(DN_HEADS):
            cs = slice(h * HEAD_DIM, (h + 1) * HEAD_DIM)
            kh = k_ref[rows, cs]
            qh = q_ref[rows, cs]
            vh = v_ref[rows, cs]
            bcol = bg[:, h:h + 1]
            gcol = bg[:, DN_HEADS + h:DN_HEADS + h + 1]
            grow = gt[h:h + 1, :]
            glast = grow[:, CHUNK - 1:CHUNK]
            decay = jnp.where(incl, jnp.exp(jnp.where(incl, gcol - grow, 0.0)), 0.0)
            kb = kh * bcol
            lhs = jnp.concatenate([kb, qh], 0).astype(BF16)
            s = lax.dot_general(lhs, kh.astype(BF16), nt, preferred_element_type=F32)
            a = jnp.where(strict, s[:CHUNK] * decay, 0.0)
            aqk = s[CHUNK:] * decay
            pw = -a
            tinv = eye + pw
            for _ in range(5):
                pw = _dot_hi(pw, pw)
                tinv = tinv + _dot_hi(tinv, pw)
            eg = jnp.exp(gcol)
            rhs = jnp.concatenate([vh * bcol, kb * eg], 1)
            sol = _dot_hi(tinv, rhs)
            u = sol[:, :HEAD_DIM]
            w = sol[:, HEAD_DIM:]
            st = state_ref[h]
            wq = jnp.concatenate([w, qh * eg], 0).astype(BF16)
            r = _dot(wq, st.astype(BF16))
            v_new = (u - r[:CHUNK]).astype(BF16)
            o = r[CHUNK:] + _dot(aqk.astype(BF16), v_new)
            kd = (kh * jnp.exp(glast - gcol)).astype(BF16)
            state_ref[h] = st * jnp.exp(glast) + lax.dot_general(kd, v_new, tn, preferred_element_type=F32)
            zz = z_ref[rows, cs]
            y = o * lax.rsqrt(jnp.mean(o * o, -1, keepdims=True) + RMS_EPS) * nw_ref[...]
            o_ref[rows, cs] = (y * (zz * jax.nn.sigmoid(zz))).astype(o_ref.dtype)
        return carry

    lax.fori_loop(0, nc, chunk, 0)


def _delta(q, k, v, proj, bg, gt, norm_w, B, S):
    T = q.shape[0]
    sblk = min(512, S)
    nc = sblk // CHUNK
    nsb = S // sblk
    row = lambda b, s: (b * nsb + s, 0)
    return pl.pallas_call(
        functools.partial(_delta_body, nc),
        grid=(B, nsb),
        in_specs=[
            pl.BlockSpec((sblk, DN_WIDTH), row),
            pl.BlockSpec((sblk, DN_WIDTH), row),
            pl.BlockSpec((sblk, DN_WIDTH), row),
            pl.BlockSpec((sblk, DN_WIDTH), lambda b, s: (b * nsb + s, 3)),
            pl.BlockSpec((sblk, LANES), row),
            pl.BlockSpec((1, nc, DN_HEADS, CHUNK), lambda b, s: (b, s, 0, 0)),
            pl.BlockSpec((1, HEAD_DIM), lambda b, s: (0, 0)),
        ],
        out_specs=pl.BlockSpec((sblk, DN_WIDTH), row),
        out_shape=jax.ShapeDtypeStruct((T, DN_WIDTH), BF16),
        scratch_shapes=[pltpu.VMEM((DN_HEADS, HEAD_DIM, HEAD_DIM), F32)],
        compiler_params=_cparams(("parallel", "arbitrary")),
        name="delta",
    )(q, k, v, proj, bg, gt, norm_w.astype(F32).reshape(1, HEAD_DIM))


def _pool_body(tiles_per_seq, tm, u_ref, halo_ref, w_ref, sc_ref, o_ref, us_ref):
    t_in_seq = (pl.program_id(0) % tiles_per_seq) * tm
    first = t_in_seq == 0
    us_ref[0:POOL_HALO, :] = jnp.where(first, 0.0, halo_ref[...])
    us_ref[POOL_HALO:POOL_HALO + tm, :] = u_ref[...]
    tpos = (t_in_seq + lax.broadcasted_iota(I32, (tm, 1), 0) + 1).astype(F32)
    for gi, win in enumerate(POOL_WINDOWS):
        cs = slice(gi * POOL_GROUP_DIM, (gi + 1) * POOL_GROUP_DIM)
        cur = us_ref[POOL_HALO:POOL_HALO + tm, cs]
        wsum = cur
        for j in range(1, win):
            wsum = wsum + us_ref[POOL_HALO - j:POOL_HALO - j + tm, cs]
        d = wsum / jnp.minimum(tpos, float(win)) - cur
        y = _dot(d.astype(BF16), w_ref[gi])
        o_ref[:, cs] = (y * sc_ref[:, cs]).astype(o_ref.dtype)


def _pool(proj, pool_w, pool_scale, S):
    T = proj.shape[0]
    tm = min(256, S)
    ucol = (4 * DN_WIDTH) // POOL_WIDTH
    halo_blocks = tm // POOL_HALO
    return pl.pallas_call(
        functools.partial(_pool_body, S // tm, tm),
        grid=(T // tm,),
        in_specs=[
            pl.BlockSpec((tm, POOL_WIDTH), lambda i: (i, ucol)),
            pl.BlockSpec((POOL_HALO, POOL_WIDTH), lambda i: (jnp.maximum(i * halo_blocks - 1, 0), ucol)),
            pl.BlockSpec((len(POOL_WINDOWS), POOL_GROUP_DIM, POOL_GROUP_DIM), lambda i: (0, 0, 0)),
            pl.BlockSpec((1, POOL_WIDTH), lambda i: (0, 0)),
        ],
        out_specs=pl.BlockSpec((tm, POOL_WIDTH), lambda i: (i, 0)),
        out_shape=jax.ShapeDtypeStruct((T, POOL_WIDTH), BF16),
        scratch_shapes=[pltpu.VMEM((tm + POOL_HALO, POOL_WIDTH), F32)],
        compiler_params=_cparams(("parallel",)),
        name="pool",
    )(proj, proj, pool_w.astype(BF16), pool_scale.astype(F32).reshape(1, POOL_WIDTH))


def _outproj_body(ydn_ref, ypool_ref, w_ref, x_ref, g_ref, b_ref, o_ref):
    mix = _dot(ydn_ref[...], w_ref[0:DN_WIDTH, :]) + _dot(ypool_ref[...], w_ref[DN_WIDTH:, :])
    o_ref[...] = _layer_norm(ALPHA * x_ref[...] + mix, g_ref[...], b_ref[...])


def _outproj(y_dn, y_pool, w_out, x, g, b):
    T = x.shape[0]
    tm = min(512, T)
    row = lambda i: (i, 0)
    const = lambda i: (0, 0)
    return pl.pallas_call(
        _outproj_body,
        grid=(T // tm,),
        in_specs=[
            pl.BlockSpec((tm, DN_WIDTH), row),
            pl.BlockSpec((tm, POOL_WIDTH), row),
            pl.BlockSpec((D_MODEL, D_MODEL), const),
            pl.BlockSpec((tm, D_MODEL), row),
            pl.BlockSpec((1, D_MODEL), const),
            pl.BlockSpec((1, D_MODEL), const),
        ],
        out_specs=pl.BlockSpec((tm, D_MODEL), row),
        out_shape=jax.ShapeDtypeStruct((T, D_MODEL), F32),
        compiler_params=_cparams(("parallel",)),
        name="outproj",
    )(y_dn, y_pool, w_out.astype(BF16), x, g.astype(F32).reshape(1, D_MODEL), b.astype(F32).reshape(1, D_MODEL))


def _router_body(tm, x_ref, wr_ref, br_ref, upper_ref, eidx_ref, gate_ref, rank_ref, cnt_ref, carry_ref):
    @pl.when(pl.program_id(0) == 0)
    def _():
        carry_ref[...] = jnp.zeros(carry_ref.shape, F32)

    nt = (((1,), (1,)), ((), ()))
    logits = lax.dot_general(wr_ref[...], x_ref[...], nt, precision=HIGHEST,
                             preferred_element_type=F32) + br_ref[:, 0:1]
    m = jnp.max(logits, axis=0, keepdims=True)
    e = jnp.exp(logits - m)
    p = e / jnp.sum(e, axis=0, keepdims=True)
    rows = [p[i:i + 1, :] for i in range(N_EXPERTS)]

    scores = []
    for g in range(N_GROUPS):
        a, b, c, d = rows[EXPERTS_PER_GROUP * g:EXPERTS_PER_GROUP * (g + 1)]
        hi1, lo1 = jnp.maximum(a, b), jnp.minimum(a, b)
        hi2, lo2 = jnp.maximum(c, d), jnp.minimum(c, d)
        top1 = jnp.maximum(hi1, hi2)
        top2 = jnp.maximum(jnp.minimum(hi1, hi2), jnp.where(hi1 >= hi2, lo1, lo2))
        scores.append(top1 + top2)
    gsel = jnp.zeros((1, tm), I32)
    best = scores[0]
    for g in range(1, N_GROUPS):
        better = scores[g] > best
        gsel = jnp.where(better, g, gsel)
        best = jnp.where(better, scores[g], best)
    ing = []
    for j in range(EXPERTS_PER_GROUP):
        sel = rows[(N_GROUPS - 1) * EXPERTS_PER_GROUP + j]
        for g in range(N_GROUPS - 2, -1, -1):
            sel = jnp.where(gsel == g, rows[g * EXPERTS_PER_GROUP + j], sel)
        ing.append(sel)
    i1 = jnp.zeros((1, tm), I32)
    p1 = ing[0]
    for j in range(1, EXPERTS_PER_GROUP):
        better = ing[j] > p1
        i1 = jnp.where(better, j, i1)
        p1 = jnp.where(better, ing[j], p1)
    i2 = jnp.zeros((1, tm), I32)
    p2 = jnp.full((1, tm), -1.0, F32)
    for j in range(EXPERTS_PER_GROUP):
        cand = jnp.where(i1 == j, -1.0, ing[j])
        better = cand > p2
        i2 = jnp.where(better, j, i2)
        p2 = jnp.where(better, cand, p2)
    den = p1 + p2
    e0 = gsel * EXPERTS_PER_GROUP + i1
    e1 = gsel * EXPERTS_PER_GROUP + i2
    eidx_ref[0:1, :] = e0
    eidx_ref[1:2, :] = e1
    gate_ref[0:1, :] = p1 / den
    gate_ref[1:2, :] = p2 / den

    er = lax.broadcasted_iota(I32, (N_EXPERTS, tm), 0)
    oh0 = er == e0
    oh1 = er == e1
    oh = jnp.where(oh0 | oh1, 1.0, 0.0)
    before = carry_ref[:, 0:1] + _dot(oh.astype(BF16), upper_ref[...])
    rank_ref[0:1, :] = jnp.sum(jnp.where(oh0, before, 0.0), axis=0, keepdims=True).astype(I32)
    rank_ref[1:2, :] = jnp.sum(jnp.where(oh1, before, 0.0), axis=0, keepdims=True).astype(I32)
    total = carry_ref[...] + jnp.sum(oh, axis=1, keepdims=True)
    carry_ref[...] = total
    cnt_ref[...] = total.astype(I32)


def _router(x1, w_router, b_router):
    T = x1.shape[0]
    tm = min(512, T)
    r = jnp.arange(tm)
    upper = (r[:, None] < r[None, :]).astype(BF16)
    tok = lambda i: (0, i)
    const = lambda i: (0, 0)
    return pl.pallas_call(
        functools.partial(_router_body, tm),
        grid=(T // tm,),
        in_specs=[
            pl.BlockSpec((tm, D_MODEL), lambda i: (i, 0)),
            pl.BlockSpec((N_EXPERTS, D_MODEL), const),
            pl.BlockSpec((N_EXPERTS, LANES), const),
            pl.BlockSpec((tm, tm), const),
        ],
        out_specs=[
            pl.BlockSpec((2, tm), tok),
            pl.BlockSpec((2, tm), tok),
            pl.BlockSpec((2, tm), tok),
            pl.BlockSpec((N_EXPERTS, LANES), const),
        ],
        out_shape=[
            jax.ShapeDtypeStruct((2, T), I32),
            jax.ShapeDtypeStruct((2, T), F32),
            jax.ShapeDtypeStruct((2, T), I32),
            jax.ShapeDtypeStruct((N_EXPERTS, LANES), I32),
        ],
        scratch_shapes=[pltpu.VMEM((N_EXPERTS, LANES), F32)],
        compiler_params=_cparams(("arbitrary",)),
        name="router",
    )(x1, w_router.astype(F32).T, jnp.broadcast_to(b_router.astype(F32)[:, None], (N_EXPERTS, LANES)), upper)


def _dispatch_body(tm, T, pos_ref, x_ref, xs_in_ref, xs_ref, sem):
    del xs_in_ref
    base = pl.program_id(0) * tm

    def issue(r, carry):
        for k in range(2):
            dst = pos_ref[k * T + base + r]
            pltpu.make_async_copy(x_ref.at[pl.ds(r, 1)], xs_ref.at[pl.ds(dst, 1)], sem).start()
        return carry

    lax.fori_loop(0, tm, issue, 0)
    for k in range(2):
        pltpu.make_async_copy(x_ref, xs_ref.at[pl.ds(0, tm)], sem).wait()


def _dispatch(pos_flat, x1, n_rows):
    T = x1.shape[0]
    tm = min(256, T)
    xs0 = jnp.zeros((n_rows, D_MODEL), F32)
    return pl.pallas_call(
        functools.partial(_dispatch_body, tm, T),
        grid_spec=pltpu.PrefetchScalarGridSpec(
            num_scalar_prefetch=1,
            grid=(T // tm,),
            in_specs=[
                pl.BlockSpec((tm, D_MODEL), lambda i, pos: (i, 0)),
                pl.BlockSpec(memory_space=pl.ANY),
            ],
            out_specs=pl.BlockSpec(memory_space=pl.ANY),
            scratch_shapes=[pltpu.SemaphoreType.DMA],
        ),
        out_shape=jax.ShapeDtypeStruct((n_rows, D_MODEL), F32),
        input_output_aliases={2: 0},
        compiler_params=_cparams(("arbitrary",)),
        name="dispatch",
    )(pos_flat, x1, xs0)


def _experts_body(be_ref, nb_ref, xs_ref, wg_ref, wu_ref, wd_ref, y_ref):
    @pl.when(pl.program_id(0) < nb_ref[0])
    def _():
        xb = xs_ref[...].astype(BF16)
        g = _dot(xb, wg_ref[0])
        u = _dot(xb, wu_ref[0])
        hid = (g * jax.nn.sigmoid(g)) * u
        y_ref[...] = _dot(hid.astype(BF16), wd_ref[0])

    @pl.when(pl.program_id(0) >= nb_ref[0])
    def _():
        y_ref[...] = jnp.zeros(y_ref.shape, F32)


def _experts(block_e, nb, xs, wg, wu, wd):
    n_rows = xs.shape[0]
    bm = EXPERT_BLOCK
    blk = lambda i, be, nb: (jnp.minimum(i, nb[0] - 1), 0)
    wsel = lambda i, be, nb: (be[jnp.minimum(i, nb[0] - 1)], 0, 0)
    return pl.pallas_call(
        _experts_body,
        grid_spec=pltpu.PrefetchScalarGridSpec(
            num_scalar_prefetch=2,
            grid=(n_rows // bm,),
            in_specs=[
                pl.BlockSpec((bm, D_MODEL), blk),
                pl.BlockSpec((1, D_MODEL, D_EXPERT), wsel),
                pl.BlockSpec((1, D_MODEL, D_EXPERT), wsel),
                pl.BlockSpec((1, D_EXPERT, D_MODEL), wsel),
            ],
            out_specs=pl.BlockSpec((bm, D_MODEL), lambda i, be, nb: (i, 0)),
        ),
        out_shape=jax.ShapeDtypeStruct((n_rows, D_MODEL), F32),
        compiler_params=_cparams(("arbitrary",)),
        name="experts",
    )(block_e, nb, xs, wg, wu, wd)


def _combine_body(tm, T, pos_ref, y_ref, gate_ref, x1_ref, p_ref, wpg_ref, wpp_ref, g_ref, b_ref,
                  o_ref, ob_ref, ybuf_ref, sem):
    base = pl.program_id(0) * tm

    def issue(r, carry):
        for k in range(2):
            src = pos_ref[k * T + base + r]
            pltpu.make_async_copy(y_ref.at[pl.ds(src, 1)], ybuf_ref.at[k, pl.ds(r, 1)], sem).start()
        return carry

    lax.fori_loop(0, tm, issue, 0)
    for k in range(2):
        pltpu.make_async_copy(y_ref.at[pl.ds(0, tm)], ybuf_ref.at[k], sem).wait()
    gate = gate_ref[...]
    ffn = ybuf_ref[0] * gate[:, 0:1] + ybuf_ref[1] * gate[:, 1:2]
    x2 = _layer_norm(ALPHA * x1_ref[...] + ffn, g_ref[...], b_ref[...])
    gl = _dot(x2.astype(BF16), wpg_ref[...])
    pp = _dot(p_ref[...].astype(BF16), wpp_ref[...])
    x3 = x2 + jax.nn.sigmoid(gl) * pp
    o_ref[...] = x3
    ob_ref[...] = x3.astype(BF16)


def _combine(pos_flat, y, gate_t, x1, p, w_ple_gate, w_ple_proj, g, b):
    T = x1.shape[0]
    tm = min(256, T)
    row = lambda i, pos: (i, 0)
    const = lambda i, pos: (0, 0)
    return pl.pallas_call(
        functools.partial(_combine_body, tm, T),
        grid_spec=pltpu.PrefetchScalarGridSpec(
            num_scalar_prefetch=1,
            grid=(T // tm,),
            in_specs=[
                pl.BlockSpec(memory_space=pl.ANY),
                pl.BlockSpec((tm, 2), row),
                pl.BlockSpec((tm, D_MODEL), row),
                pl.BlockSpec((tm, PLE_DIM), row),
                pl.BlockSpec((D_MODEL, D_MODEL), const),
                pl.BlockSpec((PLE_DIM, D_MODEL), const),
                pl.BlockSpec((1, D_MODEL), const),
                pl.BlockSpec((1, D_MODEL), const),
            ],
            out_specs=[
                pl.BlockSpec((tm, D_MODEL), row),
                pl.BlockSpec((tm, D_MODEL), row),
            ],
            scratch_shapes=[pltpu.VMEM((2, tm, D_MODEL), F32), pltpu.SemaphoreType.DMA],
        ),
        out_shape=[
            jax.ShapeDtypeStruct((T, D_MODEL), F32),
            jax.ShapeDtypeStruct((T, D_MODEL), BF16),
        ],
        compiler_params=_cparams(("arbitrary",)),
        name="combine",
    )(pos_flat, y, gate_t, x1, p, w_ple_gate.astype(BF16), w_ple_proj.astype(BF16),
      g.astype(F32).reshape(1, D_MODEL), b.astype(F32).reshape(1, D_MODEL))


def _route_plan(eidx, rank, cnt, n_blocks):
    bm = EXPERT_BLOCK
    counts = cnt[:, 0]
    pcounts = (counts + bm - 1) // bm * bm
    pend = jnp.cumsum(pcounts)
    pstart = pend - pcounts
    pos = pstart[eidx] + rank
    nb = (pend[-1] // bm).astype(I32).reshape(1)
    block_start = jnp.arange(n_blocks, dtype=I32) * bm
    block_e = jnp.minimum(jnp.searchsorted(pend, block_start, side="right"), N_EXPERTS - 1).astype(I32)
    return pos.reshape(-1).astype(I32), block_e, nb


def kernel(x, p, w_in, conv_w, a_log, dt_bias, dn_norm_w, pool_w, pool_scale, w_out, ln1_g, ln1_b,
           w_router, b_router, w_e_gate, w_e_up, w_e_down, ln2_g, ln2_b, w_ple_proj, w_ple_gate):
    B, S, D = x.shape
    T = B * S
    n_rows = 2 * T + N_EXPERTS * EXPERT_BLOCK
    xf = x.reshape(T, D).astype(F32)
    xb = xf.astype(BF16)
    for i in range(DEPTH):
        wi = w_in[i]
        w_main = jnp.concatenate([wi[:, :4 * DN_WIDTH], wi[:, 4 * DN_WIDTH + 2 * DN_HEADS:]], 1).astype(BF16)
        w_ba = jnp.pad(wi[:, 4 * DN_WIDTH:4 * DN_WIDTH + 2 * DN_HEADS],
                       ((0, 0), (0, LANES - 2 * DN_HEADS))).astype(BF16)
        proj, ba = _proj(xb, w_main, w_ba)
        q, k, v, bg = _prep(proj, ba, conv_w[i], a_log[i], dt_bias[i], S)
        gt = bg[:, DN_HEADS:2 * DN_HEADS].reshape(B, S // CHUNK, CHUNK, DN_HEADS).transpose(0, 1, 3, 2)
        y_dn = _delta(q, k, v, proj, bg, gt, dn_norm_w[i], B, S)
        y_pool = _pool(proj, pool_w[i], pool_scale[i], S)
        x1 = _outproj(y_dn, y_pool, w_out[i], xf, ln1_g[i], ln1_b[i])
        eidx, gate, rank, cnt = _router(x1, w_router, b_router)
        pos_flat, block_e, nb = _route_plan(eidx, rank, cnt, n_rows // EXPERT_BLOCK)
        xs = _dispatch(pos_flat, x1, n_rows)
        y = _experts(block_e, nb, xs, w_e_gate[i].astype(BF16), w_e_up[i].astype(BF16),
                     w_e_down[i].astype(BF16))
        xf, xb = _combine(pos_flat, y, gate.T, x1, p[i].reshape(T, PLE_DIM), w_ple_gate[i], w_ple_proj[i],
                          ln2_g[i], ln2_b[i])
    return xf.reshape(B, S, D).astype(x.dtype)
```

```python
import functools

import jax
import jax.numpy as jnp
from jax import lax
from jax.experimental import pallas as pl
from jax.experimental.pallas import tpu as pltpu

F32 = jnp.float32
BF16 = jnp.bfloat16
I32 = jnp.int32
HIGHEST = lax.Precision.HIGHEST

D_MODEL = 2048
DN_HEADS = 8
HEAD_DIM = 128
DN_WIDTH = DN_HEADS * HEAD_DIM
CONV_WIDTH = 4
CHUNK = 64
POOL_WINDOWS = (2, 4, 8, 16)
POOL_GROUP_DIM = 256
POOL_WIDTH = 1024
N_EXPERTS = 16
N_GROUPS = 4
EXPERTS_PER_GROUP = 4
D_EXPERT = 1024
PLE_DIM = 256
DEPTH = 2
ALPHA = (2.0 * DEPTH) ** 0.25
LN_EPS = 1e-5
RMS_EPS = 1e-6

LANES = 128
MAIN_COLS = 4 * DN_WIDTH + POOL_WIDTH
CONV_HALO = 8
POOL_HALO = 16
EXPERT_BLOCK = 256
VMEM_LIMIT = 56 * 1024 * 1024


def _cparams(sem):
    return pltpu.CompilerParams(dimension_semantics=sem, vmem_limit_bytes=VMEM_LIMIT)


def _dot(a, b):
    return jnp.dot(a, b, preferred_element_type=F32)


def _dot_hi(a, b):
    return jnp.dot(a, b, preferred_element_type=F32, precision=HIGHEST)


def _layer_norm(h, g, b):
    mu = jnp.mean(h, -1, keepdims=True)
    d = h - mu
    var = jnp.mean(d * d, -1, keepdims=True)
    return d * lax.rsqrt(var + LN_EPS) * g + b


def _proj_body(x_ref, w_ref, wba_ref, o_ref, ba_ref):
    x = x_ref[...]
    o_ref[...] = _dot(x, w_ref[...])

    @pl.when(pl.program_id(1) == 0)
    def _():
        ba_ref[...] = _dot(x, wba_ref[...])


def _proj(xb, w_main, w_ba):
    T = xb.shape[0]
    tm = min(1024, T)
    tn = 1024
    return pl.pallas_call(
        _proj_body,
        grid=(T // tm, MAIN_COLS // tn),
        in_specs=[
            pl.BlockSpec((tm, D_MODEL), lambda i, j: (i, 0)),
            pl.BlockSpec((D_MODEL, tn), lambda i, j: (0, j)),
            pl.BlockSpec((D_MODEL, LANES), lambda i, j: (0, 0)),
        ],
        out_specs=[
            pl.BlockSpec((tm, tn), lambda i, j: (i, j)),
            pl.BlockSpec((tm, LANES), lambda i, j: (i, 0)),
        ],
        out_shape=[
            jax.ShapeDtypeStruct((T, MAIN_COLS), F32),
            jax.ShapeDtypeStruct((T, LANES), F32),
        ],
        compiler_params=_cparams(("parallel", "arbitrary")),
        name="proj",
    )(xb, w_main, w_ba)


def _prep_body(tiles_per_seq, tm, qkv_ref, halo_ref, ba_ref, cw_ref, gp_ref, tri_ref,
               q_ref, k_ref, v_ref, bg_ref, xs_ref):
    first = (pl.program_id(0) % tiles_per_seq) == 0
    xs_ref[0:CONV_HALO, :] = jnp.where(first, 0.0, halo_ref[...])
    xs_ref[CONV_HALO:CONV_HALO + tm, :] = qkv_ref[...]
    base = CONV_HALO - (CONV_WIDTH - 1)
    for cb in range(3 * DN_HEADS):
        cs = slice(cb * HEAD_DIM, (cb + 1) * HEAD_DIM)
        acc = xs_ref[base:base + tm, cs] * cw_ref[0:1, cs]
        for j in range(1, CONV_WIDTH):
            acc = acc + xs_ref[base + j:base + j + tm, cs] * cw_ref[j:j + 1, cs]
        y = acc * jax.nn.sigmoid(acc)
        if cb < 2 * DN_HEADS:
            y = y * lax.rsqrt(jnp.sum(y * y, -1, keepdims=True) + RMS_EPS)
        if cb < DN_HEADS:
            q_ref[:, cs] = y * (HEAD_DIM ** -0.5)
        elif cb < 2 * DN_HEADS:
            k_ref[:, slice((cb - DN_HEADS) * HEAD_DIM, (cb - DN_HEADS + 1) * HEAD_DIM)] = y
        else:
            v_ref[:, slice((cb - 2 * DN_HEADS) * HEAD_DIM, (cb - 2 * DN_HEADS + 1) * HEAD_DIM)] = y
    ba = ba_ref[...]
    beta = jax.nn.sigmoid(ba)
    xx = ba + gp_ref[1:2, :]
    softplus = jnp.maximum(xx, 0.0) + jnp.log1p(jnp.exp(-jnp.abs(xx)))
    g = -jnp.exp(gp_ref[0:1, :]) * softplus
    gam = _dot_hi(tri_ref[...], g)
    lane = lax.broadcasted_iota(I32, ba.shape, 1)
    bg_ref[...] = jnp.where(lane < DN_HEADS, beta, gam)


def _prep(proj, ba, conv_w, a_log, dt_bias, S):
    T = proj.shape[0]
    tm = min(256, S)
    pad = LANES - 2 * DN_HEADS
    gp = jnp.stack([
        jnp.pad(a_log.astype(F32), (DN_HEADS, pad)),
        jnp.pad(dt_bias.astype(F32), (DN_HEADS, pad)),
    ])
    r = jnp.arange(tm)
    tri = ((r[:, None] >= r[None, :]) & (r[:, None] // CHUNK == r[None, :] // CHUNK)).astype(F32)
    halo_blocks = tm // CONV_HALO
    out = jax.ShapeDtypeStruct((T, DN_WIDTH), F32)
    return pl.pallas_call(
        functools.partial(_prep_body, S // tm, tm),
        grid=(T // tm,),
        in_specs=[
            pl.BlockSpec((tm, 3 * DN_WIDTH), lambda i: (i, 0)),
            pl.BlockSpec((CONV_HALO, 3 * DN_WIDTH), lambda i: (jnp.maximum(i * halo_blocks - 1, 0), 0)),
            pl.BlockSpec((tm, LANES), lambda i: (i, 0)),
            pl.BlockSpec((CONV_WIDTH, 3 * DN_WIDTH), lambda i: (0, 0)),
            pl.BlockSpec((2, LANES), lambda i: (0, 0)),
            pl.BlockSpec((tm, tm), lambda i: (0, 0)),
        ],
        out_specs=[
            pl.BlockSpec((tm, DN_WIDTH), lambda i: (i, 0)),
            pl.BlockSpec((tm, DN_WIDTH), lambda i: (i, 0)),
            pl.BlockSpec((tm, DN_WIDTH), lambda i: (i, 0)),
            pl.BlockSpec((tm, LANES), lambda i: (i, 0)),
        ],
        out_shape=[out, out, out, jax.ShapeDtypeStruct((T, LANES), F32)],
        scratch_shapes=[pltpu.VMEM((tm + CONV_HALO, 3 * DN_WIDTH), F32)],
        compiler_params=_cparams(("parallel",)),
        name="prep",
    )(proj, proj, ba, conv_w.astype(F32), gp, tri)


def _delta_body(nc, q_ref, k_ref, v_ref, z_ref, bg_ref, gt_ref, nw_ref, o_ref, state_ref):
    @pl.when(pl.program_id(1) == 0)
    def _():
        state_ref[...] = jnp.zeros(state_ref.shape, F32)

    ii = lax.broadcasted_iota(I32, (CHUNK, CHUNK), 0)
    jj = lax.broadcasted_iota(I32, (CHUNK, CHUNK), 1)
    incl = ii >= jj
    strict = ii > jj
    eye = (ii == jj).astype(F32)
    nt = (((1,), (1,)), ((), ()))
    tn = (((0,), (0,)), ((), ()))

    def chunk(c, carry):
        r0 = pl.multiple_of(c * CHUNK, CHUNK)
        rows = pl.ds(r0, CHUNK)
        bg = bg_ref[rows, :]
        gt = gt_ref[0, c]
        heads = range(DN_HEADS)
        cs = [slice(h * HEAD_DIM, (h + 1) * HEAD_DIM) for h in heads]
        kh = [k_ref[rows, cs[h]] for h in heads]
        qh = [q_ref[rows, cs[h]] for h in heads]
        bcol = [bg[:, h:h + 1] for h in heads]
        gcol = [bg[:, DN_HEADS + h:DN_HEADS + h + 1] for h in heads]
        glast = [gt[h:h + 1, CHUNK - 1:CHUNK] for h in heads]
        kb = [kh[h] * bcol[h] for h in heads]
        s = [lax.dot_general(jnp.concatenate([kb[h], qh[h]], 0).astype(BF16), kh[h].astype(BF16), nt,
                             preferred_element_type=F32) for h in heads]
        decay = [jnp.where(incl, jnp.exp(jnp.where(incl, gcol[h] - gt[h:h + 1, :], 0.0)), 0.0) for h in heads]
        aqk = [(s[h][CHUNK:] * decay[h]).astype(BF16) for h in heads]
        pw = [jnp.where(strict, -s[h][:CHUNK] * decay[h], 0.0) for h in heads]
        qs = pw
        for _ in range(5):
            pwb = [pw[h].astype(BF16) for h in heads]
            pw = [_dot(pwb[h], pwb[h]) for h in heads]
            qp = [_dot(qs[h].astype(BF16), pw[h].astype(BF16)) for h in heads]
            qs = [qs[h] + pw[h] + qp[h] for h in heads]
        eg = [jnp.exp(gcol[h]) for h in heads]
        rhs = [jnp.concatenate([v_ref[rows, cs[h]] * bcol[h], kb[h] * eg[h]], 1) for h in heads]
        sol = [rhs[h] + _dot(qs[h].astype(BF16), rhs[h].astype(BF16)) for h in heads]
        st = [state_ref[h] for h in heads]
        r = [_dot(jnp.concatenate([sol[h][:, HEAD_DIM:], qh[h] * eg[h]], 0).astype(BF16), st[h].astype(BF16))
             for h in heads]
        v_new = [(sol[h][:, :HEAD_DIM] - r[h][:CHUNK]).astype(BF16) for h in heads]
        o = [r[h][CHUNK:] + _dot(aqk[h], v_new[h]) for h in heads]
        kd = [(kh[h] * jnp.exp(glast[h] - gcol[h])).astype(BF16) for h in heads]
        upd = [lax.dot_general(kd[h], v_new[h], tn, preferred_element_type=F32) for h in heads]
        for h in heads:
            state_ref[h] = st[h] * jnp.exp(glast[h]) + upd[h]
            zz = z_ref[rows, cs[h]]
            y = o[h] * lax.rsqrt(jnp.mean(o[h] * o[h], -1, keepdims=True) + RMS_EPS) * nw_ref[...]
            o_ref[rows, cs[h]] = (y * (zz * jax.nn.sigmoid(zz))).astype(o_ref.dtype)
        return carry

    lax.fori_loop(0, nc, chunk, 0)


def _delta(q, k, v, proj, bg, gt, norm_w, B, S):
    T = q.shape[0]
    sblk = min(512, S)
    nc = sblk // CHUNK
    nsb = S // sblk
    row = lambda b, s: (b * nsb + s, 0)
    return pl.pallas_call(
        functools.partial(_delta_body, nc),
        grid=(B, nsb),
        in_specs=[
            pl.BlockSpec((sblk, DN_WIDTH), row),
            pl.BlockSpec((sblk, DN_WIDTH), row),
            pl.BlockSpec((sblk, DN_WIDTH), row),
            pl.BlockSpec((sblk, DN_WIDTH), lambda b, s: (b * nsb + s, 3)),
            pl.BlockSpec((sblk, LANES), row),
            pl.BlockSpec((1, nc, DN_HEADS, CHUNK), lambda b, s: (b, s, 0, 0)),
            pl.BlockSpec((1, HEAD_DIM), lambda b, s: (0, 0)),
        ],
        out_specs=pl.BlockSpec((sblk, DN_WIDTH), row),
        out_shape=jax.ShapeDtypeStruct((T, DN_WIDTH), BF16),
        scratch_shapes=[pltpu.VMEM((DN_HEADS, HEAD_DIM, HEAD_DIM), F32)],
        compiler_params=_cparams(("parallel", "arbitrary")),
        name="delta",
    )(q, k, v, proj, bg, gt, norm_w.astype(F32).reshape(1, HEAD_DIM))


def _pool_body(tiles_per_seq, tm, u_ref, halo_ref, w_ref, sc_ref, o_ref, us_ref):
    t_in_seq = (pl.program_id(0) % tiles_per_seq) * tm
    first = t_in_seq == 0
    us_ref[0:POOL_HALO, :] = jnp.where(first, 0.0, halo_ref[...])
    us_ref[POOL_HALO:POOL_HALO + tm, :] = u_ref[...]
    tpos = (t_in_seq + lax.broadcasted_iota(I32, (tm, 1), 0) + 1).astype(F32)
    for gi, win in enumerate(POOL_WINDOWS):
        cs = slice(gi * POOL_GROUP_DIM, (gi + 1) * POOL_GROUP_DIM)
        cur = us_ref[POOL_HALO:POOL_HALO + tm, cs]
        wsum = cur
        for j in range(1, win):
            wsum = wsum + us_ref[POOL_HALO - j:POOL_HALO - j + tm, cs]
        d = wsum / jnp.minimum(tpos, float(win)) - cur
        y = _dot(d.astype(BF16), w_ref[gi])
        o_ref[:, cs] = (y * sc_ref[:, cs]).astype(o_ref.dtype)


def _pool(proj, pool_w, pool_scale, S):
    T = proj.shape[0]
    tm = min(256, S)
    ucol = (4 * DN_WIDTH) // POOL_WIDTH
    halo_blocks = tm // POOL_HALO
    return pl.pallas_call(
        functools.partial(_pool_body, S // tm, tm),
        grid=(T // tm,),
        in_specs=[
            pl.BlockSpec((tm, POOL_WIDTH), lambda i: (i, ucol)),
            pl.BlockSpec((POOL_HALO, POOL_WIDTH), lambda i: (jnp.maximum(i * halo_blocks - 1, 0), ucol)),
            pl.BlockSpec((len(POOL_WINDOWS), POOL_GROUP_DIM, POOL_GROUP_DIM), lambda i: (0, 0, 0)),
            pl.BlockSpec((1, POOL_WIDTH), lambda i: (0, 0)),
        ],
        out_specs=pl.BlockSpec((tm, POOL_WIDTH), lambda i: (i, 0)),
        out_shape=jax.ShapeDtypeStruct((T, POOL_WIDTH), BF16),
        scratch_shapes=[pltpu.VMEM((tm + POOL_HALO, POOL_WIDTH), F32)],
        compiler_params=_cparams(("parallel",)),
        name="pool",
    )(proj, proj, pool_w.astype(BF16), pool_scale.astype(F32).reshape(1, POOL_WIDTH))


def _outproj_body(ydn_ref, ypool_ref, w_ref, x_ref, g_ref, b_ref, o_ref):
    mix = _dot(ydn_ref[...], w_ref[0:DN_WIDTH, :]) + _dot(ypool_ref[...], w_ref[DN_WIDTH:, :])
    o_ref[...] = _layer_norm(ALPHA * x_ref[...] + mix, g_ref[...], b_ref[...])


def _outproj(y_dn, y_pool, w_out, x, g, b):
    T = x.shape[0]
    tm = min(512, T)
    row = lambda i: (i, 0)
    const = lambda i: (0, 0)
    return pl.pallas_call(
        _outproj_body,
        grid=(T // tm,),
        in_specs=[
            pl.BlockSpec((tm, DN_WIDTH), row),
            pl.BlockSpec((tm, POOL_WIDTH), row),
            pl.BlockSpec((D_MODEL, D_MODEL), const),
            pl.BlockSpec((tm, D_MODEL), row),
            pl.BlockSpec((1, D_MODEL), const),
            pl.BlockSpec((1, D_MODEL), const),
        ],
        out_specs=pl.BlockSpec((tm, D_MODEL), row),
        out_shape=jax.ShapeDtypeStruct((T, D_MODEL), F32),
        compiler_params=_cparams(("parallel",)),
        name="outproj",
    )(y_dn, y_pool, w_out.astype(BF16), x, g.astype(F32).reshape(1, D_MODEL), b.astype(F32).reshape(1, D_MODEL))


def _router_body(tm, x_ref, wr_ref, br_ref, upper_ref, eidx_ref, gate_ref, rank_ref, cnt_ref, carry_ref):
    @pl.when(pl.program_id(0) == 0)
    def _():
        carry_ref[...] = jnp.zeros(carry_ref.shape, F32)

    nt = (((1,), (1,)), ((), ()))
    logits = lax.dot_general(wr_ref[...], x_ref[...], nt, precision=HIGHEST,
                             preferred_element_type=F32) + br_ref[:, 0:1]
    m = jnp.max(logits, axis=0, keepdims=True)
    e = jnp.exp(logits - m)
    p = e / jnp.sum(e, axis=0, keepdims=True)
    rows = [p[i:i + 1, :] for i in range(N_EXPERTS)]

    scores = []
    for g in range(N_GROUPS):
        a, b, c, d = rows[EXPERTS_PER_GROUP * g:EXPERTS_PER_GROUP * (g + 1)]
        hi1, lo1 = jnp.maximum(a, b), jnp.minimum(a, b)
        hi2, lo2 = jnp.maximum(c, d), jnp.minimum(c, d)
        top1 = jnp.maximum(hi1, hi2)
        top2 = jnp.maximum(jnp.minimum(hi1, hi2), jnp.where(hi1 >= hi2, lo1, lo2))
        scores.append(top1 + top2)
    gsel = jnp.zeros((1, tm), I32)
    best = scores[0]
    for g in range(1, N_GROUPS):
        better = scores[g] > best
        gsel = jnp.where(better, g, gsel)
        best = jnp.where(better, scores[g], best)
    ing = []
    for j in range(EXPERTS_PER_GROUP):
        sel = rows[(N_GROUPS - 1) * EXPERTS_PER_GROUP + j]
        for g in range(N_GROUPS - 2, -1, -1):
            sel = jnp.where(gsel == g, rows[g * EXPERTS_PER_GROUP + j], sel)
        ing.append(sel)
    i1 = jnp.zeros((1, tm), I32)
    p1 = ing[0]
    for j in range(1, EXPERTS_PER_GROUP):
        better = ing[j] > p1
        i1 = jnp.where(better, j, i1)
        p1 = jnp.where(better, ing[j], p1)
    i2 = jnp.zeros((1, tm), I32)
    p2 = jnp.full((1, tm), -1.0, F32)
    for j in range(EXPERTS_PER_GROUP):
        cand = jnp.where(i1 == j, -1.0, ing[j])
        better = cand > p2
        i2 = jnp.where(better, j, i2)
        p2 = jnp.where(better, cand, p2)
    den = p1 + p2
    e0 = gsel * EXPERTS_PER_GROUP + i1
    e1 = gsel * EXPERTS_PER_GROUP + i2
    eidx_ref[0:1, :] = e0
    eidx_ref[1:2, :] = e1
    gate_ref[0:1, :] = p1 / den
    gate_ref[1:2, :] = p2 / den

    er = lax.broadcasted_iota(I32, (N_EXPERTS, tm), 0)
    oh0 = er == e0
    oh1 = er == e1
    oh = jnp.where(oh0 | oh1, 1.0, 0.0)
    before = carry_ref[:, 0:1] + _dot(oh.astype(BF16), upper_ref[...])
    rank_ref[0:1, :] = jnp.sum(jnp.where(oh0, before, 0.0), axis=0, keepdims=True).astype(I32)
    rank_ref[1:2, :] = jnp.sum(jnp.where(oh1, before, 0.0), axis=0, keepdims=True).astype(I32)
    total = carry_ref[...] + jnp.sum(oh, axis=1, keepdims=True)
    carry_ref[...] = total
    cnt_ref[...] = total.astype(I32)


def _router(x1, w_router, b_router):
    T = x1.shape[0]
    tm = min(512, T)
    r = jnp.arange(tm)
    upper = (r[:, None] < r[None, :]).astype(BF16)
    tok = lambda i: (0, i)
    const = lambda i: (0, 0)
    return pl.pallas_call(
        functools.partial(_router_body, tm),
        grid=(T // tm,),
        in_specs=[
            pl.BlockSpec((tm, D_MODEL), lambda i: (i, 0)),
            pl.BlockSpec((N_EXPERTS, D_MODEL), const),
            pl.BlockSpec((N_EXPERTS, LANES), const),
            pl.BlockSpec((tm, tm), const),
        ],
        out_specs=[
            pl.BlockSpec((2, tm), tok),
            pl.BlockSpec((2, tm), tok),
            pl.BlockSpec((2, tm), tok),
            pl.BlockSpec((N_EXPERTS, LANES), const),
        ],
        out_shape=[
            jax.ShapeDtypeStruct((2, T), I32),
            jax.ShapeDtypeStruct((2, T), F32),
            jax.ShapeDtypeStruct((2, T), I32),
            jax.ShapeDtypeStruct((N_EXPERTS, LANES), I32),
        ],
        scratch_shapes=[pltpu.VMEM((N_EXPERTS, LANES), F32)],
        compiler_params=_cparams(("arbitrary",)),
        name="router",
    )(x1, w_router.astype(F32).T, jnp.broadcast_to(b_router.astype(F32)[:, None], (N_EXPERTS, LANES)), upper)


def _dispatch_body(tm, T, pos_ref, x_ref, xs_in_ref, xs_ref, sem):
    del xs_in_ref
    base = pl.program_id(0) * tm

    def issue(r, carry):
        for k in range(2):
            dst = pos_ref[k * T + base + r]
            pltpu.make_async_copy(x_ref.at[pl.ds(r, 1)], xs_ref.at[pl.ds(dst, 1)], sem).start()
        return carry

    lax.fori_loop(0, tm, issue, 0)
    for k in range(2):
        pltpu.make_async_copy(x_ref, xs_ref.at[pl.ds(0, tm)], sem).wait()


def _dispatch(pos_flat, x1, n_rows):
    T = x1.shape[0]
    tm = min(256, T)
    xs0 = jnp.zeros((n_rows, D_MODEL), F32)
    return pl.pallas_call(
        functools.partial(_dispatch_body, tm, T),
        grid_spec=pltpu.PrefetchScalarGridSpec(
            num_scalar_prefetch=1,
            grid=(T // tm,),
            in_specs=[
                pl.BlockSpec((tm, D_MODEL), lambda i, pos: (i, 0)),
                pl.BlockSpec(memory_space=pl.ANY),
            ],
            out_specs=pl.BlockSpec(memory_space=pl.ANY),
            scratch_shapes=[pltpu.SemaphoreType.DMA],
        ),
        out_shape=jax.ShapeDtypeStruct((n_rows, D_MODEL), F32),
        input_output_aliases={2: 0},
        compiler_params=_cparams(("arbitrary",)),
        name="dispatch",
    )(pos_flat, x1, xs0)


def _experts_body(be_ref, nb_ref, xs_ref, wg_ref, wu_ref, wd_ref, y_ref):
    @pl.when(pl.program_id(0) < nb_ref[0])
    def _():
        xb = xs_ref[...].astype(BF16)
        g = _dot(xb, wg_ref[0])
        u = _dot(xb, wu_ref[0])
        hid = (g * jax.nn.sigmoid(g)) * u
        y_ref[...] = _dot(hid.astype(BF16), wd_ref[0])

    @pl.when(pl.program_id(0) >= nb_ref[0])
    def _():
        y_ref[...] = jnp.zeros(y_ref.shape, F32)


def _experts(block_e, nb, xs, wg, wu, wd):
    n_rows = xs.shape[0]
    bm = EXPERT_BLOCK
    blk = lambda i, be, nb: (jnp.minimum(i, nb[0] - 1), 0)
    wsel = lambda i, be, nb: (be[jnp.minimum(i, nb[0] - 1)], 0, 0)
    return pl.pallas_call(
        _experts_body,
        grid_spec=pltpu.PrefetchScalarGridSpec(
            num_scalar_prefetch=2,
            grid=(n_rows // bm,),
            in_specs=[
                pl.BlockSpec((bm, D_MODEL), blk),
                pl.BlockSpec((1, D_MODEL, D_EXPERT), wsel),
                pl.BlockSpec((1, D_MODEL, D_EXPERT), wsel),
                pl.BlockSpec((1, D_EXPERT, D_MODEL), wsel),
            ],
            out_specs=pl.BlockSpec((bm, D_MODEL), lambda i, be, nb: (i, 0)),
        ),
        out_shape=jax.ShapeDtypeStruct((n_rows, D_MODEL), F32),
        compiler_params=_cparams(("arbitrary",)),
        name="experts",
    )(block_e, nb, xs, wg, wu, wd)


def _combine_body(tm, T, pos_ref, y_ref, gate_ref, x1_ref, p_ref, wpg_ref, wpp_ref, g_ref, b_ref,
                  o_ref, ob_ref, ybuf_ref, sem):
    base = pl.program_id(0) * tm

    def issue(r, carry):
        for k in range(2):
            src = pos_ref[k * T + base + r]
            pltpu.make_async_copy(y_ref.at[pl.ds(src, 1)], ybuf_ref.at[k, pl.ds(r, 1)], sem).start()
        return carry

    lax.fori_loop(0, tm, issue, 0)
    for k in range(2):
        pltpu.make_async_copy(y_ref.at[pl.ds(0, tm)], ybuf_ref.at[k], sem).wait()
    gate = gate_ref[...]
    ffn = ybuf_ref[0] * gate[:, 0:1] + ybuf_ref[1] * gate[:, 1:2]
    x2 = _layer_norm(ALPHA * x1_ref[...] + ffn, g_ref[...], b_ref[...])
    gl = _dot(x2.astype(BF16), wpg_ref[...])
    pp = _dot(p_ref[...].astype(BF16), wpp_ref[...])
    x3 = x2 + jax.nn.sigmoid(gl) * pp
    o_ref[...] = x3
    ob_ref[...] = x3.astype(BF16)


def _combine(pos_flat, y, gate_t, x1, p, w_ple_gate, w_ple_proj, g, b):
    T = x1.shape[0]
    tm = min(256, T)
    row = lambda i, pos: (i, 0)
    const = lambda i, pos: (0, 0)
    return pl.pallas_call(
        functools.partial(_combine_body, tm, T),
        grid_spec=pltpu.PrefetchScalarGridSpec(
            num_scalar_prefetch=1,
            grid=(T // tm,),
            in_specs=[
                pl.BlockSpec(memory_space=pl.ANY),
                pl.BlockSpec((tm, 2), row),
                pl.BlockSpec((tm, D_MODEL), row),
                pl.BlockSpec((tm, PLE_DIM), row),
                pl.BlockSpec((D_MODEL, D_MODEL), const),
                pl.BlockSpec((PLE_DIM, D_MODEL), const),
                pl.BlockSpec((1, D_MODEL), const),
                pl.BlockSpec((1, D_MODEL), const),
            ],
            out_specs=[
                pl.BlockSpec((tm, D_MODEL), row),
                pl.BlockSpec((tm, D_MODEL), row),
            ],
            scratch_shapes=[pltpu.VMEM((2, tm, D_MODEL), F32), pltpu.SemaphoreType.DMA],
        ),
        out_shape=[
            jax.ShapeDtypeStruct((T, D_MODEL), F32),
            jax.ShapeDtypeStruct((T, D_MODEL), BF16),
        ],
        compiler_params=_cparams(("arbitrary",)),
        name="combine",
    )(pos_flat, y, gate_t, x1, p, w_ple_gate.astype(BF16), w_ple_proj.astype(BF16),
      g.astype(F32).reshape(1, D_MODEL), b.astype(F32).reshape(1, D_MODEL))


def _route_plan(eidx, rank, cnt, n_blocks):
    bm = EXPERT_BLOCK
    counts = cnt[:, 0]
    pcounts = (counts + bm - 1) // bm * bm
    pend = jnp.cumsum(pcounts)
    pstart = pend - pcounts
    onehot = eidx[:, :, None] == jnp.arange(N_EXPERTS, dtype=I32)
    pos = jnp.sum(jnp.where(onehot, pstart, 0), -1) + rank
    nb = (pend[-1] // bm).astype(I32).reshape(1)
    block_start = jnp.arange(n_blocks, dtype=I32) * bm
    block_e = jnp.minimum(jnp.sum(block_start[:, None] >= pend[None, :], -1), N_EXPERTS - 1).astype(I32)
    return pos.reshape(-1).astype(I32), block_e, nb


def kernel(x, p, w_in, conv_w, a_log, dt_bias, dn_norm_w, pool_w, pool_scale, w_out, ln1_g, ln1_b,
           w_router, b_router, w_e_gate, w_e_up, w_e_down, ln2_g, ln2_b, w_ple_proj, w_ple_gate):
    B, S, D = x.shape
    T = B * S
    n_rows = 2 * T + N_EXPERTS * EXPERT_BLOCK
    xf = x.reshape(T, D).astype(F32)
    xb = xf.astype(BF16)
    for i in range(DEPTH):
        wi = w_in[i]
        w_main = jnp.concatenate([wi[:, :4 * DN_WIDTH], wi[:, 4 * DN_WIDTH + 2 * DN_HEADS:]], 1).astype(BF16)
        w_ba = jnp.pad(wi[:, 4 * DN_WIDTH:4 * DN_WIDTH + 2 * DN_HEADS],
                       ((0, 0), (0, LANES - 2 * DN_HEADS))).astype(BF16)
        proj, ba = _proj(xb, w_main, w_ba)
        q, k, v, bg = _prep(proj, ba, conv_w[i], a_log[i], dt_bias[i], S)
        gt = bg[:, DN_HEADS:2 * DN_HEADS].reshape(B, S // CHUNK, CHUNK, DN_HEADS).transpose(0, 1, 3, 2)
        y_dn = _delta(q, k, v, proj, bg, gt, dn_norm_w[i], B, S)
        y_pool = _pool(proj, pool_w[i], pool_scale[i], S)
        x1 = _outproj(y_dn, y_pool, w_out[i], xf, ln1_g[i], ln1_b[i])
        eidx, gate, rank, cnt = _router(x1, w_router, b_router)
        pos_flat, block_e, nb = _route_plan(eidx, rank, cnt, n_rows // EXPERT_BLOCK)
        xs = _dispatch(pos_flat, x1, n_rows)
        y = _experts(block_e, nb, xs, w_e_gate[i].astype(BF16), w_e_up[i].astype(BF16),
                     w_e_down[i].astype(BF16))
        xf, xb = _combine(pos_flat, y, gate.T, x1, p[i].reshape(T, PLE_DIM), w_ple_gate[i], w_ple_proj[i],
                          ln2_g[i], ln2_b[i])
    return xf.reshape(B, S, D).astype(x.dtype)
```

```python
import functools

import jax
import jax.numpy as jnp
from jax import lax
from jax.experimental import pallas as pl
from jax.experimental.pallas import tpu as pltpu

F32 = jnp.float32
BF16 = jnp.bfloat16
I32 = jnp.int32
HIGHEST = lax.Precision.HIGHEST

D_MODEL = 2048
DN_HEADS = 8
HEAD_DIM = 128
DN_WIDTH = DN_HEADS * HEAD_DIM
CONV_WIDTH = 4
CHUNK = 64
POOL_WINDOWS = (2, 4, 8, 16)
POOL_GROUP_DIM = 256
POOL_WIDTH = 1024
N_EXPERTS = 16
N_GROUPS = 4
EXPERTS_PER_GROUP = 4
D_EXPERT = 1024
PLE_DIM = 256
DEPTH = 2
ALPHA = (2.0 * DEPTH) ** 0.25
LN_EPS = 1e-5
RMS_EPS = 1e-6

LANES = 128
MAIN_COLS = 4 * DN_WIDTH + POOL_WIDTH
CONV_HALO = 8
POOL_HALO = 16
EXPERT_BLOCK = 256
VMEM_LIMIT = 56 * 1024 * 1024


def _cparams(sem):
    return pltpu.CompilerParams(dimension_semantics=sem, vmem_limit_bytes=VMEM_LIMIT)


def _dot(a, b):
    return jnp.dot(a, b, preferred_element_type=F32)


def _dot_hi(a, b):
    return jnp.dot(a, b, preferred_element_type=F32, precision=HIGHEST)


def _layer_norm(h, g, b):
    mu = jnp.mean(h, -1, keepdims=True)
    d = h - mu
    var = jnp.mean(d * d, -1, keepdims=True)
    return d * lax.rsqrt(var + LN_EPS) * g + b


def _proj_body(x_ref, w_ref, wba_ref, o_ref, ba_ref):
    x = x_ref[...]
    o_ref[...] = _dot(x, w_ref[...])

    @pl.when(pl.program_id(1) == 0)
    def _():
        ba_ref[...] = _dot(x, wba_ref[...])


def _proj(xb, w_main, w_ba):
    T = xb.shape[0]
    tm = min(1024, T)
    tn = 1024
    return pl.pallas_call(
        _proj_body,
        grid=(T // tm, MAIN_COLS // tn),
        in_specs=[
            pl.BlockSpec((tm, D_MODEL), lambda i, j: (i, 0)),
            pl.BlockSpec((D_MODEL, tn), lambda i, j: (0, j)),
            pl.BlockSpec((D_MODEL, LANES), lambda i, j: (0, 0)),
        ],
        out_specs=[
            pl.BlockSpec((tm, tn), lambda i, j: (i, j)),
            pl.BlockSpec((tm, LANES), lambda i, j: (i, 0)),
        ],
        out_shape=[
            jax.ShapeDtypeStruct((T, MAIN_COLS), F32),
            jax.ShapeDtypeStruct((T, LANES), F32),
        ],
        compiler_params=_cparams(("parallel", "arbitrary")),
        name="proj",
    )(xb, w_main, w_ba)


def _prep_body(tiles_per_seq, tm, qkv_ref, halo_ref, ba_ref, cw_ref, gp_ref, tri_ref,
               q_ref, k_ref, v_ref, bg_ref, xs_ref):
    first = (pl.program_id(0) % tiles_per_seq) == 0
    xs_ref[0:CONV_HALO, :] = jnp.where(first, 0.0, halo_ref[...])
    xs_ref[CONV_HALO:CONV_HALO + tm, :] = qkv_ref[...]
    base = CONV_HALO - (CONV_WIDTH - 1)
    for cb in range(3 * DN_HEADS):
        cs = slice(cb * HEAD_DIM, (cb + 1) * HEAD_DIM)
        acc = xs_ref[base:base + tm, cs] * cw_ref[0:1, cs]
        for j in range(1, CONV_WIDTH):
            acc = acc + xs_ref[base + j:base + j + tm, cs] * cw_ref[j:j + 1, cs]
        y = acc * jax.nn.sigmoid(acc)
        if cb < 2 * DN_HEADS:
            y = y * lax.rsqrt(jnp.sum(y * y, -1, keepdims=True) + RMS_EPS)
        if cb < DN_HEADS:
            q_ref[:, cs] = y * (HEAD_DIM ** -0.5)
        elif cb < 2 * DN_HEADS:
            k_ref[:, slice((cb - DN_HEADS) * HEAD_DIM, (cb - DN_HEADS + 1) * HEAD_DIM)] = y
        else:
            v_ref[:, slice((cb - 2 * DN_HEADS) * HEAD_DIM, (cb - 2 * DN_HEADS + 1) * HEAD_DIM)] = y
    ba = ba_ref[...]
    beta = jax.nn.sigmoid(ba)
    xx = ba + gp_ref[1:2, :]
    softplus = jnp.maximum(xx, 0.0) + jnp.log1p(jnp.exp(-jnp.abs(xx)))
    g = -jnp.exp(gp_ref[0:1, :]) * softplus
    gam = _dot_hi(tri_ref[...], g)
    lane = lax.broadcasted_iota(I32, ba.shape, 1)
    bg_ref[...] = jnp.where(lane < DN_HEADS, beta, gam)


def _prep(proj, ba, conv_w, a_log, dt_bias, S):
    T = proj.shape[0]
    tm = min(256, S)
    pad = LANES - 2 * DN_HEADS
    gp = jnp.stack([
        jnp.pad(a_log.astype(F32), (DN_HEADS, pad)),
        jnp.pad(dt_bias.astype(F32), (DN_HEADS, pad)),
    ])
    r = jnp.arange(tm)
    tri = ((r[:, None] >= r[None, :]) & (r[:, None] // CHUNK == r[None, :] // CHUNK)).astype(F32)
    halo_blocks = tm // CONV_HALO
    out = jax.ShapeDtypeStruct((T, DN_WIDTH), F32)
    return pl.pallas_call(
        functools.partial(_prep_body, S // tm, tm),
        grid=(T // tm,),
        in_specs=[
            pl.BlockSpec((tm, 3 * DN_WIDTH), lambda i: (i, 0)),
            pl.BlockSpec((CONV_HALO, 3 * DN_WIDTH), lambda i: (jnp.maximum(i * halo_blocks - 1, 0), 0)),
            pl.BlockSpec((tm, LANES), lambda i: (i, 0)),
            pl.BlockSpec((CONV_WIDTH, 3 * DN_WIDTH), lambda i: (0, 0)),
            pl.BlockSpec((2, LANES), lambda i: (0, 0)),
            pl.BlockSpec((tm, tm), lambda i: (0, 0)),
        ],
        out_specs=[
            pl.BlockSpec((tm, DN_WIDTH), lambda i: (i, 0)),
            pl.BlockSpec((tm, DN_WIDTH), lambda i: (i, 0)),
            pl.BlockSpec((tm, DN_WIDTH), lambda i: (i, 0)),
            pl.BlockSpec((tm, LANES), lambda i: (i, 0)),
        ],
        out_shape=[out, out, out, jax.ShapeDtypeStruct((T, LANES), F32)],
        scratch_shapes=[pltpu.VMEM((tm + CONV_HALO, 3 * DN_WIDTH), F32)],
        compiler_params=_cparams(("parallel",)),
        name="prep",
    )(proj, proj, ba, conv_w.astype(F32), gp, tri)


def _delta_body(nc, q_ref, k_ref, v_ref, z_ref, bg_ref, gt_ref, nw_ref, o_ref, state_ref):
    @pl.when(pl.program_id(1) == 0)
    def _():
        state_ref[...] = jnp.zeros(state_ref.shape, F32)

    ii = lax.broadcasted_iota(I32, (CHUNK, CHUNK), 0)
    jj = lax.broadcasted_iota(I32, (CHUNK, CHUNK), 1)
    incl = ii >= jj
    strict = ii > jj
    eye = (ii == jj).astype(F32)
    nt = (((1,), (1,)), ((), ()))
    tn = (((0,), (0,)), ((), ()))

    def chunk(c, carry):
        r0 = pl.multiple_of(c * CHUNK, CHUNK)
        rows = pl.ds(r0, CHUNK)
        bg = bg_ref[rows, :]
        gt = gt_ref[0, c]
        heads = range(DN_HEADS)
        cs = [slice(h * HEAD_DIM, (h + 1) * HEAD_DIM) for h in heads]
        kh = [k_ref[rows, cs[h]] for h in heads]
        qh = [q_ref[rows, cs[h]] for h in heads]
        bcol = [bg[:, h:h + 1] for h in heads]
        gcol = [bg[:, DN_HEADS + h:DN_HEADS + h + 1] for h in heads]
        glast = [gt[h:h + 1, CHUNK - 1:CHUNK] for h in heads]
        kb = [kh[h] * bcol[h] for h in heads]
        s = [lax.dot_general(jnp.concatenate([kb[h], qh[h]], 0).astype(BF16), kh[h].astype(BF16), nt,
                             preferred_element_type=F32) for h in heads]
        decay = [jnp.where(incl, jnp.exp(jnp.where(incl, gcol[h] - gt[h:h + 1, :], 0.0)), 0.0) for h in heads]
        aqk = [(s[h][CHUNK:] * decay[h]).astype(BF16) for h in heads]
        pw = [jnp.where(strict, -s[h][:CHUNK] * decay[h], 0.0) for h in heads]
        qs = pw
        for _ in range(5):
            pwb = [pw[h].astype(BF16) for h in heads]
            pw = [_dot(pwb[h], pwb[h]) for h in heads]
            qp = [_dot(qs[h].astype(BF16), pw[h].astype(BF16)) for h in heads]
            qs = [qs[h] + pw[h] + qp[h] for h in heads]
        eg = [jnp.exp(gcol[h]) for h in heads]
        rhs = [jnp.concatenate([v_ref[rows, cs[h]] * bcol[h], kb[h] * eg[h]], 1) for h in heads]
        sol = [rhs[h] + _dot(qs[h].astype(BF16), rhs[h].astype(BF16)) for h in heads]
        st = [state_ref[h] for h in heads]
        r = [_dot(jnp.concatenate([sol[h][:, HEAD_DIM:], qh[h] * eg[h]], 0).astype(BF16), st[h].astype(BF16))
             for h in heads]
        v_new = [(sol[h][:, :HEAD_DIM] - r[h][:CHUNK]).astype(BF16) for h in heads]
        o = [r[h][CHUNK:] + _dot(aqk[h], v_new[h]) for h in heads]
        kd = [(kh[h] * jnp.exp(glast[h] - gcol[h])).astype(BF16) for h in heads]
        upd = [lax.dot_general(kd[h], v_new[h], tn, preferred_element_type=F32) for h in heads]
        for h in heads:
            state_ref[h] = st[h] * jnp.exp(glast[h]) + upd[h]
            zz = z_ref[rows, cs[h]]
            y = o[h] * lax.rsqrt(jnp.mean(o[h] * o[h], -1, keepdims=True) + RMS_EPS) * nw_ref[...]
            o_ref[rows, cs[h]] = (y * (zz * jax.nn.sigmoid(zz))).astype(o_ref.dtype)
        return carry

    lax.fori_loop(0, nc, chunk, 0)


def _delta(q, k, v, proj, bg, gt, norm_w, B, S):
    T = q.shape[0]
    sblk = min(512, S)
    nc = sblk // CHUNK
    nsb = S // sblk
    row = lambda b, s: (b * nsb + s, 0)
    return pl.pallas_call(
        functools.partial(_delta_body, nc),
        grid=(B, nsb),
        in_specs=[
            pl.BlockSpec((sblk, DN_WIDTH), row),
            pl.BlockSpec((sblk, DN_WIDTH), row),
            pl.BlockSpec((sblk, DN_WIDTH), row),
            pl.BlockSpec((sblk, DN_WIDTH), lambda b, s: (b * nsb + s, 3)),
            pl.BlockSpec((sblk, LANES), row),
            pl.BlockSpec((1, nc, DN_HEADS, CHUNK), lambda b, s: (b, s, 0, 0)),
            pl.BlockSpec((1, HEAD_DIM), lambda b, s: (0, 0)),
        ],
        out_specs=pl.BlockSpec((sblk, DN_WIDTH), row),
        out_shape=jax.ShapeDtypeStruct((T, DN_WIDTH), BF16),
        scratch_shapes=[pltpu.VMEM((DN_HEADS, HEAD_DIM, HEAD_DIM), F32)],
        compiler_params=_cparams(("parallel", "arbitrary")),
        name="delta",
    )(q, k, v, proj, bg, gt, norm_w.astype(F32).reshape(1, HEAD_DIM))


def _pool_body(tiles_per_seq, tm, u_ref, halo_ref, w_ref, sc_ref, o_ref, us_ref):
    t_in_seq = (pl.program_id(0) % tiles_per_seq) * tm
    first = t_in_seq == 0
    us_ref[0:POOL_HALO, :] = jnp.where(first, 0.0, halo_ref[...])
    us_ref[POOL_HALO:POOL_HALO + tm, :] = u_ref[...]
    tpos = (t_in_seq + lax.broadcasted_iota(I32, (tm, 1), 0) + 1).astype(F32)
    for gi, win in enumerate(POOL_WINDOWS):
        cs = slice(gi * POOL_GROUP_DIM, (gi + 1) * POOL_GROUP_DIM)
        cur = us_ref[POOL_HALO:POOL_HALO + tm, cs]
        wsum = cur
        for j in range(1, win):
            wsum = wsum + us_ref[POOL_HALO - j:POOL_HALO - j + tm, cs]
        d = wsum / jnp.minimum(tpos, float(win)) - cur
        y = _dot(d.astype(BF16), w_ref[gi])
        o_ref[:, cs] = (y * sc_ref[:, cs]).astype(o_ref.dtype)


def _pool(proj, pool_w, pool_scale, S):
    T = proj.shape[0]
    tm = min(256, S)
    ucol = (4 * DN_WIDTH) // POOL_WIDTH
    halo_blocks = tm // POOL_HALO
    return pl.pallas_call(
        functools.partial(_pool_body, S // tm, tm),
        grid=(T // tm,),
        in_specs=[
            pl.BlockSpec((tm, POOL_WIDTH), lambda i: (i, ucol)),
            pl.BlockSpec((POOL_HALO, POOL_WIDTH), lambda i: (jnp.maximum(i * halo_blocks - 1, 0), ucol)),
            pl.BlockSpec((len(POOL_WINDOWS), POOL_GROUP_DIM, POOL_GROUP_DIM), lambda i: (0, 0, 0)),
            pl.BlockSpec((1, POOL_WIDTH), lambda i: (0, 0)),
        ],
        out_specs=pl.BlockSpec((tm, POOL_WIDTH), lambda i: (i, 0)),
        out_shape=jax.ShapeDtypeStruct((T, POOL_WIDTH), BF16),
        scratch_shapes=[pltpu.VMEM((tm + POOL_HALO, POOL_WIDTH), F32)],
        compiler_params=_cparams(("parallel",)),
        name="pool",
    )(proj, proj, pool_w.astype(BF16), pool_scale.astype(F32).reshape(1, POOL_WIDTH))


def _outproj_body(ydn_ref, ypool_ref, w_ref, x_ref, g_ref, b_ref, o_ref):
    mix = _dot(ydn_ref[...], w_ref[0:DN_WIDTH, :]) + _dot(ypool_ref[...], w_ref[DN_WIDTH:, :])
    o_ref[...] = _layer_norm(ALPHA * x_ref[...] + mix, g_ref[...], b_ref[...])


def _outproj(y_dn, y_pool, w_out, x, g, b):
    T = x.shape[0]
    tm = min(512, T)
    row = lambda i: (i, 0)
    const = lambda i: (0, 0)
    return pl.pallas_call(
        _outproj_body,
        grid=(T // tm,),
        in_specs=[
            pl.BlockSpec((tm, DN_WIDTH), row),
            pl.BlockSpec((tm, POOL_WIDTH), row),
            pl.BlockSpec((D_MODEL, D_MODEL), const),
            pl.BlockSpec((tm, D_MODEL), row),
            pl.BlockSpec((1, D_MODEL), const),
            pl.BlockSpec((1, D_MODEL), const),
        ],
        out_specs=pl.BlockSpec((tm, D_MODEL), row),
        out_shape=jax.ShapeDtypeStruct((T, D_MODEL), F32),
        compiler_params=_cparams(("parallel",)),
        name="outproj",
    )(y_dn, y_pool, w_out.astype(BF16), x, g.astype(F32).reshape(1, D_MODEL), b.astype(F32).reshape(1, D_MODEL))


def _router_body(tm, x_ref, wr_ref, br_ref, upper_ref, eidx_ref, gate_ref, rank_ref, cnt_ref, carry_ref):
    @pl.when(pl.program_id(0) == 0)
    def _():
        carry_ref[...] = jnp.zeros(carry_ref.shape, F32)

    nt = (((1,), (1,)), ((), ()))
    logits = lax.dot_general(wr_ref[...], x_ref[...], nt, precision=HIGHEST,
                             preferred_element_type=F32) + br_ref[:, 0:1]
    m = jnp.max(logits, axis=0, keepdims=True)
    e = jnp.exp(logits - m)
    p = e / jnp.sum(e, axis=0, keepdims=True)
    rows = [p[i:i + 1, :] for i in range(N_EXPERTS)]

    scores = []
    for g in range(N_GROUPS):
        a, b, c, d = rows[EXPERTS_PER_GROUP * g:EXPERTS_PER_GROUP * (g + 1)]
        hi1, lo1 = jnp.maximum(a, b), jnp.minimum(a, b)
        hi2, lo2 = jnp.maximum(c, d), jnp.minimum(c, d)
        top1 = jnp.maximum(hi1, hi2)
        top2 = jnp.maximum(jnp.minimum(hi1, hi2), jnp.where(hi1 >= hi2, lo1, lo2))
        scores.append(top1 + top2)
    gsel = jnp.zeros((1, tm), I32)
    best = scores[0]
    for g in range(1, N_GROUPS):
        better = scores[g] > best
        gsel = jnp.where(better, g, gsel)
        best = jnp.where(better, scores[g], best)
    ing = []
    for j in range(EXPERTS_PER_GROUP):
        sel = rows[(N_GROUPS - 1) * EXPERTS_PER_GROUP + j]
        for g in range(N_GROUPS - 2, -1, -1):
            sel = jnp.where(gsel == g, rows[g * EXPERTS_PER_GROUP + j], sel)
        ing.append(sel)
    i1 = jnp.zeros((1, tm), I32)
    p1 = ing[0]
    for j in range(1, EXPERTS_PER_GROUP):
        better = ing[j] > p1
        i1 = jnp.where(better, j, i1)
        p1 = jnp.where(better, ing[j], p1)
    i2 = jnp.zeros((1, tm), I32)
    p2 = jnp.full((1, tm), -1.0, F32)
    for j in range(EXPERTS_PER_GROUP):
        cand = jnp.where(i1 == j, -1.0, ing[j])
        better = cand > p2
        i2 = jnp.where(better, j, i2)
        p2 = jnp.where(better, cand, p2)
    den = p1 + p2
    e0 = gsel * EXPERTS_PER_GROUP + i1
    e1 = gsel * EXPERTS_PER_GROUP + i2
    eidx_ref[0:1, :] = e0
    eidx_ref[1:2, :] = e1
    gate_ref[0:1, :] = p1 / den
    gate_ref[1:2, :] = p2 / den

    er = lax.broadcasted_iota(I32, (N_EXPERTS, tm), 0)
    oh0 = er == e0
    oh1 = er == e1
    oh = jnp.where(oh0 | oh1, 1.0, 0.0)
    before = carry_ref[:, 0:1] + _dot(oh.astype(BF16), upper_ref[...])
    rank_ref[0:1, :] = jnp.sum(jnp.where(oh0, before, 0.0), axis=0, keepdims=True).astype(I32)
    rank_ref[1:2, :] = jnp.sum(jnp.where(oh1, before, 0.0), axis=0, keepdims=True).astype(I32)
    total = carry_ref[...] + jnp.sum(oh, axis=1, keepdims=True)
    carry_ref[...] = total
    cnt_ref[...] = total.astype(I32)


def _router(x1, w_router, b_router):
    T = x1.shape[0]
    tm = min(512, T)
    r = jnp.arange(tm)
    upper = (r[:, None] < r[None, :]).astype(BF16)
    tok = lambda i: (0, i)
    const = lambda i: (0, 0)
    return pl.pallas_call(
        functools.partial(_router_body, tm),
        grid=(T // tm,),
        in_specs=[
            pl.BlockSpec((tm, D_MODEL), lambda i: (i, 0)),
            pl.BlockSpec((N_EXPERTS, D_MODEL), const),
            pl.BlockSpec((N_EXPERTS, LANES), const),
            pl.BlockSpec((tm, tm), const),
        ],
        out_specs=[
            pl.BlockSpec((2, tm), tok),
            pl.BlockSpec((2, tm), tok),
            pl.BlockSpec((2, tm), tok),
            pl.BlockSpec((N_EXPERTS, LANES), const),
        ],
        out_shape=[
            jax.ShapeDtypeStruct((2, T), I32),
            jax.ShapeDtypeStruct((2, T), F32),
            jax.ShapeDtypeStruct((2, T), I32),
            jax.ShapeDtypeStruct((N_EXPERTS, LANES), I32),
        ],
        scratch_shapes=[pltpu.VMEM((N_EXPERTS, LANES), F32)],
        compiler_params=_cparams(("arbitrary",)),
        name="router",
    )(x1, w_router.astype(F32).T, jnp.broadcast_to(b_router.astype(F32)[:, None], (N_EXPERTS, LANES)), upper)


def _dispatch_body(tm, T, n_blocks, pos_ref, pad_start_ref, pad_len_ref, nb_ref, x_ref, xs_ref, zrow_ref, sem, zsem):
    step = pl.program_id(0)

    def zero_copy(dst):
        return pltpu.make_async_copy(zrow_ref.at[pl.ds(0, 1)], xs_ref.at[pl.ds(dst, 1)], zsem)

    @pl.when(step == 0)
    def _():
        zrow_ref[...] = jnp.zeros(zrow_ref.shape, F32)
        for e in range(N_EXPERTS):
            def fill(r, carry, e=e):
                zero_copy(pad_start_ref[e] + r).start()
                return carry
            lax.fori_loop(0, pad_len_ref[e], fill, 0)
        for e in range(N_EXPERTS):
            def drain(r, carry):
                zero_copy(0).wait()
                return carry
            lax.fori_loop(0, pad_len_ref[e], drain, 0)

        def tail_copy(blk):
            return pltpu.make_async_copy(zrow_ref, xs_ref.at[pl.ds(blk * EXPERT_BLOCK, EXPERT_BLOCK)], zsem)

        def tail_fill(blk, carry):
            tail_copy(blk).start()
            return carry

        def tail_drain(blk, carry):
            tail_copy(blk).wait()
            return carry
        lax.fori_loop(nb_ref[0], n_blocks, tail_fill, 0)
        lax.fori_loop(nb_ref[0], n_blocks, tail_drain, 0)

    base = step * tm
    for r in range(tm):
        for k in range(2):
            dst = pos_ref[k * T + base + r]
            pltpu.make_async_copy(x_ref.at[pl.ds(r, 1)], xs_ref.at[pl.ds(dst, 1)], sem).start()
    for k in range(2):
        pltpu.make_async_copy(x_ref, xs_ref.at[pl.ds(0, tm)], sem).wait()


def _dispatch(pos_flat, pad_start, pad_len, nb, x1, n_rows):
    T = x1.shape[0]
    tm = min(256, T)
    return pl.pallas_call(
        functools.partial(_dispatch_body, tm, T, n_rows // EXPERT_BLOCK),
        grid_spec=pltpu.PrefetchScalarGridSpec(
            num_scalar_prefetch=4,
            grid=(T // tm,),
            in_specs=[pl.BlockSpec((tm, D_MODEL), lambda i, *_: (i, 0))],
            out_specs=pl.BlockSpec(memory_space=pl.ANY),
            scratch_shapes=[pltpu.VMEM((EXPERT_BLOCK, D_MODEL), F32), pltpu.SemaphoreType.DMA,
                            pltpu.SemaphoreType.DMA],
        ),
        out_shape=jax.ShapeDtypeStruct((n_rows, D_MODEL), F32),
        compiler_params=_cparams(("arbitrary",)),
        name="dispatch",
    )(pos_flat, pad_start, pad_len, nb, x1)


W_CHUNKS = 8
N_CHUNKS = 3 * W_CHUNKS
N_STAGE = 2
GU_ROWS = D_MODEL // W_CHUNKS
DN_ROWS = D_EXPERT // W_CHUNKS


def _experts_body(layer, be_ref, nb_ref, slot_ref, c0_ref, c1_ref, nxt_ref, first_ref,
                  xs_ref, wg_hbm, wu_hbm, wd_hbm, y_ref, wgu_ref, wdn_ref, stg_a, stg_d, sem_a, sem_d):
    i = pl.program_id(0)

    def chunk_copy(kind, e, idx, b):
        if kind == 0:
            return pltpu.make_async_copy(wg_hbm.at[layer, e, pl.ds(idx * GU_ROWS, GU_ROWS), :], stg_a.at[b], sem_a.at[b])
        if kind == 1:
            return pltpu.make_async_copy(wu_hbm.at[layer, e, pl.ds(idx * GU_ROWS, GU_ROWS), :], stg_a.at[b], sem_a.at[b])
        return pltpu.make_async_copy(wd_hbm.at[layer, e, pl.ds(idx * DN_ROWS, DN_ROWS), :], stg_d.at[b], sem_d.at[b])

    def start_chunk(e, c):
        for kind in range(3):
            @pl.when(c // W_CHUNKS == kind)
            def _():
                chunk_copy(kind, e, c % W_CHUNKS, c % N_STAGE).start()

    def finish_chunk(e, c, slot):
        for kind in range(3):
            @pl.when(c // W_CHUNKS == kind)
            def _():
                idx = c % W_CHUNKS
                b = c % N_STAGE
                chunk_copy(kind, e, idx, b).wait()
                if kind < 2:
                    rows = pl.ds(pl.multiple_of(idx * GU_ROWS, GU_ROWS), GU_ROWS)
                    wgu_ref[slot, kind, rows, :] = stg_a[b].astype(BF16)
                else:
                    rows = pl.ds(pl.multiple_of(idx * DN_ROWS, DN_ROWS), DN_ROWS)
                    wdn_ref[slot, rows, :] = stg_d[b].astype(BF16)

    def stream(e, slot, lo, hi):
        def body(c, carry):
            finish_chunk(e, c, slot)

            @pl.when(c + N_STAGE < N_CHUNKS)
            def _():
                start_chunk(e, c + N_STAGE)
            return carry
        lax.fori_loop(lo, hi, body, 0)

    def prime(e):
        for c in range(N_STAGE):
            start_chunk(e, jnp.int32(c))

    @pl.when(i == 0)
    def _():
        prime(be_ref[0])
        stream(be_ref[0], 0, 0, N_CHUNKS)

    @pl.when(i < nb_ref[0])
    def _():
        s = slot_ref[i]
        ne = nxt_ref[i]

        @pl.when((first_ref[i] == 1) & (ne >= 0))
        def _():
            prime(ne)

        stream(ne, 1 - s, c0_ref[i], c1_ref[i])
        xb = xs_ref[...].astype(BF16)
        g = _dot(xb, wgu_ref[s, 0])
        u = _dot(xb, wgu_ref[s, 1])
        hid = (g * jax.nn.sigmoid(g)) * u
        y_ref[...] = _dot(hid.astype(BF16), wdn_ref[s])

    @pl.when(i >= nb_ref[0])
    def _():
        y_ref[...] = jnp.zeros(y_ref.shape, F32)


def _expert_schedule(block_e, nb, n_blocks):
    idx = jnp.arange(n_blocks, dtype=I32)
    valid = idx < nb[0]
    prev = jnp.concatenate([block_e[:1] - 1, block_e[:-1]])
    first = valid & ((idx == 0) | (block_e != prev))
    run_id = jnp.cumsum(first.astype(I32)) - 1
    run_start = lax.cummax(jnp.where(first, idx, 0))
    run_len = jnp.sum((run_id[:, None] == run_id[None, :]) & valid[None, :], -1).astype(I32)
    j = idx - run_start
    nxt_idx = run_start + run_len
    nxt = jnp.where(valid & (nxt_idx < nb[0]), block_e[jnp.minimum(nxt_idx, n_blocks - 1)], -1).astype(I32)
    n = jnp.maximum(run_len, 1)
    has = nxt >= 0
    c0 = jnp.where(has, j * N_CHUNKS // n, 0).astype(I32)
    c1 = jnp.where(has, (j + 1) * N_CHUNKS // n, 0).astype(I32)
    return (run_id % 2).astype(I32), c0, c1, nxt, first.astype(I32)


def _experts(layer, block_e, nb, xs, wg, wu, wd):
    n_rows = xs.shape[0]
    bm = EXPERT_BLOCK
    n_blocks = n_rows // bm
    slot, c0, c1, nxt, first = _expert_schedule(block_e, nb, n_blocks)
    blk = lambda i, be, nb, *_: (jnp.maximum(jnp.minimum(i, nb[0] - 1), 0), 0)
    hbm = pl.BlockSpec(memory_space=pl.ANY)
    return pl.pallas_call(
        functools.partial(_experts_body, layer),
        grid_spec=pltpu.PrefetchScalarGridSpec(
            num_scalar_prefetch=7,
            grid=(n_blocks,),
            in_specs=[pl.BlockSpec((bm, D_MODEL), blk), hbm, hbm, hbm],
            out_specs=pl.BlockSpec((bm, D_MODEL), lambda i, *_: (i, 0)),
            scratch_shapes=[
                pltpu.VMEM((2, 2, D_MODEL, D_EXPERT), BF16),
                pltpu.VMEM((2, D_EXPERT, D_MODEL), BF16),
                pltpu.VMEM((N_STAGE, GU_ROWS, D_EXPERT), F32),
                pltpu.VMEM((N_STAGE, DN_ROWS, D_MODEL), F32),
                pltpu.SemaphoreType.DMA((N_STAGE,)),
                pltpu.SemaphoreType.DMA((N_STAGE,)),
            ],
        ),
        out_shape=jax.ShapeDtypeStruct((n_rows, D_MODEL), F32),
        compiler_params=_cparams(("arbitrary",)),
        name="experts",
    )(block_e, nb, slot, c0, c1, nxt, first, xs, wg, wu, wd)


def _combine_body(tm, T, n_steps, pos_ref, y_ref, gate_ref, x1_ref, p_ref, wpg_ref, wpp_ref, g_ref, b_ref,
                  o_ref, ob_ref, ya_ref, yb_ref, sem_a, sem_b):
    step = pl.program_id(0)

    def row_copy(tile_base, r, k, buf, sem):
        src = pos_ref[k * T + tile_base + r]
        return pltpu.make_async_copy(y_ref.at[pl.ds(src, 1)], buf.at[k, pl.ds(r, 1)], sem)

    def issue_unrolled(tile, buf, sem):
        for r in range(tm):
            for k in range(2):
                row_copy(tile * tm, r, k, buf, sem).start()

    def wait_tile(buf, sem):
        for k in range(2):
            pltpu.make_async_copy(y_ref.at[pl.ds(0, tm)], buf.at[k], sem).wait()

    def compute(half, buf):
        rows = slice(half * tm, (half + 1) * tm)
        gate = gate_ref[rows, :]
        ffn = buf[0] * gate[:, 0:1] + buf[1] * gate[:, 1:2]
        x2 = _layer_norm(ALPHA * x1_ref[rows, :] + ffn, g_ref[...], b_ref[...])
        gl = _dot(x2.astype(BF16), wpg_ref[...])
        pp = _dot(p_ref[rows, :].astype(BF16), wpp_ref[...])
        x3 = x2 + jax.nn.sigmoid(gl) * pp
        o_ref[rows, :] = x3
        ob_ref[rows, :] = x3.astype(BF16)

    @pl.when(step == 0)
    def _():
        def issue(r, carry):
            for k in range(2):
                row_copy(0, r, k, ya_ref, sem_a).start()
            return carry
        lax.fori_loop(0, tm, issue, 0)

    wait_tile(ya_ref, sem_a)
    issue_unrolled(2 * step + 1, yb_ref, sem_b)
    compute(0, ya_ref)
    wait_tile(yb_ref, sem_b)
    issue_unrolled(jnp.minimum(2 * step + 2, 2 * n_steps - 1), ya_ref, sem_a)
    compute(1, yb_ref)

    @pl.when(step == n_steps - 1)
    def _():
        wait_tile(ya_ref, sem_a)


def _combine(pos_flat, y, gate_t, x1, p, w_ple_gate, w_ple_proj, g, b):
    T = x1.shape[0]
    tm = min(256, T // 2)
    n_steps = T // (2 * tm)
    row = lambda i, pos: (i, 0)
    const = lambda i, pos: (0, 0)
    return pl.pallas_call(
        functools.partial(_combine_body, tm, T, n_steps),
        grid_spec=pltpu.PrefetchScalarGridSpec(
            num_scalar_prefetch=1,
            grid=(n_steps,),
            in_specs=[
                pl.BlockSpec(memory_space=pl.ANY),
                pl.BlockSpec((2 * tm, 2), row),
                pl.BlockSpec((2 * tm, D_MODEL), row),
                pl.BlockSpec((2 * tm, PLE_DIM), row),
                pl.BlockSpec((D_MODEL, D_MODEL), const),
                pl.BlockSpec((PLE_DIM, D_MODEL), const),
                pl.BlockSpec((1, D_MODEL), const),
                pl.BlockSpec((1, D_MODEL), const),
            ],
            out_specs=[
                pl.BlockSpec((2 * tm, D_MODEL), row),
                pl.BlockSpec((2 * tm, D_MODEL), row),
            ],
            scratch_shapes=[pltpu.VMEM((2, tm, D_MODEL), F32), pltpu.VMEM((2, tm, D_MODEL), F32),
                            pltpu.SemaphoreType.DMA, pltpu.SemaphoreType.DMA],
        ),
        out_shape=[
            jax.ShapeDtypeStruct((T, D_MODEL), F32),
            jax.ShapeDtypeStruct((T, D_MODEL), BF16),
        ],
        compiler_params=_cparams(("arbitrary",)),
        name="combine",
    )(pos_flat, y, gate_t, x1, p, w_ple_gate.astype(BF16), w_ple_proj.astype(BF16),
      g.astype(F32).reshape(1, D_MODEL), b.astype(F32).reshape(1, D_MODEL))


def _route_plan(eidx, rank, cnt, n_blocks):
    bm = EXPERT_BLOCK
    counts = cnt[:, 0]
    pcounts = (counts + bm - 1) // bm * bm
    pend = jnp.cumsum(pcounts)
    pstart = pend - pcounts
    onehot = eidx[:, :, None] == jnp.arange(N_EXPERTS, dtype=I32)
    pos = jnp.sum(jnp.where(onehot, pstart, 0), -1) + rank
    nb = (pend[-1] // bm).astype(I32).reshape(1)
    block_start = jnp.arange(n_blocks, dtype=I32) * bm
    block_e = jnp.minimum(jnp.sum(block_start[:, None] >= pend[None, :], -1), N_EXPERTS - 1).astype(I32)
    pad_start = (pstart + counts).astype(I32)
    pad_len = (pcounts - counts).astype(I32)
    return pos.reshape(-1).astype(I32), block_e, nb, pad_start, pad_len


def kernel(x, p, w_in, conv_w, a_log, dt_bias, dn_norm_w, pool_w, pool_scale, w_out, ln1_g, ln1_b,
           w_router, b_router, w_e_gate, w_e_up, w_e_down, ln2_g, ln2_b, w_ple_proj, w_ple_gate):
    B, S, D = x.shape
    T = B * S
    n_rows = 2 * T + N_EXPERTS * EXPERT_BLOCK
    xf = x.reshape(T, D).astype(F32)
    xb = xf.astype(BF16)
    for i in range(DEPTH):
        wi = w_in[i]
        w_main = jnp.concatenate([wi[:, :4 * DN_WIDTH], wi[:, 4 * DN_WIDTH + 2 * DN_HEADS:]], 1).astype(BF16)
        w_ba = jnp.pad(wi[:, 4 * DN_WIDTH:4 * DN_WIDTH + 2 * DN_HEADS],
                       ((0, 0), (0, LANES - 2 * DN_HEADS))).astype(BF16)
        proj, ba = _proj(xb, w_main, w_ba)
        q, k, v, bg = _prep(proj, ba, conv_w[i], a_log[i], dt_bias[i], S)
        gt = bg[:, DN_HEADS:2 * DN_HEADS].reshape(B, S // CHUNK, CHUNK, DN_HEADS).transpose(0, 1, 3, 2)
        y_dn = _delta(q, k, v, proj, bg, gt, dn_norm_w[i], B, S)
        y_pool = _pool(proj, pool_w[i], pool_scale[i], S)
        x1 = _outproj(y_dn, y_pool, w_out[i], xf, ln1_g[i], ln1_b[i])
        eidx, gate, rank, cnt = _router(x1, w_router, b_router)
        pos_flat, block_e, nb, pad_start, pad_len = _route_plan(eidx, rank, cnt, n_rows // EXPERT_BLOCK)
        xs = _dispatch(pos_flat, pad_start, pad_len, nb, x1, n_rows)
        y = _experts(i, block_e, nb, xs, w_e_gate, w_e_up, w_e_down)
        xf, xb = _combine(pos_flat, y, gate.T, x1, p[i].reshape(T, PLE_DIM), w_ple_gate[i], w_ple_proj[i],
                          ln2_g[i], ln2_b[i])
    return xf.reshape(B, S, D).astype(x.dtype)
```

```python
import functools

import jax
import jax.numpy as jnp
from jax import lax
from jax.experimental import pallas as pl
from jax.experimental.pallas import tpu as pltpu

F32 = jnp.float32
BF16 = jnp.bfloat16
I32 = jnp.int32
HIGHEST = lax.Precision.HIGHEST

D_MODEL = 2048
DN_HEADS = 8
HEAD_DIM = 128
DN_WIDTH = DN_HEADS * HEAD_DIM
CONV_WIDTH = 4
CHUNK = 64
POOL_WINDOWS = (2, 4, 8, 16)
POOL_GROUP_DIM = 256
POOL_WIDTH = 1024
N_EXPERTS = 16
N_GROUPS = 4
EXPERTS_PER_GROUP = 4
D_EXPERT = 1024
PLE_DIM = 256
DEPTH = 2
ALPHA = (2.0 * DEPTH) ** 0.25
LN_EPS = 1e-5
RMS_EPS = 1e-6

LANES = 128
MAIN_COLS = 4 * DN_WIDTH + POOL_WIDTH
CONV_HALO = 8
POOL_HALO = 16
EXPERT_BLOCK = 256
VMEM_LIMIT = 56 * 1024 * 1024


def _cparams(sem):
    return pltpu.CompilerParams(dimension_semantics=sem, vmem_limit_bytes=VMEM_LIMIT)


def _dot(a, b):
    return jnp.dot(a, b, preferred_element_type=F32)


def _dot_hi(a, b):
    return jnp.dot(a, b, preferred_element_type=F32, precision=HIGHEST)


def _layer_norm(h, g, b):
    mu = jnp.mean(h, -1, keepdims=True)
    d = h - mu
    var = jnp.mean(d * d, -1, keepdims=True)
    return d * lax.rsqrt(var + LN_EPS) * g + b


def _proj_body(x_ref, w_ref, wba_ref, o_ref, ba_ref):
    x = x_ref[...].astype(BF16)
    o_ref[...] = _dot(x, w_ref[...])

    @pl.when(pl.program_id(1) == 0)
    def _():
        ba_ref[...] = _dot(x, wba_ref[...])


def _proj(xb, w_main, w_ba):
    T = xb.shape[0]
    tm = min(1024, T)
    tn = 1024
    return pl.pallas_call(
        _proj_body,
        grid=(T // tm, MAIN_COLS // tn),
        in_specs=[
            pl.BlockSpec((tm, D_MODEL), lambda i, j: (i, 0)),
            pl.BlockSpec((D_MODEL, tn), lambda i, j: (0, j)),
            pl.BlockSpec((D_MODEL, LANES), lambda i, j: (0, 0)),
        ],
        out_specs=[
            pl.BlockSpec((tm, tn), lambda i, j: (i, j)),
            pl.BlockSpec((tm, LANES), lambda i, j: (i, 0)),
        ],
        out_shape=[
            jax.ShapeDtypeStruct((T, MAIN_COLS), F32),
            jax.ShapeDtypeStruct((T, LANES), F32),
        ],
        compiler_params=_cparams(("parallel", "arbitrary")),
        name="proj",
    )(xb, w_main, w_ba)


def _prep_body(tiles_per_seq, tm, qkv_ref, halo_ref, ba_ref, cw_ref, gp_ref, tri_ref,
               q_ref, k_ref, v_ref, bg_ref, xs_ref):
    first = (pl.program_id(0) % tiles_per_seq) == 0
    xs_ref[0:CONV_HALO, :] = jnp.where(first, 0.0, halo_ref[...])
    xs_ref[CONV_HALO:CONV_HALO + tm, :] = qkv_ref[...]
    base = CONV_HALO - (CONV_WIDTH - 1)
    for cb in range(3 * DN_HEADS):
        cs = slice(cb * HEAD_DIM, (cb + 1) * HEAD_DIM)
        acc = xs_ref[base:base + tm, cs] * cw_ref[0:1, cs]
        for j in range(1, CONV_WIDTH):
            acc = acc + xs_ref[base + j:base + j + tm, cs] * cw_ref[j:j + 1, cs]
        y = acc * jax.nn.sigmoid(acc)
        if cb < 2 * DN_HEADS:
            y = y * lax.rsqrt(jnp.sum(y * y, -1, keepdims=True) + RMS_EPS)
        if cb < DN_HEADS:
            q_ref[:, cs] = y * (HEAD_DIM ** -0.5)
        elif cb < 2 * DN_HEADS:
            k_ref[:, slice((cb - DN_HEADS) * HEAD_DIM, (cb - DN_HEADS + 1) * HEAD_DIM)] = y
        else:
            v_ref[:, slice((cb - 2 * DN_HEADS) * HEAD_DIM, (cb - 2 * DN_HEADS + 1) * HEAD_DIM)] = y
    ba = ba_ref[...]
    beta = jax.nn.sigmoid(ba)
    xx = ba + gp_ref[1:2, :]
    softplus = jnp.maximum(xx, 0.0) + jnp.log1p(jnp.exp(-jnp.abs(xx)))
    g = -jnp.exp(gp_ref[0:1, :]) * softplus
    gam = _dot_hi(tri_ref[...], g)
    lane = lax.broadcasted_iota(I32, ba.shape, 1)
    bg_ref[...] = jnp.where(lane < DN_HEADS, beta, gam)


def _prep(proj, ba, conv_w, a_log, dt_bias, S):
    T = proj.shape[0]
    tm = min(256, S)
    pad = LANES - 2 * DN_HEADS
    gp = jnp.stack([
        jnp.pad(a_log.astype(F32), (DN_HEADS, pad)),
        jnp.pad(dt_bias.astype(F32), (DN_HEADS, pad)),
    ])
    r = jnp.arange(tm)
    tri = ((r[:, None] >= r[None, :]) & (r[:, None] // CHUNK == r[None, :] // CHUNK)).astype(F32)
    halo_blocks = tm // CONV_HALO
    out = jax.ShapeDtypeStruct((T, DN_WIDTH), F32)
    return pl.pallas_call(
        functools.partial(_prep_body, S // tm, tm),
        grid=(T // tm,),
        in_specs=[
            pl.BlockSpec((tm, 3 * DN_WIDTH), lambda i: (i, 0)),
            pl.BlockSpec((CONV_HALO, 3 * DN_WIDTH), lambda i: (jnp.maximum(i * halo_blocks - 1, 0), 0)),
            pl.BlockSpec((tm, LANES), lambda i: (i, 0)),
            pl.BlockSpec((CONV_WIDTH, 3 * DN_WIDTH), lambda i: (0, 0)),
            pl.BlockSpec((2, LANES), lambda i: (0, 0)),
            pl.BlockSpec((tm, tm), lambda i: (0, 0)),
        ],
        out_specs=[
            pl.BlockSpec((tm, DN_WIDTH), lambda i: (i, 0)),
            pl.BlockSpec((tm, DN_WIDTH), lambda i: (i, 0)),
            pl.BlockSpec((tm, DN_WIDTH), lambda i: (i, 0)),
            pl.BlockSpec((tm, LANES), lambda i: (i, 0)),
        ],
        out_shape=[out, out, out, jax.ShapeDtypeStruct((T, LANES), F32)],
        scratch_shapes=[pltpu.VMEM((tm + CONV_HALO, 3 * DN_WIDTH), F32)],
        compiler_params=_cparams(("parallel",)),
        name="prep",
    )(proj, proj, ba, conv_w.astype(F32), gp, tri)


DELTA_BATCH = 2


def _delta_body(nb, nc, q_ref, k_ref, v_ref, z_ref, bg_ref, gt_ref, nw_ref, o_ref, state_ref):
    @pl.when(pl.program_id(1) == 0)
    def _():
        state_ref[...] = jnp.zeros(state_ref.shape, F32)

    ii = lax.broadcasted_iota(I32, (CHUNK, CHUNK), 0)
    jj = lax.broadcasted_iota(I32, (CHUNK, CHUNK), 1)
    incl = ii >= jj
    strict = ii > jj
    nt = (((1,), (1,)), ((), ()))
    chains = [(b, h) for b in range(nb) for h in range(DN_HEADS)]
    cs = [slice(h * HEAD_DIM, (h + 1) * HEAD_DIM) for _, h in chains]
    n = range(len(chains))

    def chunk(c, carry):
        r0 = pl.multiple_of(c * CHUNK, CHUNK)
        rows = pl.ds(r0, CHUNK)
        bg = [bg_ref[b, rows, :] for b in range(nb)]
        gt = [gt_ref[b, c] for b in range(nb)]
        kh = [k_ref[b, rows, cs[i]] for i, (b, _) in enumerate(chains)]
        qh = [q_ref[b, rows, cs[i]] for i, (b, _) in enumerate(chains)]
        bcol = [bg[b][:, h:h + 1] for b, h in chains]
        gcol = [bg[b][:, DN_HEADS + h:DN_HEADS + h + 1] for b, h in chains]
        grow = [gt[b][h:h + 1, :] for b, h in chains]
        glast = [grow[i][:, CHUNK - 1:CHUNK] for i in n]
        kb = [kh[i] * bcol[i] for i in n]
        s = [lax.dot_general(jnp.concatenate([kb[i], qh[i]], 0).astype(BF16), kh[i].astype(BF16), nt,
                             preferred_element_type=F32) for i in n]
        decay = [jnp.where(incl, jnp.exp(jnp.where(incl, gcol[i] - grow[i], 0.0)), 0.0) for i in n]
        aqk = [(s[i][CHUNK:] * decay[i]).astype(BF16) for i in n]
        pw = [jnp.where(strict, -s[i][:CHUNK] * decay[i], 0.0) for i in n]
        qs = pw
        pwb = [pw[i].astype(BF16) for i in n]
        pw = [_dot(pwb[i], pwb[i]) for i in n]
        for _ in range(4):
            pwb = [pw[i].astype(BF16) for i in n]
            both = [_dot(jnp.concatenate([pwb[i], qs[i].astype(BF16)], 0), pwb[i]) for i in n]
            qs = [qs[i] + pw[i] + both[i][CHUNK:] for i in n]
            pw = [both[i][:CHUNK] for i in n]
        qp = [_dot(qs[i].astype(BF16), pw[i].astype(BF16)) for i in n]
        qs = [qs[i] + pw[i] + qp[i] for i in n]
        eg = [jnp.exp(gcol[i]) for i in n]
        rhs = [jnp.concatenate([v_ref[b, rows, cs[i]] * bcol[i], kb[i] * eg[i]], 1)
               for i, (b, _) in enumerate(chains)]
        sol = [rhs[i] + _dot(qs[i].astype(BF16), rhs[i].astype(BF16)) for i in n]
        st = [state_ref[i] for i in n]
        r = [_dot(jnp.concatenate([sol[i][:, HEAD_DIM:], qh[i] * eg[i]], 0).astype(BF16), st[i].astype(BF16))
             for i in n]
        v_new = [(sol[i][:, :HEAD_DIM] - r[i][:CHUNK]).astype(BF16) for i in n]
        kdt = [(kh[i] * jnp.exp(glast[i] - gcol[i])).T.astype(BF16) for i in n]
        ou = [_dot(jnp.concatenate([aqk[i], kdt[i]], 0), v_new[i]) for i in n]
        for i, (b, _) in enumerate(chains):
            state_ref[i] = st[i] * jnp.exp(glast[i]) + ou[i][CHUNK:]
            zz = z_ref[b, rows, cs[i]]
            oi = r[i][CHUNK:] + ou[i][:CHUNK]
            y = oi * lax.rsqrt(jnp.mean(oi * oi, -1, keepdims=True) + RMS_EPS) * nw_ref[...]
            o_ref[b, rows, cs[i]] = (y * (zz * jax.nn.sigmoid(zz))).astype(o_ref.dtype)
        return carry

    lax.fori_loop(0, nc, chunk, 0)


def _delta(q, k, v, proj, bg, gt, norm_w, B, S):
    T = q.shape[0]
    nb = DELTA_BATCH if B % DELTA_BATCH == 0 else 1
    sblk = min(512, S)
    nc = sblk // CHUNK
    seq = lambda a: a.reshape(B, S, a.shape[-1])
    blk = lambda b, s: (b, s, 0)
    y = pl.pallas_call(
        functools.partial(_delta_body, nb, nc),
        grid=(B // nb, S // sblk),
        in_specs=[
            pl.BlockSpec((nb, sblk, DN_WIDTH), blk),
            pl.BlockSpec((nb, sblk, DN_WIDTH), blk),
            pl.BlockSpec((nb, sblk, DN_WIDTH), blk),
            pl.BlockSpec((nb, sblk, DN_WIDTH), lambda b, s: (b, s, 3)),
            pl.BlockSpec((nb, sblk, LANES), blk),
            pl.BlockSpec((nb, nc, DN_HEADS, CHUNK), lambda b, s: (b, s, 0, 0)),
            pl.BlockSpec((1, HEAD_DIM), lambda b, s: (0, 0)),
        ],
        out_specs=pl.BlockSpec((nb, sblk, DN_WIDTH), blk),
        out_shape=jax.ShapeDtypeStruct((B, S, DN_WIDTH), BF16),
        scratch_shapes=[pltpu.VMEM((nb * DN_HEADS, HEAD_DIM, HEAD_DIM), F32)],
        compiler_params=_cparams(("parallel", "arbitrary")),
        name="delta",
    )(seq(q), seq(k), seq(v), seq(proj), seq(bg), gt, norm_w.astype(F32).reshape(1, HEAD_DIM))
    return y.reshape(T, DN_WIDTH)


def _pool_body(tiles_per_seq, tm, u_ref, halo_ref, w_ref, sc_ref, o_ref, us_ref):
    t_in_seq = (pl.program_id(0) % tiles_per_seq) * tm
    first = t_in_seq == 0
    us_ref[0:POOL_HALO, :] = jnp.where(first, 0.0, halo_ref[...])
    us_ref[POOL_HALO:POOL_HALO + tm, :] = u_ref[...]
    tpos = (t_in_seq + lax.broadcasted_iota(I32, (tm, 1), 0) + 1).astype(F32)
    for gi, win in enumerate(POOL_WINDOWS):
        cs = slice(gi * POOL_GROUP_DIM, (gi + 1) * POOL_GROUP_DIM)
        cur = us_ref[POOL_HALO:POOL_HALO + tm, cs]
        wsum = cur
        for j in range(1, win):
            wsum = wsum + us_ref[POOL_HALO - j:POOL_HALO - j + tm, cs]
        d = wsum / jnp.minimum(tpos, float(win)) - cur
        y = _dot(d.astype(BF16), w_ref[gi])
        o_ref[:, cs] = (y * sc_ref[:, cs]).astype(o_ref.dtype)


def _pool(proj, pool_w, pool_scale, S):
    T = proj.shape[0]
    tm = min(256, S)
    ucol = (4 * DN_WIDTH) // POOL_WIDTH
    halo_blocks = tm // POOL_HALO
    return pl.pallas_call(
        functools.partial(_pool_body, S // tm, tm),
        grid=(T // tm,),
        in_specs=[
            pl.BlockSpec((tm, POOL_WIDTH), lambda i: (i, ucol)),
            pl.BlockSpec((POOL_HALO, POOL_WIDTH), lambda i: (jnp.maximum(i * halo_blocks - 1, 0), ucol)),
            pl.BlockSpec((len(POOL_WINDOWS), POOL_GROUP_DIM, POOL_GROUP_DIM), lambda i: (0, 0, 0)),
            pl.BlockSpec((1, POOL_WIDTH), lambda i: (0, 0)),
        ],
        out_specs=pl.BlockSpec((tm, POOL_WIDTH), lambda i: (i, 0)),
        out_shape=jax.ShapeDtypeStruct((T, POOL_WIDTH), BF16),
        scratch_shapes=[pltpu.VMEM((tm + POOL_HALO, POOL_WIDTH), F32)],
        compiler_params=_cparams(("parallel",)),
        name="pool",
    )(proj, proj, pool_w.astype(BF16), pool_scale.astype(F32).reshape(1, POOL_WIDTH))


def _outproj_body(ydn_ref, ypool_ref, w_ref, x_ref, g_ref, b_ref, o_ref):
    mix = _dot(ydn_ref[...], w_ref[0:DN_WIDTH, :]) + _dot(ypool_ref[...], w_ref[DN_WIDTH:, :])
    o_ref[...] = _layer_norm(ALPHA * x_ref[...] + mix, g_ref[...], b_ref[...])


def _outproj(y_dn, y_pool, w_out, x, g, b):
    T = x.shape[0]
    tm = min(512, T)
    row = lambda i: (i, 0)
    const = lambda i: (0, 0)
    return pl.pallas_call(
        _outproj_body,
        grid=(T // tm,),
        in_specs=[
            pl.BlockSpec((tm, DN_WIDTH), row),
            pl.BlockSpec((tm, POOL_WIDTH), row),
            pl.BlockSpec((D_MODEL, D_MODEL), const),
            pl.BlockSpec((tm, D_MODEL), row),
            pl.BlockSpec((1, D_MODEL), const),
            pl.BlockSpec((1, D_MODEL), const),
        ],
        out_specs=pl.BlockSpec((tm, D_MODEL), row),
        out_shape=jax.ShapeDtypeStruct((T, D_MODEL), F32),
        compiler_params=_cparams(("parallel",)),
        name="outproj",
    )(y_dn, y_pool, w_out.astype(BF16), x, g.astype(F32).reshape(1, D_MODEL), b.astype(F32).reshape(1, D_MODEL))


def _router_body(tm, x_ref, wr_ref, br_ref, upper_ref, eidx_ref, gate_ref, rank_ref, cnt_ref, carry_ref):
    @pl.when(pl.program_id(0) == 0)
    def _():
        carry_ref[...] = jnp.zeros(carry_ref.shape, F32)

    nt = (((1,), (1,)), ((), ()))
    logits = lax.dot_general(wr_ref[...], x_ref[...], nt, precision=HIGHEST,
                             preferred_element_type=F32) + br_ref[:, 0:1]
    m = jnp.max(logits, axis=0, keepdims=True)
    e = jnp.exp(logits - m)
    p = e / jnp.sum(e, axis=0, keepdims=True)
    rows = [p[i:i + 1, :] for i in range(N_EXPERTS)]

    scores = []
    for g in range(N_GROUPS):
        a, b, c, d = rows[EXPERTS_PER_GROUP * g:EXPERTS_PER_GROUP * (g + 1)]
        hi1, lo1 = jnp.maximum(a, b), jnp.minimum(a, b)
        hi2, lo2 = jnp.maximum(c, d), jnp.minimum(c, d)
        top1 = jnp.maximum(hi1, hi2)
        top2 = jnp.maximum(jnp.minimum(hi1, hi2), jnp.where(hi1 >= hi2, lo1, lo2))
        scores.append(top1 + top2)
    gsel = jnp.zeros((1, tm), I32)
    best = scores[0]
    for g in range(1, N_GROUPS):
        better = scores[g] > best
        gsel = jnp.where(better, g, gsel)
        best = jnp.where(better, scores[g], best)
    ing = []
    for j in range(EXPERTS_PER_GROUP):
        sel = rows[(N_GROUPS - 1) * EXPERTS_PER_GROUP + j]
        for g in range(N_GROUPS - 2, -1, -1):
            sel = jnp.where(gsel == g, rows[g * EXPERTS_PER_GROUP + j], sel)
        ing.append(sel)
    i1 = jnp.zeros((1, tm), I32)
    p1 = ing[0]
    for j in range(1, EXPERTS_PER_GROUP):
        better = ing[j] > p1
        i1 = jnp.where(better, j, i1)
        p1 = jnp.where(better, ing[j], p1)
    i2 = jnp.zeros((1, tm), I32)
    p2 = jnp.full((1, tm), -1.0, F32)
    for j in range(EXPERTS_PER_GROUP):
        cand = jnp.where(i1 == j, -1.0, ing[j])
        better = cand > p2
        i2 = jnp.where(better, j, i2)
        p2 = jnp.where(better, cand, p2)
    den = p1 + p2
    e0 = gsel * EXPERTS_PER_GROUP + i1
    e1 = gsel * EXPERTS_PER_GROUP + i2
    eidx_ref[0:1, :] = e0
    eidx_ref[1:2, :] = e1
    gate_ref[0:1, :] = p1 / den
    gate_ref[1:2, :] = p2 / den

    er = lax.broadcasted_iota(I32, (N_EXPERTS, tm), 0)
    oh0 = er == e0
    oh1 = er == e1
    oh = jnp.where(oh0 | oh1, 1.0, 0.0)
    before = carry_ref[:, 0:1] + _dot(oh.astype(BF16), upper_ref[...])
    rank_ref[0:1, :] = jnp.sum(jnp.where(oh0, before, 0.0), axis=0, keepdims=True).astype(I32)
    rank_ref[1:2, :] = jnp.sum(jnp.where(oh1, before, 0.0), axis=0, keepdims=True).astype(I32)
    total = carry_ref[...] + jnp.sum(oh, axis=1, keepdims=True)
    carry_ref[...] = total
    cnt_ref[...] = total.astype(I32)


def _router(x1, w_router, b_router):
    T = x1.shape[0]
    tm = min(512, T)
    r = jnp.arange(tm)
    upper = (r[:, None] < r[None, :]).astype(BF16)
    tok = lambda i: (0, i)
    const = lambda i: (0, 0)
    return pl.pallas_call(
        functools.partial(_router_body, tm),
        grid=(T // tm,),
        in_specs=[
            pl.BlockSpec((tm, D_MODEL), lambda i: (i, 0)),
            pl.BlockSpec((N_EXPERTS, D_MODEL), const),
            pl.BlockSpec((N_EXPERTS, LANES), const),
            pl.BlockSpec((tm, tm), const),
        ],
        out_specs=[
            pl.BlockSpec((2, tm), tok),
            pl.BlockSpec((2, tm), tok),
            pl.BlockSpec((2, tm), tok),
            pl.BlockSpec((N_EXPERTS, LANES), const),
        ],
        out_shape=[
            jax.ShapeDtypeStruct((2, T), I32),
            jax.ShapeDtypeStruct((2, T), F32),
            jax.ShapeDtypeStruct((2, T), I32),
            jax.ShapeDtypeStruct((N_EXPERTS, LANES), I32),
        ],
        scratch_shapes=[pltpu.VMEM((N_EXPERTS, LANES), F32)],
        compiler_params=_cparams(("arbitrary",)),
        name="router",
    )(x1, w_router.astype(F32).T, jnp.broadcast_to(b_router.astype(F32)[:, None], (N_EXPERTS, LANES)), upper)


def _dispatch_body(tm, T, n_blocks, pos_ref, pad_start_ref, pad_len_ref, nb_ref, x_ref, xs_ref, zrow_ref, sem, zsem):
    step = pl.program_id(0)

    def zero_copy(dst):
        return pltpu.make_async_copy(zrow_ref.at[pl.ds(0, 1)], xs_ref.at[pl.ds(dst, 1)], zsem)

    @pl.when(step == 0)
    def _():
        zrow_ref[...] = jnp.zeros(zrow_ref.shape, F32)
        for e in range(N_EXPERTS):
            def fill(r, carry, e=e):
                zero_copy(pad_start_ref[e] + r).start()
                return carry
            lax.fori_loop(0, pad_len_ref[e], fill, 0)
        for e in range(N_EXPERTS):
            def drain(r, carry):
                zero_copy(0).wait()
                return carry
            lax.fori_loop(0, pad_len_ref[e], drain, 0)

        def tail_copy(blk):
            return pltpu.make_async_copy(zrow_ref, xs_ref.at[pl.ds(blk * EXPERT_BLOCK, EXPERT_BLOCK)], zsem)

        def tail_fill(blk, carry):
            tail_copy(blk).start()
            return carry

        def tail_drain(blk, carry):
            tail_copy(blk).wait()
            return carry
        lax.fori_loop(nb_ref[0], n_blocks, tail_fill, 0)
        lax.fori_loop(nb_ref[0], n_blocks, tail_drain, 0)

    base = step * tm
    for r in range(tm):
        for k in range(2):
            dst = pos_ref[k * T + base + r]
            pltpu.make_async_copy(x_ref.at[pl.ds(r, 1)], xs_ref.at[pl.ds(dst, 1)], sem).start()
    for k in range(2):
        pltpu.make_async_copy(x_ref, xs_ref.at[pl.ds(0, tm)], sem).wait()


def _dispatch(pos_flat, pad_start, pad_len, nb, x1, n_rows):
    T = x1.shape[0]
    tm = min(256, T)
    return pl.pallas_call(
        functools.partial(_dispatch_body, tm, T, n_rows // EXPERT_BLOCK),
        grid_spec=pltpu.PrefetchScalarGridSpec(
            num_scalar_prefetch=4,
            grid=(T // tm,),
            in_specs=[pl.BlockSpec((tm, D_MODEL), lambda i, *_: (i, 0))],
            out_specs=pl.BlockSpec(memory_space=pl.ANY),
            scratch_shapes=[pltpu.VMEM((EXPERT_BLOCK, D_MODEL), F32), pltpu.SemaphoreType.DMA,
                            pltpu.SemaphoreType.DMA],
        ),
        out_shape=jax.ShapeDtypeStruct((n_rows, D_MODEL), F32),
        compiler_params=_cparams(("arbitrary",)),
        name="dispatch",
    )(pos_flat, pad_start, pad_len, nb, x1)


W_CHUNKS = 8
N_CHUNKS = 3 * W_CHUNKS
N_STAGE = 4
GU_ROWS = D_MODEL // W_CHUNKS
DN_ROWS = D_EXPERT // W_CHUNKS


def _experts_body(layer, be_ref, nb_ref, slot_ref, c0_ref, c1_ref, nxt_ref, first_ref,
                  xs_ref, wg_hbm, wu_hbm, wd_hbm, y_ref, wgu_ref, wdn_ref, stg_a, stg_d, sem_a, sem_d):
    i = pl.program_id(0)

    def chunk_copy(kind, e, idx, b):
        if kind == 0:
            return pltpu.make_async_copy(wg_hbm.at[layer, e, pl.ds(idx * GU_ROWS, GU_ROWS), :], stg_a.at[b], sem_a.at[b])
        if kind == 1:
            return pltpu.make_async_copy(wu_hbm.at[layer, e, pl.ds(idx * GU_ROWS, GU_ROWS), :], stg_a.at[b], sem_a.at[b])
        return pltpu.make_async_copy(wd_hbm.at[layer, e, pl.ds(idx * DN_ROWS, DN_ROWS), :], stg_d.at[b], sem_d.at[b])

    def start_chunk(e, c):
        for kind in range(3):
            @pl.when(c // W_CHUNKS == kind)
            def _():
                chunk_copy(kind, e, c % W_CHUNKS, c % N_STAGE).start()

    def finish_chunk(e, c, slot):
        for kind in range(3):
            @pl.when(c // W_CHUNKS == kind)
            def _():
                idx = c % W_CHUNKS
                b = c % N_STAGE
                chunk_copy(kind, e, idx, b).wait()
                if kind < 2:
                    rows = pl.ds(pl.multiple_of(idx * GU_ROWS, GU_ROWS), GU_ROWS)
                    wgu_ref[slot, kind, rows, :] = stg_a[b].astype(BF16)
                else:
                    rows = pl.ds(pl.multiple_of(idx * DN_ROWS, DN_ROWS), DN_ROWS)
                    wdn_ref[slot, rows, :] = stg_d[b].astype(BF16)

    def stream(e, slot, lo, hi):
        def body(c, carry):
            finish_chunk(e, c, slot)

            @pl.when(c + N_STAGE < N_CHUNKS)
            def _():
                start_chunk(e, c + N_STAGE)
            return carry
        lax.fori_loop(lo, hi, body, 0)

    def prime(e):
        for c in range(N_STAGE):
            start_chunk(e, jnp.int32(c))

    @pl.when(i == 0)
    def _():
        prime(be_ref[0])
        stream(be_ref[0], 0, 0, N_CHUNKS)

    @pl.when(i < nb_ref[0])
    def _():
        s = slot_ref[i]
        ne = nxt_ref[i]

        @pl.when((first_ref[i] == 1) & (ne >= 0))
        def _():
            prime(ne)

        stream(ne, 1 - s, c0_ref[i], c1_ref[i])
        xb = xs_ref[...].astype(BF16)
        g = _dot(xb, wgu_ref[s, 0])
        u = _dot(xb, wgu_ref[s, 1])
        hid = (g * jax.nn.sigmoid(g)) * u
        y_ref[...] = _dot(hid.astype(BF16), wdn_ref[s])

    @pl.when(i >= nb_ref[0])
    def _():
        y_ref[...] = jnp.zeros(y_ref.shape, F32)


def _expert_schedule(block_e, nb, n_blocks):
    idx = jnp.arange(n_blocks, dtype=I32)
    valid = idx < nb[0]
    prev = jnp.concatenate([block_e[:1] - 1, block_e[:-1]])
    first = valid & ((idx == 0) | (block_e != prev))
    run_id = jnp.cumsum(first.astype(I32)) - 1
    run_start = lax.cummax(jnp.where(first, idx, 0))
    run_len = jnp.sum((run_id[:, None] == run_id[None, :]) & valid[None, :], -1).astype(I32)
    j = idx - run_start
    nxt_idx = run_start + run_len
    nxt = jnp.where(valid & (nxt_idx < nb[0]), block_e[jnp.minimum(nxt_idx, n_blocks - 1)], -1).astype(I32)
    n = jnp.maximum(run_len, 1)
    has = nxt >= 0
    c0 = jnp.where(has, j * N_CHUNKS // n, 0).astype(I32)
    c1 = jnp.where(has, (j + 1) * N_CHUNKS // n, 0).astype(I32)
    return (run_id % 2).astype(I32), c0, c1, nxt, first.astype(I32)


def _experts(layer, block_e, nb, xs, wg, wu, wd):
    n_rows = xs.shape[0]
    bm = EXPERT_BLOCK
    n_blocks = n_rows // bm
    slot, c0, c1, nxt, first = _expert_schedule(block_e, nb, n_blocks)
    blk = lambda i, be, nb, *_: (jnp.maximum(jnp.minimum(i, nb[0] - 1), 0), 0)
    hbm = pl.BlockSpec(memory_space=pl.ANY)
    return pl.pallas_call(
        functools.partial(_experts_body, layer),
        grid_spec=pltpu.PrefetchScalarGridSpec(
            num_scalar_prefetch=7,
            grid=(n_blocks,),
            in_specs=[pl.BlockSpec((bm, D_MODEL), blk), hbm, hbm, hbm],
            out_specs=pl.BlockSpec((bm, D_MODEL), lambda i, *_: (i, 0)),
            scratch_shapes=[
                pltpu.VMEM((2, 2, D_MODEL, D_EXPERT), BF16),
                pltpu.VMEM((2, D_EXPERT, D_MODEL), BF16),
                pltpu.VMEM((N_STAGE, GU_ROWS, D_EXPERT), F32),
                pltpu.VMEM((N_STAGE, DN_ROWS, D_MODEL), F32),
                pltpu.SemaphoreType.DMA((N_STAGE,)),
                pltpu.SemaphoreType.DMA((N_STAGE,)),
            ],
        ),
        out_shape=jax.ShapeDtypeStruct((n_rows, D_MODEL), F32),
        compiler_params=_cparams(("arbitrary",)),
        name="experts",
    )(block_e, nb, slot, c0, c1, nxt, first, xs, wg, wu, wd)


def _combine_body(tm, T, n_steps, pos_ref, y_ref, gate_ref, x1_ref, p_ref, wpg_ref, wpp_ref, g_ref, b_ref,
                  o_ref, ob_ref, ya_ref, yb_ref, sem_a, sem_b):
    step = pl.program_id(0)

    def row_copy(tile_base, r, k, buf, sem):
        src = pos_ref[k * T + tile_base + r]
        return pltpu.make_async_copy(y_ref.at[pl.ds(src, 1)], buf.at[k, pl.ds(r, 1)], sem)

    def issue_unrolled(tile, buf, sem):
        for r in range(tm):
            for k in range(2):
                row_copy(tile * tm, r, k, buf, sem).start()

    def wait_tile(buf, sem):
        for k in range(2):
            pltpu.make_async_copy(y_ref.at[pl.ds(0, tm)], buf.at[k], sem).wait()

    def compute(half, buf):
        rows = slice(half * tm, (half + 1) * tm)
        gate = gate_ref[rows, :]
        ffn = buf[0] * gate[:, 0:1] + buf[1] * gate[:, 1:2]
        x2 = _layer_norm(ALPHA * x1_ref[rows, :] + ffn, g_ref[...], b_ref[...])
        gl = _dot(x2.astype(BF16), wpg_ref[...])
        pp = _dot(p_ref[rows, :].astype(BF16), wpp_ref[...])
        x3 = x2 + jax.nn.sigmoid(gl) * pp
        o_ref[rows, :] = x3
        ob_ref[rows, :] = x3.astype(BF16)

    @pl.when(step == 0)
    def _():
        def issue(r, carry):
            for k in range(2):
                row_copy(0, r, k, ya_ref, sem_a).start()
            return carry
        lax.fori_loop(0, tm, issue, 0)

    wait_tile(ya_ref, sem_a)
    issue_unrolled(2 * step + 1, yb_ref, sem_b)
    compute(0, ya_ref)
    wait_tile(yb_ref, sem_b)
    issue_unrolled(jnp.minimum(2 * step + 2, 2 * n_steps - 1), ya_ref, sem_a)
    compute(1, yb_ref)

    @pl.when(step == n_steps - 1)
    def _():
        wait_tile(ya_ref, sem_a)


def _combine(pos_flat, y, gate_t, x1, p, w_ple_gate, w_ple_proj, g, b):
    T = x1.shape[0]
    tm = min(256, T // 2)
    n_steps = T // (2 * tm)
    row = lambda i, pos: (i, 0)
    const = lambda i, pos: (0, 0)
    return pl.pallas_call(
        functools.partial(_combine_body, tm, T, n_steps),
        grid_spec=pltpu.PrefetchScalarGridSpec(
            num_scalar_prefetch=1,
            grid=(n_steps,),
            in_specs=[
                pl.BlockSpec(memory_space=pl.ANY),
                pl.BlockSpec((2 * tm, 2), row),
                pl.BlockSpec((2 * tm, D_MODEL), row),
                pl.BlockSpec((2 * tm, PLE_DIM), row),
                pl.BlockSpec((D_MODEL, D_MODEL), const),
                pl.BlockSpec((PLE_DIM, D_MODEL), const),
                pl.BlockSpec((1, D_MODEL), const),
                pl.BlockSpec((1, D_MODEL), const),
            ],
            out_specs=[
                pl.BlockSpec((2 * tm, D_MODEL), row),
                pl.BlockSpec((2 * tm, D_MODEL), row),
            ],
            scratch_shapes=[pltpu.VMEM((2, tm, D_MODEL), F32), pltpu.VMEM((2, tm, D_MODEL), F32),
                            pltpu.SemaphoreType.DMA, pltpu.SemaphoreType.DMA],
        ),
        out_shape=[
            jax.ShapeDtypeStruct((T, D_MODEL), F32),
            jax.ShapeDtypeStruct((T, D_MODEL), BF16),
        ],
        compiler_params=_cparams(("arbitrary",)),
        name="combine",
    )(pos_flat, y, gate_t, x1, p, w_ple_gate.astype(BF16), w_ple_proj.astype(BF16),
      g.astype(F32).reshape(1, D_MODEL), b.astype(F32).reshape(1, D_MODEL))


def _route_plan(eidx, rank, cnt, n_blocks):
    bm = EXPERT_BLOCK
    counts = cnt[:, 0]
    pcounts = (counts + bm - 1) // bm * bm
    pend = jnp.cumsum(pcounts)
    pstart = pend - pcounts
    onehot = eidx[:, :, None] == jnp.arange(N_EXPERTS, dtype=I32)
    pos = jnp.sum(jnp.where(onehot, pstart, 0), -1) + rank
    nb = (pend[-1] // bm).astype(I32).reshape(1)
    block_start = jnp.arange(n_blocks, dtype=I32) * bm
    block_e = jnp.minimum(jnp.sum(block_start[:, None] >= pend[None, :], -1), N_EXPERTS - 1).astype(I32)
    pad_start = (pstart + counts).astype(I32)
    pad_len = (pcounts - counts).astype(I32)
    return pos.reshape(-1).astype(I32), block_e, nb, pad_start, pad_len


def kernel(x, p, w_in, conv_w, a_log, dt_bias, dn_norm_w, pool_w, pool_scale, w_out, ln1_g, ln1_b,
           w_router, b_router, w_e_gate, w_e_up, w_e_down, ln2_g, ln2_b, w_ple_proj, w_ple_gate):
    B, S, D = x.shape
    T = B * S
    n_rows = 2 * T + N_EXPERTS * EXPERT_BLOCK
    xf = x.reshape(T, D).astype(F32)
    xb = xf
    for i in range(DEPTH):
        wi = w_in[i]
        w_main = jnp.concatenate([wi[:, :4 * DN_WIDTH], wi[:, 4 * DN_WIDTH + 2 * DN_HEADS:]], 1).astype(BF16)
        w_ba = jnp.pad(wi[:, 4 * DN_WIDTH:4 * DN_WIDTH + 2 * DN_HEADS],
                       ((0, 0), (0, LANES - 2 * DN_HEADS))).astype(BF16)
        proj, ba = _proj(xb, w_main, w_ba)
        q, k, v, bg = _prep(proj, ba, conv_w[i], a_log[i], dt_bias[i], S)
        gt = bg[:, DN_HEADS:2 * DN_HEADS].reshape(B, S // CHUNK, CHUNK, DN_HEADS).transpose(0, 1, 3, 2)
        y_dn = _delta(q, k, v, proj, bg, gt, dn_norm_w[i], B, S)
        y_pool = _pool(proj, pool_w[i], pool_scale[i], S)
        x1 = _outproj(y_dn, y_pool, w_out[i], xf, ln1_g[i], ln1_b[i])
        eidx, gate, rank, cnt = _router(x1, w_router, b_router)
        pos_flat, block_e, nb, pad_start, pad_len = _route_plan(eidx, rank, cnt, n_rows // EXPERT_BLOCK)
        xs = _dispatch(pos_flat, pad_start, pad_len, nb, x1, n_rows)
        y = _experts(i, block_e, nb, xs, w_e_gate, w_e_up, w_e_down)
        xf, xb = _combine(pos_flat, y, gate.T, x1, p[i].reshape(T, PLE_DIM), w_ple_gate[i], w_ple_proj[i],
                          ln2_g[i], ln2_b[i])
    return xf.reshape(B, S, D).astype(x.dtype)
```

```python
import functools

import jax
import jax.numpy as jnp
from jax import lax
from jax.experimental import pallas as pl
from jax.experimental.pallas import tpu as pltpu

F32 = jnp.float32
BF16 = jnp.bfloat16
I32 = jnp.int32
HIGHEST = lax.Precision.HIGHEST

D_MODEL = 2048
DN_HEADS = 8
HEAD_DIM = 128
DN_WIDTH = DN_HEADS * HEAD_DIM
CONV_WIDTH = 4
CHUNK = 64
POOL_WINDOWS = (2, 4, 8, 16)
POOL_GROUP_DIM = 256
POOL_WIDTH = 1024
N_EXPERTS = 16
N_GROUPS = 4
EXPERTS_PER_GROUP = 4
D_EXPERT = 1024
PLE_DIM = 256
DEPTH = 2
ALPHA = (2.0 * DEPTH) ** 0.25
LN_EPS = 1e-5
RMS_EPS = 1e-6

LANES = 128
MAIN_COLS = 4 * DN_WIDTH + POOL_WIDTH
CONV_HALO = 8
POOL_HALO = 16
EXPERT_BLOCK = 256
VMEM_LIMIT = 56 * 1024 * 1024


def _cparams(sem):
    return pltpu.CompilerParams(dimension_semantics=sem, vmem_limit_bytes=VMEM_LIMIT)


def _dot(a, b):
    return jnp.dot(a, b, preferred_element_type=F32)


def _dot_hi(a, b):
    return jnp.dot(a, b, preferred_element_type=F32, precision=HIGHEST)


def _layer_norm(h, g, b):
    mu = jnp.mean(h, -1, keepdims=True)
    d = h - mu
    var = jnp.mean(d * d, -1, keepdims=True)
    return d * lax.rsqrt(var + LN_EPS) * g + b


def _proj_body(x_ref, w_ref, wba_ref, o_ref, ba_ref):
    x = x_ref[...].astype(BF16)
    o_ref[...] = _dot(x, w_ref[...])

    @pl.when(pl.program_id(1) == 0)
    def _():
        ba_ref[...] = _dot(x, wba_ref[...])


def _proj(xb, w_main, w_ba):
    T = xb.shape[0]
    tm = min(1024, T)
    tn = 1024
    return pl.pallas_call(
        _proj_body,
        grid=(T // tm, MAIN_COLS // tn),
        in_specs=[
            pl.BlockSpec((tm, D_MODEL), lambda i, j: (i, 0)),
            pl.BlockSpec((D_MODEL, tn), lambda i, j: (0, j)),
            pl.BlockSpec((D_MODEL, LANES), lambda i, j: (0, 0)),
        ],
        out_specs=[
            pl.BlockSpec((tm, tn), lambda i, j: (i, j)),
            pl.BlockSpec((tm, LANES), lambda i, j: (i, 0)),
        ],
        out_shape=[
            jax.ShapeDtypeStruct((T, MAIN_COLS), F32),
            jax.ShapeDtypeStruct((T, LANES), F32),
        ],
        compiler_params=_cparams(("parallel", "arbitrary")),
        name="proj",
    )(xb, w_main, w_ba)


def _prep_body(tiles_per_seq, tm, qkv_ref, halo_ref, ba_ref, cw_ref, gp_ref, tri_ref,
               q_ref, k_ref, v_ref, bg_ref, xs_ref):
    first = (pl.program_id(0) % tiles_per_seq) == 0
    xs_ref[0:CONV_HALO, :] = jnp.where(first, 0.0, halo_ref[...])
    xs_ref[CONV_HALO:CONV_HALO + tm, :] = qkv_ref[...]
    base = CONV_HALO - (CONV_WIDTH - 1)
    for cb in range(3 * DN_HEADS):
        cs = slice(cb * HEAD_DIM, (cb + 1) * HEAD_DIM)
        acc = xs_ref[base:base + tm, cs] * cw_ref[0:1, cs]
        for j in range(1, CONV_WIDTH):
            acc = acc + xs_ref[base + j:base + j + tm, cs] * cw_ref[j:j + 1, cs]
        y = acc * jax.nn.sigmoid(acc)
        if cb < 2 * DN_HEADS:
            y = y * lax.rsqrt(jnp.sum(y * y, -1, keepdims=True) + RMS_EPS)
        if cb < DN_HEADS:
            q_ref[:, cs] = y * (HEAD_DIM ** -0.5)
        elif cb < 2 * DN_HEADS:
            k_ref[:, slice((cb - DN_HEADS) * HEAD_DIM, (cb - DN_HEADS + 1) * HEAD_DIM)] = y
        else:
            v_ref[:, slice((cb - 2 * DN_HEADS) * HEAD_DIM, (cb - 2 * DN_HEADS + 1) * HEAD_DIM)] = y
    ba = ba_ref[...]
    beta = jax.nn.sigmoid(ba)
    xx = ba + gp_ref[1:2, :]
    softplus = jnp.maximum(xx, 0.0) + jnp.log1p(jnp.exp(-jnp.abs(xx)))
    g = -jnp.exp(gp_ref[0:1, :]) * softplus
    gam = _dot_hi(tri_ref[...], g)
    lane = lax.broadcasted_iota(I32, ba.shape, 1)
    bg_ref[...] = jnp.where(lane < DN_HEADS, beta, gam)


def _prep(proj, ba, conv_w, a_log, dt_bias, S):
    T = proj.shape[0]
    tm = min(256, S)
    pad = LANES - 2 * DN_HEADS
    gp = jnp.stack([
        jnp.pad(a_log.astype(F32), (DN_HEADS, pad)),
        jnp.pad(dt_bias.astype(F32), (DN_HEADS, pad)),
    ])
    r = jnp.arange(tm)
    tri = ((r[:, None] >= r[None, :]) & (r[:, None] // CHUNK == r[None, :] // CHUNK)).astype(F32)
    halo_blocks = tm // CONV_HALO
    out = jax.ShapeDtypeStruct((T, DN_WIDTH), F32)
    return pl.pallas_call(
        functools.partial(_prep_body, S // tm, tm),
        grid=(T // tm,),
        in_specs=[
            pl.BlockSpec((tm, 3 * DN_WIDTH), lambda i: (i, 0)),
            pl.BlockSpec((CONV_HALO, 3 * DN_WIDTH), lambda i: (jnp.maximum(i * halo_blocks - 1, 0), 0)),
            pl.BlockSpec((tm, LANES), lambda i: (i, 0)),
            pl.BlockSpec((CONV_WIDTH, 3 * DN_WIDTH), lambda i: (0, 0)),
            pl.BlockSpec((2, LANES), lambda i: (0, 0)),
            pl.BlockSpec((tm, tm), lambda i: (0, 0)),
        ],
        out_specs=[
            pl.BlockSpec((tm, DN_WIDTH), lambda i: (i, 0)),
            pl.BlockSpec((tm, DN_WIDTH), lambda i: (i, 0)),
            pl.BlockSpec((tm, DN_WIDTH), lambda i: (i, 0)),
            pl.BlockSpec((tm, LANES), lambda i: (i, 0)),
        ],
        out_shape=[out, out, out, jax.ShapeDtypeStruct((T, LANES), F32)],
        scratch_shapes=[pltpu.VMEM((tm + CONV_HALO, 3 * DN_WIDTH), F32)],
        compiler_params=_cparams(("parallel",)),
        name="prep",
    )(proj, proj, ba, conv_w.astype(F32), gp, tri)


DELTA_BATCH = 2


def _delta_body(nb, nc, q_ref, k_ref, v_ref, z_ref, bg_ref, gt_ref, nw_ref, o_ref, state_ref):
    @pl.when(pl.program_id(1) == 0)
    def _():
        state_ref[...] = jnp.zeros(state_ref.shape, F32)

    ii = lax.broadcasted_iota(I32, (CHUNK, CHUNK), 0)
    jj = lax.broadcasted_iota(I32, (CHUNK, CHUNK), 1)
    incl = ii >= jj
    strict = ii > jj
    nt = (((1,), (1,)), ((), ()))
    chains = [(b, h) for b in range(nb) for h in range(DN_HEADS)]
    cs = [slice(h * HEAD_DIM, (h + 1) * HEAD_DIM) for _, h in chains]
    n = range(len(chains))

    def chunk(c, carry):
        r0 = pl.multiple_of(c * CHUNK, CHUNK)
        rows = pl.ds(r0, CHUNK)
        bg = [bg_ref[b, rows, :] for b in range(nb)]
        gt = [gt_ref[b, c] for b in range(nb)]
        kh = [k_ref[b, rows, cs[i]] for i, (b, _) in enumerate(chains)]
        qh = [q_ref[b, rows, cs[i]] for i, (b, _) in enumerate(chains)]
        bcol = [bg[b][:, h:h + 1] for b, h in chains]
        gcol = [bg[b][:, DN_HEADS + h:DN_HEADS + h + 1] for b, h in chains]
        grow = [gt[b][h:h + 1, :] for b, h in chains]
        glast = [grow[i][:, CHUNK - 1:CHUNK] for i in n]
        kb = [kh[i] * bcol[i] for i in n]
        s = [lax.dot_general(jnp.concatenate([kb[i], qh[i]], 0).astype(BF16), kh[i].astype(BF16), nt,
                             preferred_element_type=F32) for i in n]
        decay = [jnp.where(incl, jnp.exp(jnp.where(incl, gcol[i] - grow[i], 0.0)), 0.0) for i in n]
        aqk = [(s[i][CHUNK:] * decay[i]).astype(BF16) for i in n]
        pw = [jnp.where(strict, -s[i][:CHUNK] * decay[i], 0.0) for i in n]
        qs = pw
        pwb = [pw[i].astype(BF16) for i in n]
        pw = [_dot(pwb[i], pwb[i]) for i in n]
        for _ in range(4):
            pwb = [pw[i].astype(BF16) for i in n]
            both = [_dot(jnp.concatenate([pwb[i], qs[i].astype(BF16)], 0), pwb[i]) for i in n]
            qs = [qs[i] + pw[i] + both[i][CHUNK:] for i in n]
            pw = [both[i][:CHUNK] for i in n]
        qp = [_dot(qs[i].astype(BF16), pw[i].astype(BF16)) for i in n]
        qs = [qs[i] + pw[i] + qp[i] for i in n]
        eg = [jnp.exp(gcol[i]) for i in n]
        rhs = [jnp.concatenate([v_ref[b, rows, cs[i]] * bcol[i], kb[i] * eg[i]], 1)
               for i, (b, _) in enumerate(chains)]
        sol = [rhs[i] + _dot(qs[i].astype(BF16), rhs[i].astype(BF16)) for i in n]
        st = [state_ref[i] for i in n]
        r = [_dot(jnp.concatenate([sol[i][:, HEAD_DIM:], qh[i] * eg[i]], 0).astype(BF16), st[i].astype(BF16))
             for i in n]
        v_new = [(sol[i][:, :HEAD_DIM] - r[i][:CHUNK]).astype(BF16) for i in n]
        kdt = [(kh[i] * jnp.exp(glast[i] - gcol[i])).T.astype(BF16) for i in n]
        ou = [_dot(jnp.concatenate([aqk[i], kdt[i]], 0), v_new[i]) for i in n]
        for i, (b, _) in enumerate(chains):
            state_ref[i] = st[i] * jnp.exp(glast[i]) + ou[i][CHUNK:]
            zz = z_ref[b, rows, cs[i]]
            oi = r[i][CHUNK:] + ou[i][:CHUNK]
            y = oi * lax.rsqrt(jnp.mean(oi * oi, -1, keepdims=True) + RMS_EPS) * nw_ref[...]
            o_ref[b, rows, cs[i]] = (y * (zz * jax.nn.sigmoid(zz))).astype(o_ref.dtype)
        return carry

    lax.fori_loop(0, nc, chunk, 0)


def _delta(q, k, v, proj, bg, gt, norm_w, B, S):
    T = q.shape[0]
    nb = DELTA_BATCH if B % DELTA_BATCH == 0 else 1
    sblk = min(512, S)
    nc = sblk // CHUNK
    seq = lambda a: a.reshape(B, S, a.shape[-1])
    blk = lambda b, s: (b, s, 0)
    y = pl.pallas_call(
        functools.partial(_delta_body, nb, nc),
        grid=(B // nb, S // sblk),
        in_specs=[
            pl.BlockSpec((nb, sblk, DN_WIDTH), blk),
            pl.BlockSpec((nb, sblk, DN_WIDTH), blk),
            pl.BlockSpec((nb, sblk, DN_WIDTH), blk),
            pl.BlockSpec((nb, sblk, DN_WIDTH), lambda b, s: (b, s, 3)),
            pl.BlockSpec((nb, sblk, LANES), blk),
            pl.BlockSpec((nb, nc, DN_HEADS, CHUNK), lambda b, s: (b, s, 0, 0)),
            pl.BlockSpec((1, HEAD_DIM), lambda b, s: (0, 0)),
        ],
        out_specs=pl.BlockSpec((nb, sblk, DN_WIDTH), blk),
        out_shape=jax.ShapeDtypeStruct((B, S, DN_WIDTH), BF16),
        scratch_shapes=[pltpu.VMEM((nb * DN_HEADS, HEAD_DIM, HEAD_DIM), F32)],
        compiler_params=_cparams(("parallel", "arbitrary")),
        name="delta",
    )(seq(q), seq(k), seq(v), seq(proj), seq(bg), gt, norm_w.astype(F32).reshape(1, HEAD_DIM))
    return y.reshape(T, DN_WIDTH)


def _pool_body(tiles_per_seq, tm, u_ref, halo_ref, w_ref, sc_ref, o_ref, us_ref):
    t_in_seq = (pl.program_id(0) % tiles_per_seq) * tm
    first = t_in_seq == 0
    us_ref[0:POOL_HALO, :] = jnp.where(first, 0.0, halo_ref[...])
    us_ref[POOL_HALO:POOL_HALO + tm, :] = u_ref[...]
    tpos = (t_in_seq + lax.broadcasted_iota(I32, (tm, 1), 0) + 1).astype(F32)
    for gi, win in enumerate(POOL_WINDOWS):
        cs = slice(gi * POOL_GROUP_DIM, (gi + 1) * POOL_GROUP_DIM)
        cur = us_ref[POOL_HALO:POOL_HALO + tm, cs]
        wsum = cur
        for j in range(1, win):
            wsum = wsum + us_ref[POOL_HALO - j:POOL_HALO - j + tm, cs]
        d = wsum / jnp.minimum(tpos, float(win)) - cur
        y = _dot(d.astype(BF16), w_ref[gi])
        o_ref[:, cs] = (y * sc_ref[:, cs]).astype(o_ref.dtype)


def _pool(proj, pool_w, pool_scale, S):
    T = proj.shape[0]
    tm = min(256, S)
    ucol = (4 * DN_WIDTH) // POOL_WIDTH
    halo_blocks = tm // POOL_HALO
    return pl.pallas_call(
        functools.partial(_pool_body, S // tm, tm),
        grid=(T // tm,),
        in_specs=[
            pl.BlockSpec((tm, POOL_WIDTH), lambda i: (i, ucol)),
            pl.BlockSpec((POOL_HALO, POOL_WIDTH), lambda i: (jnp.maximum(i * halo_blocks - 1, 0), ucol)),
            pl.BlockSpec((len(POOL_WINDOWS), POOL_GROUP_DIM, POOL_GROUP_DIM), lambda i: (0, 0, 0)),
            pl.BlockSpec((1, POOL_WIDTH), lambda i: (0, 0)),
        ],
        out_specs=pl.BlockSpec((tm, POOL_WIDTH), lambda i: (i, 0)),
        out_shape=jax.ShapeDtypeStruct((T, POOL_WIDTH), BF16),
        scratch_shapes=[pltpu.VMEM((tm + POOL_HALO, POOL_WIDTH), F32)],
        compiler_params=_cparams(("parallel",)),
        name="pool",
    )(proj, proj, pool_w.astype(BF16), pool_scale.astype(F32).reshape(1, POOL_WIDTH))


OUTPROJ_SPLIT = 2
PACK_COLS = D_MODEL // 2
U32 = jnp.uint32


def _pack_halves(x):
    lo = lax.bitcast_convert_type(x[:, :PACK_COLS].astype(BF16).astype(F32), U32)
    hi = lax.bitcast_convert_type(x[:, PACK_COLS:].astype(BF16).astype(F32), U32)
    return (lo >> 16) | hi


def _unpack_halves(p):
    lo = lax.bitcast_convert_type(p << 16, F32)
    hi = lax.bitcast_convert_type(p & jnp.uint32(0xFFFF0000), F32)
    return lo, hi


def _route_tile(tm, logits, upper_ref, eidx_ref, gate_ref, rank_ref, cnt_ref, carry_ref):
    m = jnp.max(logits, axis=0, keepdims=True)
    e = jnp.exp(logits - m)
    p = e / jnp.sum(e, axis=0, keepdims=True)
    rows = [p[i:i + 1, :] for i in range(N_EXPERTS)]

    scores = []
    for g in range(N_GROUPS):
        a, b, c, d = rows[EXPERTS_PER_GROUP * g:EXPERTS_PER_GROUP * (g + 1)]
        hi1, lo1 = jnp.maximum(a, b), jnp.minimum(a, b)
        hi2, lo2 = jnp.maximum(c, d), jnp.minimum(c, d)
        top1 = jnp.maximum(hi1, hi2)
        top2 = jnp.maximum(jnp.minimum(hi1, hi2), jnp.where(hi1 >= hi2, lo1, lo2))
        scores.append(top1 + top2)
    gsel = jnp.zeros((1, tm), I32)
    best = scores[0]
    for g in range(1, N_GROUPS):
        better = scores[g] > best
        gsel = jnp.where(better, g, gsel)
        best = jnp.where(better, scores[g], best)
    ing = []
    for j in range(EXPERTS_PER_GROUP):
        sel = rows[(N_GROUPS - 1) * EXPERTS_PER_GROUP + j]
        for g in range(N_GROUPS - 2, -1, -1):
            sel = jnp.where(gsel == g, rows[g * EXPERTS_PER_GROUP + j], sel)
        ing.append(sel)
    i1 = jnp.zeros((1, tm), I32)
    p1 = ing[0]
    for j in range(1, EXPERTS_PER_GROUP):
        better = ing[j] > p1
        i1 = jnp.where(better, j, i1)
        p1 = jnp.where(better, ing[j], p1)
    i2 = jnp.zeros((1, tm), I32)
    p2 = jnp.full((1, tm), -1.0, F32)
    for j in range(EXPERTS_PER_GROUP):
        cand = jnp.where(i1 == j, -1.0, ing[j])
        better = cand > p2
        i2 = jnp.where(better, j, i2)
        p2 = jnp.where(better, cand, p2)
    den = p1 + p2
    e0 = gsel * EXPERTS_PER_GROUP + i1
    e1 = gsel * EXPERTS_PER_GROUP + i2
    eidx_ref[0:1, :] = e0
    eidx_ref[1:2, :] = e1
    gate_ref[0:1, :] = p1 / den
    gate_ref[1:2, :] = p2 / den

    er = lax.broadcasted_iota(I32, (N_EXPERTS, tm), 0)
    oh0 = er == e0
    oh1 = er == e1
    oh = jnp.where(oh0 | oh1, 1.0, 0.0)
    before = carry_ref[:, 0:1] + _dot(oh.astype(BF16), upper_ref[...])
    rank_ref[0:1, :] = jnp.sum(jnp.where(oh0, before, 0.0), axis=0, keepdims=True).astype(I32)
    rank_ref[1:2, :] = jnp.sum(jnp.where(oh1, before, 0.0), axis=0, keepdims=True).astype(I32)
    total = carry_ref[...] + jnp.sum(oh, axis=1, keepdims=True)
    carry_ref[...] = total
    cnt_ref[...] = total.astype(I32)


def _outproj_body(tm, ydn_ref, ypool_ref, w_ref, x_ref, g_ref, b_ref, wrh_ref, wrl_ref, br_ref, upper_ref,
                  o_ref, op_ref, eidx_ref, gate_ref, rank_ref, cnt_ref, carry_ref):
    @pl.when(pl.program_id(0) == 0)
    def _():
        carry_ref[...] = jnp.zeros(carry_ref.shape, F32)

    subs = [slice(k * (tm // OUTPROJ_SPLIT), (k + 1) * (tm // OUTPROJ_SPLIT)) for k in range(OUTPROJ_SPLIT)]
    mix = [_dot(ydn_ref[r, :], w_ref[0:DN_WIDTH, :]) + _dot(ypool_ref[r, :], w_ref[DN_WIDTH:, :]) for r in subs]
    x1 = [_layer_norm(ALPHA * x_ref[r, :] + m, g_ref[...], b_ref[...]) for r, m in zip(subs, mix)]
    xh = [x.astype(BF16) for x in x1]
    xl = [(x - h.astype(F32)).astype(BF16) for x, h in zip(x1, xh)]
    lg = [_dot(h, wrh_ref[...]) + (_dot(l, wrh_ref[...]) + _dot(h, wrl_ref[...])) for h, l in zip(xh, xl)]
    for r, x in zip(subs, x1):
        o_ref[r, :] = x
        op_ref[r, :] = _pack_halves(x)
    logits = jnp.concatenate(lg, 0).T[0:N_EXPERTS, :] + br_ref[:, 0:1]
    _route_tile(tm, logits, upper_ref, eidx_ref, gate_ref, rank_ref, cnt_ref, carry_ref)


def _outproj(y_dn, y_pool, w_out, x, g, b, w_router, b_router):
    T = x.shape[0]
    tm = min(512, T)
    r = jnp.arange(tm)
    upper = (r[:, None] < r[None, :]).astype(BF16)
    wr = jnp.pad(w_router.astype(F32), ((0, 0), (0, LANES - N_EXPERTS)))
    wr_hi = wr.astype(BF16)
    wr_lo = (wr - wr_hi.astype(F32)).astype(BF16)
    row = lambda i: (i, 0)
    tok = lambda i: (0, i)
    const = lambda i: (0, 0)
    return pl.pallas_call(
        functools.partial(_outproj_body, tm),
        grid=(T // tm,),
        in_specs=[
            pl.BlockSpec((tm, DN_WIDTH), row),
            pl.BlockSpec((tm, POOL_WIDTH), row),
            pl.BlockSpec((D_MODEL, D_MODEL), const),
            pl.BlockSpec((tm, D_MODEL), row),
            pl.BlockSpec((1, D_MODEL), const),
            pl.BlockSpec((1, D_MODEL), const),
            pl.BlockSpec((D_MODEL, LANES), const),
            pl.BlockSpec((D_MODEL, LANES), const),
            pl.BlockSpec((N_EXPERTS, LANES), const),
            pl.BlockSpec((tm, tm), const),
        ],
        out_specs=[
            pl.BlockSpec((tm, D_MODEL), row),
            pl.BlockSpec((tm, PACK_COLS), row),
            pl.BlockSpec((2, tm), tok),
            pl.BlockSpec((2, tm), tok),
            pl.BlockSpec((2, tm), tok),
            pl.BlockSpec((N_EXPERTS, LANES), const),
        ],
        out_shape=[
            jax.ShapeDtypeStruct((T, D_MODEL), F32),
            jax.ShapeDtypeStruct((T, PACK_COLS), U32),
            jax.ShapeDtypeStruct((2, T), I32),
            jax.ShapeDtypeStruct((2, T), F32),
            jax.ShapeDtypeStruct((2, T), I32),
            jax.ShapeDtypeStruct((N_EXPERTS, LANES), I32),
        ],
        scratch_shapes=[pltpu.VMEM((N_EXPERTS, LANES), F32)],
        compiler_params=_cparams(("arbitrary",)),
        name="outproj",
    )(y_dn, y_pool, w_out.astype(BF16), x, g.astype(F32).reshape(1, D_MODEL), b.astype(F32).reshape(1, D_MODEL),
      wr_hi, wr_lo, jnp.broadcast_to(b_router.astype(F32)[:, None], (N_EXPERTS, LANES)), upper)


def _dispatch_body(tm, T, n_blocks, pos_ref, pad_start_ref, pad_len_ref, nb_ref, x_ref, xs_ref, zrow_ref, sem, zsem):
    step = pl.program_id(0)

    def zero_copy(dst):
        return pltpu.make_async_copy(zrow_ref.at[pl.ds(0, 1)], xs_ref.at[pl.ds(dst, 1)], zsem)

    @pl.when(step == 0)
    def _():
        zrow_ref[...] = jnp.zeros(zrow_ref.shape, zrow_ref.dtype)
        for e in range(N_EXPERTS):
            def fill(r, carry, e=e):
                zero_copy(pad_start_ref[e] + r).start()
                return carry
            lax.fori_loop(0, pad_len_ref[e], fill, 0)
        for e in range(N_EXPERTS):
            def drain(r, carry):
                zero_copy(0).wait()
                return carry
            lax.fori_loop(0, pad_len_ref[e], drain, 0)

        def tail_copy(blk):
            return pltpu.make_async_copy(zrow_ref, xs_ref.at[pl.ds(blk * EXPERT_BLOCK, EXPERT_BLOCK)], zsem)

        def tail_fill(blk, carry):
            tail_copy(blk).start()
            return carry

        def tail_drain(blk, carry):
            tail_copy(blk).wait()
            return carry
        lax.fori_loop(nb_ref[0], n_blocks, tail_fill, 0)
        lax.fori_loop(nb_ref[0], n_blocks, tail_drain, 0)

    base = step * tm
    for r in range(tm):
        for k in range(2):
            dst = pos_ref[k * T + base + r]
            pltpu.make_async_copy(x_ref.at[pl.ds(r, 1)], xs_ref.at[pl.ds(dst, 1)], sem).start()
    for k in range(2):
        pltpu.make_async_copy(x_ref, xs_ref.at[pl.ds(0, tm)], sem).wait()


def _dispatch(pos_flat, pad_start, pad_len, nb, x1, n_rows):
    T = x1.shape[0]
    tm = min(256, T)
    return pl.pallas_call(
        functools.partial(_dispatch_body, tm, T, n_rows // EXPERT_BLOCK),
        grid_spec=pltpu.PrefetchScalarGridSpec(
            num_scalar_prefetch=4,
            grid=(T // tm,),
            in_specs=[pl.BlockSpec((tm, PACK_COLS), lambda i, *_: (i, 0))],
            out_specs=pl.BlockSpec(memory_space=pl.ANY),
            scratch_shapes=[pltpu.VMEM((EXPERT_BLOCK, PACK_COLS), U32), pltpu.SemaphoreType.DMA,
                            pltpu.SemaphoreType.DMA],
        ),
        out_shape=jax.ShapeDtypeStruct((n_rows, PACK_COLS), U32),
        compiler_params=_cparams(("arbitrary",)),
        name="dispatch",
    )(pos_flat, pad_start, pad_len, nb, x1)


W_CHUNKS = 8
N_CHUNKS = 3 * W_CHUNKS
N_STAGE = 4
GU_ROWS = D_MODEL // W_CHUNKS
DN_ROWS = D_EXPERT // W_CHUNKS


def _experts_body(layer, be_ref, nb_ref, slot_ref, c0_ref, c1_ref, nxt_ref, first_ref,
                  xs_ref, wg_hbm, wu_hbm, wd_hbm, y_ref, wgu_ref, wdn_ref, stg_a, stg_d, sem_a, sem_d):
    i = pl.program_id(0)

    def chunk_copy(kind, e, idx, b):
        if kind == 0:
            return pltpu.make_async_copy(wg_hbm.at[layer, e, pl.ds(idx * GU_ROWS, GU_ROWS), :], stg_a.at[b], sem_a.at[b])
        if kind == 1:
            return pltpu.make_async_copy(wu_hbm.at[layer, e, pl.ds(idx * GU_ROWS, GU_ROWS), :], stg_a.at[b], sem_a.at[b])
        return pltpu.make_async_copy(wd_hbm.at[layer, e, pl.ds(idx * DN_ROWS, DN_ROWS), :], stg_d.at[b], sem_d.at[b])

    def start_chunk(e, c):
        for kind in range(3):
            @pl.when(c // W_CHUNKS == kind)
            def _():
                chunk_copy(kind, e, c % W_CHUNKS, c % N_STAGE).start()

    def finish_chunk(e, c, slot):
        for kind in range(3):
            @pl.when(c // W_CHUNKS == kind)
            def _():
                idx = c % W_CHUNKS
                b = c % N_STAGE
                chunk_copy(kind, e, idx, b).wait()
                if kind < 2:
                    rows = pl.ds(pl.multiple_of(idx * GU_ROWS, GU_ROWS), GU_ROWS)
                    wgu_ref[slot, kind, rows, :] = stg_a[b].astype(BF16)
                else:
                    rows = pl.ds(pl.multiple_of(idx * DN_ROWS, DN_ROWS), DN_ROWS)
                    wdn_ref[slot, rows, :] = stg_d[b].astype(BF16)

    def stream(e, slot, lo, hi):
        def body(c, carry):
            finish_chunk(e, c, slot)

            @pl.when(c + N_STAGE < N_CHUNKS)
            def _():
                start_chunk(e, c + N_STAGE)
            return carry
        lax.fori_loop(lo, hi, body, 0)

    def prime(e):
        for c in range(N_STAGE):
            start_chunk(e, jnp.int32(c))

    @pl.when(i == 0)
    def _():
        prime(be_ref[0])
        stream(be_ref[0], 0, 0, N_CHUNKS)

    @pl.when(i < nb_ref[0])
    def _():
        s = slot_ref[i]
        ne = nxt_ref[i]

        @pl.when((first_ref[i] == 1) & (ne >= 0))
        def _():
            prime(ne)

        stream(ne, 1 - s, c0_ref[i], c1_ref[i])
        lo, hi = _unpack_halves(xs_ref[...])
        lo = lo.astype(BF16)
        hi = hi.astype(BF16)
        g = _dot(lo, wgu_ref[s, 0, 0:PACK_COLS, :]) + _dot(hi, wgu_ref[s, 0, PACK_COLS:, :])
        u = _dot(lo, wgu_ref[s, 1, 0:PACK_COLS, :]) + _dot(hi, wgu_ref[s, 1, PACK_COLS:, :])
        hid = (g * jax.nn.sigmoid(g)) * u
        y_ref[...] = _pack_halves(_dot(hid.astype(BF16), wdn_ref[s]))

    @pl.when(i >= nb_ref[0])
    def _():
        y_ref[...] = jnp.zeros(y_ref.shape, y_ref.dtype)


def _expert_schedule(block_e, nb, n_blocks):
    idx = jnp.arange(n_blocks, dtype=I32)
    valid = idx < nb[0]
    prev = jnp.concatenate([block_e[:1] - 1, block_e[:-1]])
    first = valid & ((idx == 0) | (block_e != prev))
    run_id = jnp.cumsum(first.astype(I32)) - 1
    run_start = lax.cummax(jnp.where(first, idx, 0))
    run_len = jnp.sum((run_id[:, None] == run_id[None, :]) & valid[None, :], -1).astype(I32)
    j = idx - run_start
    nxt_idx = run_start + run_len
    nxt = jnp.where(valid & (nxt_idx < nb[0]), block_e[jnp.minimum(nxt_idx, n_blocks - 1)], -1).astype(I32)
    n = jnp.maximum(run_len, 1)
    has = nxt >= 0
    c0 = jnp.where(has, j * N_CHUNKS // n, 0).astype(I32)
    c1 = jnp.where(has, (j + 1) * N_CHUNKS // n, 0).astype(I32)
    return (run_id % 2).astype(I32), c0, c1, nxt, first.astype(I32)


def _experts(layer, block_e, nb, xs, wg, wu, wd):
    n_rows = xs.shape[0]
    bm = EXPERT_BLOCK
    n_blocks = n_rows // bm
    slot, c0, c1, nxt, first = _expert_schedule(block_e, nb, n_blocks)
    blk = lambda i, be, nb, *_: (jnp.maximum(jnp.minimum(i, nb[0] - 1), 0), 0)
    hbm = pl.BlockSpec(memory_space=pl.ANY)
    return pl.pallas_call(
        functools.partial(_experts_body, layer),
        grid_spec=pltpu.PrefetchScalarGridSpec(
            num_scalar_prefetch=7,
            grid=(n_blocks,),
            in_specs=[pl.BlockSpec((bm, PACK_COLS), blk), hbm, hbm, hbm],
            out_specs=pl.BlockSpec((bm, PACK_COLS), lambda i, *_: (i, 0)),
            scratch_shapes=[
                pltpu.VMEM((2, 2, D_MODEL, D_EXPERT), BF16),
                pltpu.VMEM((2, D_EXPERT, D_MODEL), BF16),
                pltpu.VMEM((N_STAGE, GU_ROWS, D_EXPERT), F32),
                pltpu.VMEM((N_STAGE, DN_ROWS, D_MODEL), F32),
                pltpu.SemaphoreType.DMA((N_STAGE,)),
                pltpu.SemaphoreType.DMA((N_STAGE,)),
            ],
        ),
        out_shape=jax.ShapeDtypeStruct((n_rows, PACK_COLS), U32),
        compiler_params=_cparams(("arbitrary",)),
        name="experts",
    )(block_e, nb, slot, c0, c1, nxt, first, xs, wg, wu, wd)


def _combine_body(tm, T, n_steps, pos_ref, y_ref, gate_ref, x1_ref, p_ref, wpg_ref, wpp_ref, g_ref, b_ref,
                  o_ref, ob_ref, ybuf_ref, ffn_ref, sem):
    step = pl.program_id(0)
    slot = step % 2
    rows_per_step = 2 * tm

    def row_copy(base, r, k, s):
        src = pos_ref[k * T + base + r]
        return pltpu.make_async_copy(y_ref.at[pl.ds(src, 1)], ybuf_ref.at[s, k, pl.ds(r, 1)], sem.at[s])

    def wait_slot(s):
        for k in range(2):
            pltpu.make_async_copy(y_ref.at[pl.ds(0, rows_per_step)], ybuf_ref.at[s, k], sem.at[s]).wait()

    @pl.when(step == 0)
    def _():
        def issue(r, carry):
            for k in range(2):
                row_copy(0, r, k, 0).start()
            return carry
        lax.fori_loop(0, rows_per_step, issue, 0)

    wait_slot(slot)
    halves = [slice(h * tm, (h + 1) * tm) for h in range(2)]
    for rows in halves:
        gate = gate_ref[rows, :]
        lo0, hi0 = _unpack_halves(ybuf_ref[slot, 0, rows, :])
        lo1, hi1 = _unpack_halves(ybuf_ref[slot, 1, rows, :])
        g0 = gate[:, 0:1]
        g1 = gate[:, 1:2]
        ffn_ref[rows, 0:PACK_COLS] = lo0 * g0 + lo1 * g1
        ffn_ref[rows, PACK_COLS:] = hi0 * g0 + hi1 * g1
    nxt_base = jnp.minimum(step + 1, n_steps - 1) * rows_per_step
    for r in range(rows_per_step):
        for k in range(2):
            row_copy(nxt_base, r, k, 1 - slot).start()

    x2, gl = [], []
    for rows in halves:
        x2.append(_layer_norm(ALPHA * x1_ref[rows, :] + ffn_ref[rows, :], g_ref[...], b_ref[...]))
        gl.append(_dot(x2[-1].astype(BF16), wpg_ref[...]))
    pp = [_dot(p_ref[rows, :].astype(BF16), wpp_ref[...]) for rows in halves]
    for rows, x, a, b in zip(halves, x2, gl, pp):
        x3 = x + jax.nn.sigmoid(a) * b
        o_ref[rows, :] = x3
        ob_ref[rows, :] = x3.astype(BF16)

    @pl.when(step == n_steps - 1)
    def _():
        wait_slot(1 - slot)


def _combine(pos_flat, y, gate_t, x1, p, w_ple_gate, w_ple_proj, g, b):
    T = x1.shape[0]
    tm = min(256, T // 2)
    n_steps = T // (2 * tm)
    row = lambda i, pos: (i, 0)
    const = lambda i, pos: (0, 0)
    return pl.pallas_call(
        functools.partial(_combine_body, tm, T, n_steps),
        grid_spec=pltpu.PrefetchScalarGridSpec(
            num_scalar_prefetch=1,
            grid=(n_steps,),
            in_specs=[
                pl.BlockSpec(memory_space=pl.ANY),
                pl.BlockSpec((2 * tm, 2), row),
                pl.BlockSpec((2 * tm, D_MODEL), row),
                pl.BlockSpec((2 * tm, PLE_DIM), row),
                pl.BlockSpec((D_MODEL, D_MODEL), const),
                pl.BlockSpec((PLE_DIM, D_MODEL), const),
                pl.BlockSpec((1, D_MODEL), const),
                pl.BlockSpec((1, D_MODEL), const),
            ],
            out_specs=[
                pl.BlockSpec((2 * tm, D_MODEL), row),
                pl.BlockSpec((2 * tm, D_MODEL), row),
            ],
            scratch_shapes=[pltpu.VMEM((2, 2, 2 * tm, PACK_COLS), U32), pltpu.VMEM((2 * tm, D_MODEL), F32),
                            pltpu.SemaphoreType.DMA((2,))],
        ),
        out_shape=[
            jax.ShapeDtypeStruct((T, D_MODEL), F32),
            jax.ShapeDtypeStruct((T, D_MODEL), BF16),
        ],
        compiler_params=_cparams(("arbitrary",)),
        name="combine",
    )(pos_flat, y, gate_t, x1, p, w_ple_gate.astype(BF16), w_ple_proj.astype(BF16),
      g.astype(F32).reshape(1, D_MODEL), b.astype(F32).reshape(1, D_MODEL))


def _route_plan(eidx, rank, cnt, n_blocks):
    bm = EXPERT_BLOCK
    counts = cnt[:, 0]
    pcounts = (counts + bm - 1) // bm * bm
    pend = jnp.cumsum(pcounts)
    pstart = pend - pcounts
    onehot = eidx[:, :, None] == jnp.arange(N_EXPERTS, dtype=I32)
    pos = jnp.sum(jnp.where(onehot, pstart, 0), -1) + rank
    nb = (pend[-1] // bm).astype(I32).reshape(1)
    block_start = jnp.arange(n_blocks, dtype=I32) * bm
    block_e = jnp.minimum(jnp.sum(block_start[:, None] >= pend[None, :], -1), N_EXPERTS - 1).astype(I32)
    pad_start = (pstart + counts).astype(I32)
    pad_len = (pcounts - counts).astype(I32)
    return pos.reshape(-1).astype(I32), block_e, nb, pad_start, pad_len


def kernel(x, p, w_in, conv_w, a_log, dt_bias, dn_norm_w, pool_w, pool_scale, w_out, ln1_g, ln1_b,
           w_router, b_router, w_e_gate, w_e_up, w_e_down, ln2_g, ln2_b, w_ple_proj, w_ple_gate):
    B, S, D = x.shape
    T = B * S
    n_rows = 2 * T + N_EXPERTS * EXPERT_BLOCK
    xf = x.reshape(T, D).astype(F32)
    xb = xf
    for i in range(DEPTH):
        wi = w_in[i]
        w_main = jnp.concatenate([wi[:, :4 * DN_WIDTH], wi[:, 4 * DN_WIDTH + 2 * DN_HEADS:]], 1).astype(BF16)
        w_ba = jnp.pad(wi[:, 4 * DN_WIDTH:4 * DN_WIDTH + 2 * DN_HEADS],
                       ((0, 0), (0, LANES - 2 * DN_HEADS))).astype(BF16)
        proj, ba = _proj(xb, w_main, w_ba)
        q, k, v, bg = _prep(proj, ba, conv_w[i], a_log[i], dt_bias[i], S)
        gt = bg[:, DN_HEADS:2 * DN_HEADS].reshape(B, S // CHUNK, CHUNK, DN_HEADS).transpose(0, 1, 3, 2)
        y_dn = _delta(q, k, v, proj, bg, gt, dn_norm_w[i], B, S)
        y_pool = _pool(proj, pool_w[i], pool_scale[i], S)
        x1, x1p, eidx, gate, rank, cnt = _outproj(y_dn, y_pool, w_out[i], xf, ln1_g[i], ln1_b[i],
                                                  w_router, b_router)
        pos_flat, block_e, nb, pad_start, pad_len = _route_plan(eidx, rank, cnt, n_rows // EXPERT_BLOCK)
        xs = _dispatch(pos_flat, pad_start, pad_len, nb, x1p, n_rows)
        y = _experts(i, block_e, nb, xs, w_e_gate, w_e_up, w_e_down)
        xf, xb = _combine(pos_flat, y, gate.T, x1, p[i].reshape(T, PLE_DIM), w_ple_gate[i], w_ple_proj[i],
                          ln2_g[i], ln2_b[i])
    return xf.reshape(B, S, D).astype(x.dtype)
```

```python
import functools

import jax
import jax.numpy as jnp
from jax import lax
from jax.experimental import pallas as pl
from jax.experimental.pallas import tpu as pltpu

F32 = jnp.float32
BF16 = jnp.bfloat16
I32 = jnp.int32
HIGHEST = lax.Precision.HIGHEST

D_MODEL = 2048
DN_HEADS = 8
HEAD_DIM = 128
DN_WIDTH = DN_HEADS * HEAD_DIM
CONV_WIDTH = 4
CHUNK = 64
POOL_WINDOWS = (2, 4, 8, 16)
POOL_GROUP_DIM = 256
POOL_WIDTH = 1024
N_EXPERTS = 16
N_GROUPS = 4
EXPERTS_PER_GROUP = 4
D_EXPERT = 1024
PLE_DIM = 256
DEPTH = 2
ALPHA = (2.0 * DEPTH) ** 0.25
LN_EPS = 1e-5
RMS_EPS = 1e-6

LANES = 128
MAIN_COLS = 4 * DN_WIDTH + POOL_WIDTH
CONV_HALO = 8
POOL_HALO = 16
EXPERT_BLOCK = 256
VMEM_LIMIT = 56 * 1024 * 1024


def _cparams(sem):
    return pltpu.CompilerParams(dimension_semantics=sem, vmem_limit_bytes=VMEM_LIMIT)


def _dot(a, b):
    return jnp.dot(a, b, preferred_element_type=F32)


def _dot_hi(a, b):
    return jnp.dot(a, b, preferred_element_type=F32, precision=HIGHEST)


def _layer_norm(h, g, b):
    mu = jnp.mean(h, -1, keepdims=True)
    d = h - mu
    var = jnp.mean(d * d, -1, keepdims=True)
    return d * lax.rsqrt(var + LN_EPS) * g + b


PROJ_SPLIT = 8
QKV_COLS = 3 * DN_WIDTH
ZU_COLS = DN_WIDTH + POOL_WIDTH


def _proj_qkv_body(tiles_per_seq, tm, tn, x_ref, w_ref, cw_ref, o_ref, halo_ref, *xs_refs):
    i = pl.program_id(0)
    j = pl.program_id(1)
    sub = tm // PROJ_SPLIT
    first = (i % tiles_per_seq) == 0
    xs_refs[0][0:CONV_HALO, :] = jnp.where(first, 0.0, halo_ref[j])
    for k in range(PROJ_SPLIT):
        raw = _dot(x_ref[k * sub:(k + 1) * sub, :].astype(BF16), w_ref[...])
        xs_refs[k][CONV_HALO:, :] = raw
        if k + 1 < PROJ_SPLIT:
            xs_refs[k + 1][0:CONV_HALO, :] = raw[sub - CONV_HALO:, :]
        else:
            halo_ref[j] = raw[sub - CONV_HALO:, :]
    unit = jnp.where(j == 2, 1.0, 0.0)
    qk_scale = jnp.where(j == 0, HEAD_DIM ** -0.5, jnp.where(j == 1, 1.0, 0.0))
    base = CONV_HALO - (CONV_WIDTH - 1)
    for k in range(PROJ_SPLIT):
        xs_ref = xs_refs[k]
        for cb in range(tn // HEAD_DIM):
            cs = slice(cb * HEAD_DIM, (cb + 1) * HEAD_DIM)
            acc = xs_ref[base:base + sub, cs] * cw_ref[0:1, cs]
            for t in range(1, CONV_WIDTH):
                acc = acc + xs_ref[base + t:base + t + sub, cs] * cw_ref[t:t + 1, cs]
            y = acc * jax.nn.sigmoid(acc)
            inv = lax.rsqrt(jnp.sum(y * y, -1, keepdims=True) + RMS_EPS)
            o_ref[k * sub:(k + 1) * sub, cs] = y * (inv * qk_scale + unit)


def _proj_qkv(x, w_qkv, conv_w, S):
    T = x.shape[0]
    tm = min(1024, S)
    tn = DN_WIDTH
    return pl.pallas_call(
        functools.partial(_proj_qkv_body, S // tm, tm, tn),
        grid=(T // tm, QKV_COLS // tn),
        in_specs=[
            pl.BlockSpec((tm, D_MODEL), lambda i, j: (i, 0)),
            pl.BlockSpec((D_MODEL, tn), lambda i, j: (0, j)),
            pl.BlockSpec((CONV_WIDTH, tn), lambda i, j: (0, j)),
        ],
        out_specs=pl.BlockSpec((tm, tn), lambda i, j: (i, j)),
        out_shape=jax.ShapeDtypeStruct((T, QKV_COLS), F32),
        scratch_shapes=[pltpu.VMEM((QKV_COLS // tn, CONV_HALO, tn), F32)]
        + [pltpu.VMEM((tm // PROJ_SPLIT + CONV_HALO, tn), F32)] * PROJ_SPLIT,
        compiler_params=_cparams(("arbitrary", "arbitrary")),
        name="proj_qkv",
    )(x, w_qkv, conv_w.astype(F32))


def _proj_zu_body(tm, x_ref, w_ref, wba_ref, gp_ref, tri_ref, o_ref, bg_ref):
    x = x_ref[...].astype(BF16)
    o_ref[...] = _dot(x, w_ref[...])

    @pl.when(pl.program_id(1) == 0)
    def _():
        ba = _dot(x, wba_ref[...])
        beta = jax.nn.sigmoid(ba)
        xx = ba + gp_ref[1:2, :]
        softplus = jnp.maximum(xx, 0.0) + jnp.log1p(jnp.exp(-jnp.abs(xx)))
        g = -jnp.exp(gp_ref[0:1, :]) * softplus
        tri = tri_ref[...]
        tb = tri.shape[0]
        gam = jnp.concatenate([_dot_hi(tri, g[r:r + tb, :]) for r in range(0, tm, tb)], 0)
        lane = lax.broadcasted_iota(I32, ba.shape, 1)
        bg_ref[...] = jnp.where(lane < DN_HEADS, beta, gam)


def _proj_zu(x, w_zu, w_ba, a_log, dt_bias):
    T = x.shape[0]
    tm = min(1024, T)
    tn = 1024
    tb = min(256, tm)
    pad = LANES - 2 * DN_HEADS
    gp = jnp.stack([
        jnp.pad(a_log.astype(F32), (DN_HEADS, pad)),
        jnp.pad(dt_bias.astype(F32), (DN_HEADS, pad)),
    ])
    r = jnp.arange(tb)
    tri = ((r[:, None] >= r[None, :]) & (r[:, None] // CHUNK == r[None, :] // CHUNK)).astype(F32)
    return pl.pallas_call(
        functools.partial(_proj_zu_body, tm),
        grid=(T // tm, ZU_COLS // tn),
        in_specs=[
            pl.BlockSpec((tm, D_MODEL), lambda i, j: (i, 0)),
            pl.BlockSpec((D_MODEL, tn), lambda i, j: (0, j)),
            pl.BlockSpec((D_MODEL, LANES), lambda i, j: (0, 0)),
            pl.BlockSpec((2, LANES), lambda i, j: (0, 0)),
            pl.BlockSpec((tb, tb), lambda i, j: (0, 0)),
        ],
        out_specs=[
            pl.BlockSpec((tm, tn), lambda i, j: (i, j)),
            pl.BlockSpec((tm, LANES), lambda i, j: (i, 0)),
        ],
        out_shape=[
            jax.ShapeDtypeStruct((T, ZU_COLS), F32),
            jax.ShapeDtypeStruct((T, LANES), F32),
        ],
        compiler_params=_cparams(("parallel", "arbitrary")),
        name="proj_zu",
    )(x, w_zu, w_ba, gp, tri)


DELTA_BATCH = 2


def _delta_body(nb, nc, q_ref, k_ref, v_ref, z_ref, bg_ref, gt_ref, nw_ref, o_ref, state_ref):
    @pl.when(pl.program_id(1) == 0)
    def _():
        state_ref[...] = jnp.zeros(state_ref.shape, F32)

    ii = lax.broadcasted_iota(I32, (CHUNK, CHUNK), 0)
    jj = lax.broadcasted_iota(I32, (CHUNK, CHUNK), 1)
    incl = ii >= jj
    strict = ii > jj
    nt = (((1,), (1,)), ((), ()))
    chains = [(b, h) for b in range(nb) for h in range(DN_HEADS)]
    cs = [slice(h * HEAD_DIM, (h + 1) * HEAD_DIM) for _, h in chains]
    n = range(len(chains))

    def chunk(c, carry):
        r0 = pl.multiple_of(c * CHUNK, CHUNK)
        rows = pl.ds(r0, CHUNK)
        bg = [bg_ref[b, rows, :] for b in range(nb)]
        gt = [gt_ref[b, c] for b in range(nb)]
        kh = [k_ref[b, rows, cs[i]] for i, (b, _) in enumerate(chains)]
        qh = [q_ref[b, rows, cs[i]] for i, (b, _) in enumerate(chains)]
        bcol = [bg[b][:, h:h + 1] for b, h in chains]
        gcol = [bg[b][:, DN_HEADS + h:DN_HEADS + h + 1] for b, h in chains]
        grow = [gt[b][h:h + 1, :] for b, h in chains]
        glast = [grow[i][:, CHUNK - 1:CHUNK] for i in n]
        kb = [kh[i] * bcol[i] for i in n]
        s = [lax.dot_general(jnp.concatenate([kb[i], qh[i]], 0).astype(BF16), kh[i].astype(BF16), nt,
                             preferred_element_type=F32) for i in n]
        decay = [jnp.where(incl, jnp.exp(jnp.where(incl, gcol[i] - grow[i], 0.0)), 0.0) for i in n]
        aqk = [(s[i][CHUNK:] * decay[i]).astype(BF16) for i in n]
        pw = [jnp.where(strict, -s[i][:CHUNK] * decay[i], 0.0) for i in n]
        qs = pw
        pwb = [pw[i].astype(BF16) for i in n]
        pw = [_dot(pwb[i], pwb[i]) for i in n]
        for _ in range(4):
            pwb = [pw[i].astype(BF16) for i in n]
            both = [_dot(jnp.concatenate([pwb[i], qs[i].astype(BF16)], 0), pwb[i]) for i in n]
            qs = [qs[i] + pw[i] + both[i][CHUNK:] for i in n]
            pw = [both[i][:CHUNK] for i in n]
        qp = [_dot(qs[i].astype(BF16), pw[i].astype(BF16)) for i in n]
        qs = [qs[i] + pw[i] + qp[i] for i in n]
        eg = [jnp.exp(gcol[i]) for i in n]
        rhs = [jnp.concatenate([v_ref[b, rows, cs[i]] * bcol[i], kb[i] * eg[i]], 1)
               for i, (b, _) in enumerate(chains)]
        sol = [rhs[i] + _dot(qs[i].astype(BF16), rhs[i].astype(BF16)) for i in n]
        st = [state_ref[i] for i in n]
        r = [_dot(jnp.concatenate([sol[i][:, HEAD_DIM:], qh[i] * eg[i]], 0).astype(BF16), st[i].astype(BF16))
             for i in n]
        v_new = [(sol[i][:, :HEAD_DIM] - r[i][:CHUNK]).astype(BF16) for i in n]
        kdt = [(kh[i] * jnp.exp(glast[i] - gcol[i])).T.astype(BF16) for i in n]
        ou = [_dot(jnp.concatenate([aqk[i], kdt[i]], 0), v_new[i]) for i in n]
        for i, (b, _) in enumerate(chains):
            state_ref[i] = st[i] * jnp.exp(glast[i]) + ou[i][CHUNK:]
            zz = z_ref[b, rows, cs[i]]
            oi = r[i][CHUNK:] + ou[i][:CHUNK]
            y = oi * lax.rsqrt(jnp.mean(oi * oi, -1, keepdims=True) + RMS_EPS) * nw_ref[...]
            o_ref[b, rows, cs[i]] = (y * (zz * jax.nn.sigmoid(zz))).astype(o_ref.dtype)
        return carry

    lax.fori_loop(0, nc, chunk, 0)


def _delta(qkv, zu, bg, gt, norm_w, B, S):
    T = qkv.shape[0]
    nb = DELTA_BATCH if B % DELTA_BATCH == 0 else 1
    sblk = min(512, S)
    nc = sblk // CHUNK
    seq = lambda a: a.reshape(B, S, a.shape[-1])
    blk = lambda b, s: (b, s, 0)
    y = pl.pallas_call(
        functools.partial(_delta_body, nb, nc),
        grid=(B // nb, S // sblk),
        in_specs=[
            pl.BlockSpec((nb, sblk, DN_WIDTH), blk),
            pl.BlockSpec((nb, sblk, DN_WIDTH), lambda b, s: (b, s, 1)),
            pl.BlockSpec((nb, sblk, DN_WIDTH), lambda b, s: (b, s, 2)),
            pl.BlockSpec((nb, sblk, DN_WIDTH), blk),
            pl.BlockSpec((nb, sblk, LANES), blk),
            pl.BlockSpec((nb, nc, DN_HEADS, CHUNK), lambda b, s: (b, s, 0, 0)),
            pl.BlockSpec((1, HEAD_DIM), lambda b, s: (0, 0)),
        ],
        out_specs=pl.BlockSpec((nb, sblk, DN_WIDTH), blk),
        out_shape=jax.ShapeDtypeStruct((B, S, DN_WIDTH), BF16),
        scratch_shapes=[pltpu.VMEM((nb * DN_HEADS, HEAD_DIM, HEAD_DIM), F32)],
        compiler_params=_cparams(("parallel", "arbitrary")),
        name="delta",
    )(seq(qkv), seq(qkv), seq(qkv), seq(zu), seq(bg), gt, norm_w.astype(F32).reshape(1, HEAD_DIM))
    return y.reshape(T, DN_WIDTH)


def _pool_body(tiles_per_seq, tm, u_ref, halo_ref, w_ref, sc_ref, o_ref, us_ref):
    t_in_seq = (pl.program_id(0) % tiles_per_seq) * tm
    first = t_in_seq == 0
    us_ref[0:POOL_HALO, :] = jnp.where(first, 0.0, halo_ref[...])
    us_ref[POOL_HALO:POOL_HALO + tm, :] = u_ref[...]
    tpos = (t_in_seq + lax.broadcasted_iota(I32, (tm, 1), 0) + 1).astype(F32)
    for gi, win in enumerate(POOL_WINDOWS):
        cs = slice(gi * POOL_GROUP_DIM, (gi + 1) * POOL_GROUP_DIM)
        cur = us_ref[POOL_HALO:POOL_HALO + tm, cs]
        wsum = cur
        for j in range(1, win):
            wsum = wsum + us_ref[POOL_HALO - j:POOL_HALO - j + tm, cs]
        d = wsum / jnp.minimum(tpos, float(win)) - cur
        y = _dot(d.astype(BF16), w_ref[gi])
        o_ref[:, cs] = (y * sc_ref[:, cs]).astype(o_ref.dtype)


def _pool(proj, pool_w, pool_scale, S):
    T = proj.shape[0]
    tm = min(256, S)
    ucol = DN_WIDTH // POOL_WIDTH
    halo_blocks = tm // POOL_HALO
    return pl.pallas_call(
        functools.partial(_pool_body, S // tm, tm),
        grid=(T // tm,),
        in_specs=[
            pl.BlockSpec((tm, POOL_WIDTH), lambda i: (i, ucol)),
            pl.BlockSpec((POOL_HALO, POOL_WIDTH), lambda i: (jnp.maximum(i * halo_blocks - 1, 0), ucol)),
            pl.BlockSpec((len(POOL_WINDOWS), POOL_GROUP_DIM, POOL_GROUP_DIM), lambda i: (0, 0, 0)),
            pl.BlockSpec((1, POOL_WIDTH), lambda i: (0, 0)),
        ],
        out_specs=pl.BlockSpec((tm, POOL_WIDTH), lambda i: (i, 0)),
        out_shape=jax.ShapeDtypeStruct((T, POOL_WIDTH), BF16),
        scratch_shapes=[pltpu.VMEM((tm + POOL_HALO, POOL_WIDTH), F32)],
        compiler_params=_cparams(("parallel",)),
        name="pool",
    )(proj, proj, pool_w.astype(BF16), pool_scale.astype(F32).reshape(1, POOL_WIDTH))


OUTPROJ_SPLIT = 2
PACK_COLS = D_MODEL // 2
U32 = jnp.uint32


def _pack_halves(x):
    lo = lax.bitcast_convert_type(x[:, :PACK_COLS].astype(BF16).astype(F32), U32)
    hi = lax.bitcast_convert_type(x[:, PACK_COLS:].astype(BF16).astype(F32), U32)
    return (lo >> 16) | hi


def _unpack_halves(p):
    lo = lax.bitcast_convert_type(p << 16, F32)
    hi = lax.bitcast_convert_type(p & jnp.uint32(0xFFFF0000), F32)
    return lo, hi


def _route_tile(tm, logits, upper_ref, eidx_ref, gate_ref, rank_ref, cnt_ref, carry_ref):
    m = jnp.max(logits, axis=0, keepdims=True)
    e = jnp.exp(logits - m)
    p = e / jnp.sum(e, axis=0, keepdims=True)
    rows = [p[i:i + 1, :] for i in range(N_EXPERTS)]

    scores = []
    for g in range(N_GROUPS):
        a, b, c, d = rows[EXPERTS_PER_GROUP * g:EXPERTS_PER_GROUP * (g + 1)]
        hi1, lo1 = jnp.maximum(a, b), jnp.minimum(a, b)
        hi2, lo2 = jnp.maximum(c, d), jnp.minimum(c, d)
        top1 = jnp.maximum(hi1, hi2)
        top2 = jnp.maximum(jnp.minimum(hi1, hi2), jnp.where(hi1 >= hi2, lo1, lo2))
        scores.append(top1 + top2)
    gsel = jnp.zeros((1, tm), I32)
    best = scores[0]
    for g in range(1, N_GROUPS):
        better = scores[g] > best
        gsel = jnp.where(better, g, gsel)
        best = jnp.where(better, scores[g], best)
    ing = []
    for j in range(EXPERTS_PER_GROUP):
        sel = rows[(N_GROUPS - 1) * EXPERTS_PER_GROUP + j]
        for g in range(N_GROUPS - 2, -1, -1):
            sel = jnp.where(gsel == g, rows[g * EXPERTS_PER_GROUP + j], sel)
        ing.append(sel)
    i1 = jnp.zeros((1, tm), I32)
    p1 = ing[0]
    for j in range(1, EXPERTS_PER_GROUP):
        better = ing[j] > p1
        i1 = jnp.where(better, j, i1)
        p1 = jnp.where(better, ing[j], p1)
    i2 = jnp.zeros((1, tm), I32)
    p2 = jnp.full((1, tm), -1.0, F32)
    for j in range(EXPERTS_PER_GROUP):
        cand = jnp.where(i1 == j, -1.0, ing[j])
        better = cand > p2
        i2 = jnp.where(better, j, i2)
        p2 = jnp.where(better, cand, p2)
    den = p1 + p2
    e0 = gsel * EXPERTS_PER_GROUP + i1
    e1 = gsel * EXPERTS_PER_GROUP + i2
    eidx_ref[0:1, :] = e0
    eidx_ref[1:2, :] = e1
    gate_ref[0:1, :] = p1 / den
    gate_ref[1:2, :] = p2 / den

    er = lax.broadcasted_iota(I32, (N_EXPERTS, tm), 0)
    oh0 = er == e0
    oh1 = er == e1
    oh = jnp.where(oh0 | oh1, 1.0, 0.0)
    before = carry_ref[:, 0:1] + _dot(oh.astype(BF16), upper_ref[...])
    rank_ref[0:1, :] = jnp.sum(jnp.where(oh0, before, 0.0), axis=0, keepdims=True).astype(I32)
    rank_ref[1:2, :] = jnp.sum(jnp.where(oh1, before, 0.0), axis=0, keepdims=True).astype(I32)
    total = carry_ref[...] + jnp.sum(oh, axis=1, keepdims=True)
    carry_ref[...] = total
    cnt_ref[...] = total.astype(I32)


def _outproj_body(tm, ydn_ref, ypool_ref, w_ref, x_ref, g_ref, b_ref, wrh_ref, wrl_ref, br_ref, upper_ref,
                  o_ref, op_ref, eidx_ref, gate_ref, rank_ref, cnt_ref, carry_ref):
    @pl.when(pl.program_id(0) == 0)
    def _():
        carry_ref[...] = jnp.zeros(carry_ref.shape, F32)

    subs = [slice(k * (tm // OUTPROJ_SPLIT), (k + 1) * (tm // OUTPROJ_SPLIT)) for k in range(OUTPROJ_SPLIT)]
    mix = [_dot(ydn_ref[r, :], w_ref[0:DN_WIDTH, :]) + _dot(ypool_ref[r, :], w_ref[DN_WIDTH:, :]) for r in subs]
    x1 = [_layer_norm(ALPHA * x_ref[r, :] + m, g_ref[...], b_ref[...]) for r, m in zip(subs, mix)]
    xh = [x.astype(BF16) for x in x1]
    xl = [(x - h.astype(F32)).astype(BF16) for x, h in zip(x1, xh)]
    lg = [_dot(h, wrh_ref[...]) + (_dot(l, wrh_ref[...]) + _dot(h, wrl_ref[...])) for h, l in zip(xh, xl)]
    for r, x in zip(subs, x1):
        o_ref[r, :] = x
        op_ref[r, :] = _pack_halves(x)
    logits = jnp.concatenate(lg, 0).T[0:N_EXPERTS, :] + br_ref[:, 0:1]
    _route_tile(tm, logits, upper_ref, eidx_ref, gate_ref, rank_ref, cnt_ref, carry_ref)


def _outproj(y_dn, y_pool, w_out, x, g, b, w_router, b_router):
    T = x.shape[0]
    tm = min(512, T)
    r = jnp.arange(tm)
    upper = (r[:, None] < r[None, :]).astype(BF16)
    wr = jnp.pad(w_router.astype(F32), ((0, 0), (0, LANES - N_EXPERTS)))
    wr_hi = wr.astype(BF16)
    wr_lo = (wr - wr_hi.astype(F32)).astype(BF16)
    row = lambda i: (i, 0)
    tok = lambda i: (0, i)
    const = lambda i: (0, 0)
    return pl.pallas_call(
        functools.partial(_outproj_body, tm),
        grid=(T // tm,),
        in_specs=[
            pl.BlockSpec((tm, DN_WIDTH), row),
            pl.BlockSpec((tm, POOL_WIDTH), row),
            pl.BlockSpec((D_MODEL, D_MODEL), const),
            pl.BlockSpec((tm, D_MODEL), row),
            pl.BlockSpec((1, D_MODEL), const),
            pl.BlockSpec((1, D_MODEL), const),
            pl.BlockSpec((D_MODEL, LANES), const),
            pl.BlockSpec((D_MODEL, LANES), const),
            pl.BlockSpec((N_EXPERTS, LANES), const),
            pl.BlockSpec((tm, tm), const),
        ],
        out_specs=[
            pl.BlockSpec((tm, D_MODEL), row),
            pl.BlockSpec((tm, PACK_COLS), row),
            pl.BlockSpec((2, tm), tok),
            pl.BlockSpec((2, tm), tok),
            pl.BlockSpec((2, tm), tok),
            pl.BlockSpec((N_EXPERTS, LANES), const),
        ],
        out_shape=[
            jax.ShapeDtypeStruct((T, D_MODEL), F32),
            jax.ShapeDtypeStruct((T, PACK_COLS), U32),
            jax.ShapeDtypeStruct((2, T), I32),
            jax.ShapeDtypeStruct((2, T), F32),
            jax.ShapeDtypeStruct((2, T), I32),
            jax.ShapeDtypeStruct((N_EXPERTS, LANES), I32),
        ],
        scratch_shapes=[pltpu.VMEM((N_EXPERTS, LANES), F32)],
        compiler_params=_cparams(("arbitrary",)),
        name="outproj",
    )(y_dn, y_pool, w_out.astype(BF16), x, g.astype(F32).reshape(1, D_MODEL), b.astype(F32).reshape(1, D_MODEL),
      wr_hi, wr_lo, jnp.broadcast_to(b_router.astype(F32)[:, None], (N_EXPERTS, LANES)), upper)


def _dispatch_body(tm, T, n_blocks, pos_ref, pad_start_ref, pad_len_ref, nb_ref, x_ref, xs_ref, zrow_ref, sem, zsem):
    step = pl.program_id(0)

    def zero_copy(dst):
        return pltpu.make_async_copy(zrow_ref.at[pl.ds(0, 1)], xs_ref.at[pl.ds(dst, 1)], zsem)

    @pl.when(step == 0)
    def _():
        zrow_ref[...] = jnp.zeros(zrow_ref.shape, zrow_ref.dtype)
        for e in range(N_EXPERTS):
            def fill(r, carry, e=e):
                zero_copy(pad_start_ref[e] + r).start()
                return carry
            lax.fori_loop(0, pad_len_ref[e], fill, 0)
        for e in range(N_EXPERTS):
            def drain(r, carry):
                zero_copy(0).wait()
                return carry
            lax.fori_loop(0, pad_len_ref[e], drain, 0)

        def tail_copy(blk):
            return pltpu.make_async_copy(zrow_ref, xs_ref.at[pl.ds(blk * EXPERT_BLOCK, EXPERT_BLOCK)], zsem)

        def tail_fill(blk, carry):
            tail_copy(blk).start()
            return carry

        def tail_drain(blk, carry):
            tail_copy(blk).wait()
            return carry
        lax.fori_loop(nb_ref[0], n_blocks, tail_fill, 0)
        lax.fori_loop(nb_ref[0], n_blocks, tail_drain, 0)

    base = step * tm
    for r in range(tm):
        for k in range(2):
            dst = pos_ref[k * T + base + r]
            pltpu.make_async_copy(x_ref.at[pl.ds(r, 1)], xs_ref.at[pl.ds(dst, 1)], sem).start(priority=k)
    for k in range(2):
        pltpu.make_async_copy(x_ref, xs_ref.at[pl.ds(0, tm)], sem).wait()


def _dispatch(pos_flat, pad_start, pad_len, nb, x1, n_rows):
    T = x1.shape[0]
    tm = min(256, T)
    return pl.pallas_call(
        functools.partial(_dispatch_body, tm, T, n_rows // EXPERT_BLOCK),
        grid_spec=pltpu.PrefetchScalarGridSpec(
            num_scalar_prefetch=4,
            grid=(T // tm,),
            in_specs=[pl.BlockSpec((tm, PACK_COLS), lambda i, *_: (i, 0))],
            out_specs=pl.BlockSpec(memory_space=pl.ANY),
            scratch_shapes=[pltpu.VMEM((EXPERT_BLOCK, PACK_COLS), U32), pltpu.SemaphoreType.DMA,
                            pltpu.SemaphoreType.DMA],
        ),
        out_shape=jax.ShapeDtypeStruct((n_rows, PACK_COLS), U32),
        compiler_params=_cparams(("arbitrary",)),
        name="dispatch",
    )(pos_flat, pad_start, pad_len, nb, x1)


W_CHUNKS = 8
N_CHUNKS = 3 * W_CHUNKS
N_STAGE = 4
GU_ROWS = D_MODEL // W_CHUNKS
DN_ROWS = D_EXPERT // W_CHUNKS


def _experts_body(layer, be_ref, nb_ref, slot_ref, c0_ref, c1_ref, nxt_ref, first_ref,
                  xs_ref, wg_hbm, wu_hbm, wd_hbm, y_ref, wgu_ref, wdn_ref, stg_a, stg_d, sem_a, sem_d):
    i = pl.program_id(0)

    def chunk_copy(kind, e, idx, b):
        if kind == 0:
            return pltpu.make_async_copy(wg_hbm.at[layer, e, pl.ds(idx * GU_ROWS, GU_ROWS), :], stg_a.at[b], sem_a.at[b])
        if kind == 1:
            return pltpu.make_async_copy(wu_hbm.at[layer, e, pl.ds(idx * GU_ROWS, GU_ROWS), :], stg_a.at[b], sem_a.at[b])
        return pltpu.make_async_copy(wd_hbm.at[layer, e, pl.ds(idx * DN_ROWS, DN_ROWS), :], stg_d.at[b], sem_d.at[b])

    def start_chunk(e, c):
        for kind in range(3):
            @pl.when(c // W_CHUNKS == kind)
            def _():
                chunk_copy(kind, e, c % W_CHUNKS, c % N_STAGE).start()

    def finish_chunk(e, c, slot):
        for kind in range(3):
            @pl.when(c // W_CHUNKS == kind)
            def _():
                idx = c % W_CHUNKS
                b = c % N_STAGE
                chunk_copy(kind, e, idx, b).wait()
                if kind < 2:
                    rows = pl.ds(pl.multiple_of(idx * GU_ROWS, GU_ROWS), GU_ROWS)
                    wgu_ref[slot, kind, rows, :] = stg_a[b].astype(BF16)
                else:
                    rows = pl.ds(pl.multiple_of(idx * DN_ROWS, DN_ROWS), DN_ROWS)
                    wdn_ref[slot, rows, :] = stg_d[b].astype(BF16)

    def stream(e, slot, lo, hi):
        def body(c, carry):
            finish_chunk(e, c, slot)

            @pl.when(c + N_STAGE < N_CHUNKS)
            def _():
                start_chunk(e, c + N_STAGE)
            return carry
        lax.fori_loop(lo, hi, body, 0)

    def prime(e):
        for c in range(N_STAGE):
            start_chunk(e, jnp.int32(c))

    @pl.when(i == 0)
    def _():
        prime(be_ref[0])
        stream(be_ref[0], 0, 0, N_CHUNKS)

    @pl.when(i < nb_ref[0])
    def _():
        s = slot_ref[i]
        ne = nxt_ref[i]

        @pl.when((first_ref[i] == 1) & (ne >= 0))
        def _():
            prime(ne)

        stream(ne, 1 - s, c0_ref[i], c1_ref[i])
        lo, hi = _unpack_halves(xs_ref[...])
        lo = lo.astype(BF16)
        hi = hi.astype(BF16)
        g = _dot(lo, wgu_ref[s, 0, 0:PACK_COLS, :]) + _dot(hi, wgu_ref[s, 0, PACK_COLS:, :])
        u = _dot(lo, wgu_ref[s, 1, 0:PACK_COLS, :]) + _dot(hi, wgu_ref[s, 1, PACK_COLS:, :])
        hid = (g * jax.nn.sigmoid(g)) * u
        y_ref[...] = _pack_halves(_dot(hid.astype(BF16), wdn_ref[s]))

    @pl.when(i >= nb_ref[0])
    def _():
        y_ref[...] = jnp.zeros(y_ref.shape, y_ref.dtype)


def _expert_schedule(block_e, nb, n_blocks):
    idx = jnp.arange(n_blocks, dtype=I32)
    valid = idx < nb[0]
    prev = jnp.concatenate([block_e[:1] - 1, block_e[:-1]])
    first = valid & ((idx == 0) | (block_e != prev))
    run_id = jnp.cumsum(first.astype(I32)) - 1
    run_start = lax.cummax(jnp.where(first, idx, 0))
    run_len = jnp.sum((run_id[:, None] == run_id[None, :]) & valid[None, :], -1).astype(I32)
    j = idx - run_start
    nxt_idx = run_start + run_len
    nxt = jnp.where(valid & (nxt_idx < nb[0]), block_e[jnp.minimum(nxt_idx, n_blocks - 1)], -1).astype(I32)
    n = jnp.maximum(run_len, 1)
    has = nxt >= 0
    c0 = jnp.where(has, j * N_CHUNKS // n, 0).astype(I32)
    c1 = jnp.where(has, (j + 1) * N_CHUNKS // n, 0).astype(I32)
    return (run_id % 2).astype(I32), c0, c1, nxt, first.astype(I32)


def _experts(layer, block_e, nb, xs, wg, wu, wd):
    n_rows = xs.shape[0]
    bm = EXPERT_BLOCK
    n_blocks = n_rows // bm
    slot, c0, c1, nxt, first = _expert_schedule(block_e, nb, n_blocks)
    blk = lambda i, be, nb, *_: (jnp.maximum(jnp.minimum(i, nb[0] - 1), 0), 0)
    hbm = pl.BlockSpec(memory_space=pl.ANY)
    return pl.pallas_call(
        functools.partial(_experts_body, layer),
        grid_spec=pltpu.PrefetchScalarGridSpec(
            num_scalar_prefetch=7,
            grid=(n_blocks,),
            in_specs=[pl.BlockSpec((bm, PACK_COLS), blk), hbm, hbm, hbm],
            out_specs=pl.BlockSpec((bm, PACK_COLS), lambda i, *_: (i, 0)),
            scratch_shapes=[
                pltpu.VMEM((2, 2, D_MODEL, D_EXPERT), BF16),
                pltpu.VMEM((2, D_EXPERT, D_MODEL), BF16),
                pltpu.VMEM((N_STAGE, GU_ROWS, D_EXPERT), F32),
                pltpu.VMEM((N_STAGE, DN_ROWS, D_MODEL), F32),
                pltpu.SemaphoreType.DMA((N_STAGE,)),
                pltpu.SemaphoreType.DMA((N_STAGE,)),
            ],
        ),
        out_shape=jax.ShapeDtypeStruct((n_rows, PACK_COLS), U32),
        compiler_params=_cparams(("arbitrary",)),
        name="experts",
    )(block_e, nb, slot, c0, c1, nxt, first, xs, wg, wu, wd)


def _combine_body(tm, T, n_steps, pos_ref, y_ref, gate_ref, x1_ref, p_ref, wpg_ref, wpp_ref, g_ref, b_ref,
                  o_ref, ob_ref, ybuf_ref, ffn_ref, sem):
    step = pl.program_id(0)
    slot = step % 2
    rows_per_step = 2 * tm

    def row_copy(base, r, k, s):
        src = pos_ref[k * T + base + r]
        return pltpu.make_async_copy(y_ref.at[pl.ds(src, 1)], ybuf_ref.at[s, k, pl.ds(r, 1)], sem.at[s])

    def wait_slot(s):
        for k in range(2):
            pltpu.make_async_copy(y_ref.at[pl.ds(0, rows_per_step)], ybuf_ref.at[s, k], sem.at[s]).wait()

    @pl.when(step == 0)
    def _():
        def issue(r, carry):
            for k in range(2):
                row_copy(0, r, k, 0).start()
            return carry
        lax.fori_loop(0, rows_per_step, issue, 0)

    wait_slot(slot)
    halves = [slice(h * tm, (h + 1) * tm) for h in range(2)]
    for rows in halves:
        gate = gate_ref[rows, :]
        lo0, hi0 = _unpack_halves(ybuf_ref[slot, 0, rows, :])
        lo1, hi1 = _unpack_halves(ybuf_ref[slot, 1, rows, :])
        g0 = gate[:, 0:1]
        g1 = gate[:, 1:2]
        ffn_ref[rows, 0:PACK_COLS] = lo0 * g0 + lo1 * g1
        ffn_ref[rows, PACK_COLS:] = hi0 * g0 + hi1 * g1
    nxt_base = jnp.minimum(step + 1, n_steps - 1) * rows_per_step
    for r in range(rows_per_step):
        for k in range(2):
            row_copy(nxt_base, r, k, 1 - slot).start(priority=k)

    x2, gl = [], []
    for rows in halves:
        x2.append(_layer_norm(ALPHA * x1_ref[rows, :] + ffn_ref[rows, :], g_ref[...], b_ref[...]))
        gl.append(_dot(x2[-1].astype(BF16), wpg_ref[...]))
    pp = [_dot(p_ref[rows, :].astype(BF16), wpp_ref[...]) for rows in halves]
    for rows, x, a, b in zip(halves, x2, gl, pp):
        x3 = x + jax.nn.sigmoid(a) * b
        o_ref[rows, :] = x3
        ob_ref[rows, :] = x3.astype(BF16)

    @pl.when(step == n_steps - 1)
    def _():
        wait_slot(1 - slot)


def _combine(pos_flat, y, gate_t, x1, p, w_ple_gate, w_ple_proj, g, b):
    T = x1.shape[0]
    tm = min(256, T // 2)
    n_steps = T // (2 * tm)
    row = lambda i, pos: (i, 0)
    const = lambda i, pos: (0, 0)
    return pl.pallas_call(
        functools.partial(_combine_body, tm, T, n_steps),
        grid_spec=pltpu.PrefetchScalarGridSpec(
            num_scalar_prefetch=1,
            grid=(n_steps,),
            in_specs=[
                pl.BlockSpec(memory_space=pl.ANY),
                pl.BlockSpec((2 * tm, 2), row),
                pl.BlockSpec((2 * tm, D_MODEL), row),
                pl.BlockSpec((2 * tm, PLE_DIM), row),
                pl.BlockSpec((D_MODEL, D_MODEL), const),
                pl.BlockSpec((PLE_DIM, D_MODEL), const),
                pl.BlockSpec((1, D_MODEL), const),
                pl.BlockSpec((1, D_MODEL), const),
            ],
            out_specs=[
                pl.BlockSpec((2 * tm, D_MODEL), row),
                pl.BlockSpec((2 * tm, D_MODEL), row),
            ],
            scratch_shapes=[pltpu.VMEM((2, 2, 2 * tm, PACK_COLS), U32), pltpu.VMEM((2 * tm, D_MODEL), F32),
                            pltpu.SemaphoreType.DMA((2,))],
        ),
        out_shape=[
            jax.ShapeDtypeStruct((T, D_MODEL), F32),
            jax.ShapeDtypeStruct((T, D_MODEL), BF16),
        ],
        compiler_params=_cparams(("arbitrary",)),
        name="combine",
    )(pos_flat, y, gate_t, x1, p, w_ple_gate.astype(BF16), w_ple_proj.astype(BF16),
      g.astype(F32).reshape(1, D_MODEL), b.astype(F32).reshape(1, D_MODEL))


def _route_plan(eidx, rank, cnt, n_blocks):
    bm = EXPERT_BLOCK
    counts = cnt[:, 0]
    pcounts = (counts + bm - 1) // bm * bm
    pend = jnp.cumsum(pcounts)
    pstart = pend - pcounts
    onehot = eidx[:, :, None] == jnp.arange(N_EXPERTS, dtype=I32)
    pos = jnp.sum(jnp.where(onehot, pstart, 0), -1) + rank
    nb = (pend[-1] // bm).astype(I32).reshape(1)
    block_start = jnp.arange(n_blocks, dtype=I32) * bm
    block_e = jnp.minimum(jnp.sum(block_start[:, None] >= pend[None, :], -1), N_EXPERTS - 1).astype(I32)
    pad_start = (pstart + counts).astype(I32)
    pad_len = (pcounts - counts).astype(I32)
    return pos.reshape(-1).astype(I32), block_e, nb, pad_start, pad_len


def kernel(x, p, w_in, conv_w, a_log, dt_bias, dn_norm_w, pool_w, pool_scale, w_out, ln1_g, ln1_b,
           w_router, b_router, w_e_gate, w_e_up, w_e_down, ln2_g, ln2_b, w_ple_proj, w_ple_gate):
    B, S, D = x.shape
    T = B * S
    n_rows = 2 * T + N_EXPERTS * EXPERT_BLOCK
    xf = x.reshape(T, D).astype(F32)
    xb = xf
    for i in range(DEPTH):
        wi = w_in[i]
        w_qkv = wi[:, :QKV_COLS].astype(BF16)
        w_zu = jnp.concatenate([wi[:, QKV_COLS:4 * DN_WIDTH], wi[:, 4 * DN_WIDTH + 2 * DN_HEADS:]], 1).astype(BF16)
        w_ba = jnp.pad(wi[:, 4 * DN_WIDTH:4 * DN_WIDTH + 2 * DN_HEADS],
                       ((0, 0), (0, LANES - 2 * DN_HEADS))).astype(BF16)
        qkv = _proj_qkv(xb, w_qkv, conv_w[i], S)
        zu, bg = _proj_zu(xb, w_zu, w_ba, a_log[i], dt_bias[i])
        gt = bg[:, DN_HEADS:2 * DN_HEADS].reshape(B, S // CHUNK, CHUNK, DN_HEADS).transpose(0, 1, 3, 2)
        y_dn = _delta(qkv, zu, bg, gt, dn_norm_w[i], B, S)
        y_pool = _pool(zu, pool_w[i], pool_scale[i], S)
        x1, x1p, eidx, gate, rank, cnt = _outproj(y_dn, y_pool, w_out[i], xf, ln1_g[i], ln1_b[i],
                                                  w_router, b_router)
        pos_flat, block_e, nb, pad_start, pad_len = _route_plan(eidx, rank, cnt, n_rows // EXPERT_BLOCK)
        xs = _dispatch(pos_flat, pad_start, pad_len, nb, x1p, n_rows)
        y = _experts(i, block_e, nb, xs, w_e_gate, w_e_up, w_e_down)
        xf, xb = _combine(pos_flat, y, gate.T, x1, p[i].reshape(T, PLE_DIM), w_ple_gate[i], w_ple_proj[i],
                          ln2_g[i], ln2_b[i])
    return xf.reshape(B, S, D).astype(x.dtype)
```

```python
import functools

import jax
import jax.numpy as jnp
from jax import lax
from jax.experimental import pallas as pl
from jax.experimental.pallas import tpu as pltpu

F32 = jnp.float32
BF16 = jnp.bfloat16
I32 = jnp.int32
HIGHEST = lax.Precision.HIGHEST

D_MODEL = 2048
DN_HEADS = 8
HEAD_DIM = 128
DN_WIDTH = DN_HEADS * HEAD_DIM
CONV_WIDTH = 4
CHUNK = 64
POOL_WINDOWS = (2, 4, 8, 16)
POOL_GROUP_DIM = 256
POOL_WIDTH = 1024
N_EXPERTS = 16
N_GROUPS = 4
EXPERTS_PER_GROUP = 4
D_EXPERT = 1024
PLE_DIM = 256
DEPTH = 2
ALPHA = (2.0 * DEPTH) ** 0.25
LN_EPS = 1e-5
RMS_EPS = 1e-6

LANES = 128
MAIN_COLS = 4 * DN_WIDTH + POOL_WIDTH
CONV_HALO = 8
POOL_HALO = 16
EXPERT_BLOCK = 256
VMEM_LIMIT = 56 * 1024 * 1024


def _cparams(sem):
    return pltpu.CompilerParams(dimension_semantics=sem, vmem_limit_bytes=VMEM_LIMIT)


def _dot(a, b):
    return jnp.dot(a, b, preferred_element_type=F32)


def _dot_hi(a, b):
    return jnp.dot(a, b, preferred_element_type=F32, precision=HIGHEST)


def _silu(x):
    h = 0.5 * x
    return h + h * jnp.tanh(h)


def _layer_norm(h, g, b):
    mu = jnp.mean(h, -1, keepdims=True)
    d = h - mu
    var = jnp.mean(d * d, -1, keepdims=True)
    return d * lax.rsqrt(var + LN_EPS) * g + b


PROJ_SPLIT = 8
QKV_COLS = 3 * DN_WIDTH
ZU_COLS = DN_WIDTH + POOL_WIDTH


def _proj_qkv_body(tiles_per_seq, tm, tn, x_ref, w_ref, cw_ref, o_ref, halo_ref, *xs_refs):
    i = pl.program_id(0)
    j = pl.program_id(1)
    sub = tm // PROJ_SPLIT
    first = (i % tiles_per_seq) == 0
    xs_refs[0][0:CONV_HALO, :] = jnp.where(first, 0.0, halo_ref[j])
    for k in range(PROJ_SPLIT):
        raw = _dot(x_ref[k * sub:(k + 1) * sub, :].astype(BF16), w_ref[...])
        xs_refs[k][CONV_HALO:, :] = raw
        if k + 1 < PROJ_SPLIT:
            xs_refs[k + 1][0:CONV_HALO, :] = raw[sub - CONV_HALO:, :]
        else:
            halo_ref[j] = raw[sub - CONV_HALO:, :]
    unit = jnp.where(j == 2, 1.0, 0.0)
    qk_scale = jnp.where(j == 0, HEAD_DIM ** -0.5, jnp.where(j == 1, 1.0, 0.0))
    for k in range(PROJ_SPLIT):
        for cb in range(tn // HEAD_DIM):
            cs = slice(cb * HEAD_DIM, (cb + 1) * HEAD_DIM)
            blk = xs_refs[k][:, cs]
            acc = blk * cw_ref[CONV_WIDTH - 1:CONV_WIDTH, cs]
            for s in range(1, CONV_WIDTH):
                acc = acc + pltpu.roll(blk, s, 0) * cw_ref[CONV_WIDTH - 1 - s:CONV_WIDTH - s, cs]
            acc = acc[CONV_HALO:, :]
            y = _silu(acc)
            inv = lax.rsqrt(jnp.sum(y * y, -1, keepdims=True) + RMS_EPS)
            o_ref[k * sub:(k + 1) * sub, cs] = y * (inv * qk_scale + unit)


def _proj_qkv(x, w_qkv, conv_w, S):
    T = x.shape[0]
    tm = min(1024, S)
    tn = DN_WIDTH
    return pl.pallas_call(
        functools.partial(_proj_qkv_body, S // tm, tm, tn),
        grid=(T // tm, QKV_COLS // tn),
        in_specs=[
            pl.BlockSpec((tm, D_MODEL), lambda i, j: (i, 0)),
            pl.BlockSpec((D_MODEL, tn), lambda i, j: (0, j)),
            pl.BlockSpec((CONV_WIDTH, tn), lambda i, j: (0, j)),
        ],
        out_specs=pl.BlockSpec((tm, tn), lambda i, j: (i, j)),
        out_shape=jax.ShapeDtypeStruct((T, QKV_COLS), F32),
        scratch_shapes=[pltpu.VMEM((QKV_COLS // tn, CONV_HALO, tn), F32)]
        + [pltpu.VMEM((tm // PROJ_SPLIT + CONV_HALO, tn), F32)] * PROJ_SPLIT,
        compiler_params=_cparams(("arbitrary", "arbitrary")),
        name="proj_qkv",
    )(x, w_qkv, conv_w.astype(F32))


def _proj_zu_body(tm, x_ref, w_ref, wba_ref, gp_ref, tri_ref, o_ref, bg_ref):
    x = x_ref[...].astype(BF16)
    o_ref[...] = _dot(x, w_ref[...])

    @pl.when(pl.program_id(1) == 0)
    def _():
        ba = _dot(x, wba_ref[...])
        beta = jax.nn.sigmoid(ba)
        xx = ba + gp_ref[1:2, :]
        softplus = jnp.maximum(xx, 0.0) + jnp.log1p(jnp.exp(-jnp.abs(xx)))
        g = -jnp.exp(gp_ref[0:1, :]) * softplus
        tri = tri_ref[...]
        tb = tri.shape[0]
        gam = jnp.concatenate([_dot_hi(tri, g[r:r + tb, :]) for r in range(0, tm, tb)], 0)
        lane = lax.broadcasted_iota(I32, ba.shape, 1)
        bg_ref[...] = jnp.where(lane < DN_HEADS, beta, gam)


def _proj_zu(x, w_zu, w_ba, a_log, dt_bias):
    T = x.shape[0]
    tm = min(1024, T)
    tn = 1024
    tb = min(256, tm)
    pad = LANES - 2 * DN_HEADS
    gp = jnp.stack([
        jnp.pad(a_log.astype(F32), (DN_HEADS, pad)),
        jnp.pad(dt_bias.astype(F32), (DN_HEADS, pad)),
    ])
    r = jnp.arange(tb)
    tri = ((r[:, None] >= r[None, :]) & (r[:, None] // CHUNK == r[None, :] // CHUNK)).astype(F32)
    return pl.pallas_call(
        functools.partial(_proj_zu_body, tm),
        grid=(T // tm, ZU_COLS // tn),
        in_specs=[
            pl.BlockSpec((tm, D_MODEL), lambda i, j: (i, 0)),
            pl.BlockSpec((D_MODEL, tn), lambda i, j: (0, j)),
            pl.BlockSpec((D_MODEL, LANES), lambda i, j: (0, 0)),
            pl.BlockSpec((2, LANES), lambda i, j: (0, 0)),
            pl.BlockSpec((tb, tb), lambda i, j: (0, 0)),
        ],
        out_specs=[
            pl.BlockSpec((tm, tn), lambda i, j: (i, j)),
            pl.BlockSpec((tm, LANES), lambda i, j: (i, 0)),
        ],
        out_shape=[
            jax.ShapeDtypeStruct((T, ZU_COLS), F32),
            jax.ShapeDtypeStruct((T, LANES), F32),
        ],
        compiler_params=_cparams(("parallel", "arbitrary")),
        name="proj_zu",
    )(x, w_zu, w_ba, gp, tri)


DELTA_BATCH = 2


def _delta_body(nb, nc, q_ref, k_ref, v_ref, z_ref, bg_ref, gt_ref, nw_ref, o_ref, state_ref):
    @pl.when(pl.program_id(1) == 0)
    def _():
        state_ref[...] = jnp.zeros(state_ref.shape, F32)

    ii = lax.broadcasted_iota(I32, (CHUNK, CHUNK), 0)
    jj = lax.broadcasted_iota(I32, (CHUNK, CHUNK), 1)
    incl = ii >= jj
    strict = ii > jj
    nt = (((1,), (1,)), ((), ()))
    chains = [(b, h) for b in range(nb) for h in range(DN_HEADS)]
    cs = [slice(h * HEAD_DIM, (h + 1) * HEAD_DIM) for _, h in chains]
    n = range(len(chains))

    def chunk(c, carry):
        r0 = pl.multiple_of(c * CHUNK, CHUNK)
        rows = pl.ds(r0, CHUNK)
        bg = [bg_ref[b, rows, :] for b in range(nb)]
        gt = [gt_ref[b, c] for b in range(nb)]
        kh = [k_ref[b, rows, cs[i]] for i, (b, _) in enumerate(chains)]
        qh = [q_ref[b, rows, cs[i]] for i, (b, _) in enumerate(chains)]
        bcol = [bg[b][:, h:h + 1] for b, h in chains]
        gcol = [bg[b][:, DN_HEADS + h:DN_HEADS + h + 1] for b, h in chains]
        grow = [gt[b][h:h + 1, :] for b, h in chains]
        glast = [grow[i][:, CHUNK - 1:CHUNK] for i in n]
        kb = [kh[i] * bcol[i] for i in n]
        s = [lax.dot_general(jnp.concatenate([kb[i], qh[i]], 0).astype(BF16), kh[i].astype(BF16), nt,
                             preferred_element_type=F32) for i in n]
        decay = [jnp.where(incl, jnp.exp(jnp.where(incl, gcol[i] - grow[i], 0.0)), 0.0) for i in n]
        aqk = [(s[i][CHUNK:] * decay[i]).astype(BF16) for i in n]
        pw = [jnp.where(strict, -s[i][:CHUNK] * decay[i], 0.0) for i in n]
        qs = pw
        pwb = [pw[i].astype(BF16) for i in n]
        pw = [_dot(pwb[i], pwb[i]) for i in n]
        for _ in range(4):
            pwb = [pw[i].astype(BF16) for i in n]
            both = [_dot(jnp.concatenate([pwb[i], qs[i].astype(BF16)], 0), pwb[i]) for i in n]
            qs = [qs[i] + pw[i] + both[i][CHUNK:] for i in n]
            pw = [both[i][:CHUNK] for i in n]
        qp = [_dot(qs[i].astype(BF16), pw[i].astype(BF16)) for i in n]
        qs = [qs[i] + pw[i] + qp[i] for i in n]
        eg = [jnp.exp(gcol[i]) for i in n]
        rhs = [jnp.concatenate([v_ref[b, rows, cs[i]] * bcol[i], kb[i] * eg[i]], 1)
               for i, (b, _) in enumerate(chains)]
        sol = [rhs[i] + _dot(qs[i].astype(BF16), rhs[i].astype(BF16)) for i in n]
        st = [state_ref[i] for i in n]
        r = [_dot(jnp.concatenate([sol[i][:, HEAD_DIM:], qh[i] * eg[i]], 0).astype(BF16), st[i].astype(BF16))
             for i in n]
        v_new = [(sol[i][:, :HEAD_DIM] - r[i][:CHUNK]).astype(BF16) for i in n]
        kdt = [(kh[i] * jnp.exp(glast[i] - gcol[i])).T.astype(BF16) for i in n]
        ou = [_dot(jnp.concatenate([aqk[i], kdt[i]], 0), v_new[i]) for i in n]
        for i, (b, _) in enumerate(chains):
            state_ref[i] = st[i] * jnp.exp(glast[i]) + ou[i][CHUNK:]
            zz = z_ref[b, rows, cs[i]]
            oi = r[i][CHUNK:] + ou[i][:CHUNK]
            y = oi * lax.rsqrt(jnp.mean(oi * oi, -1, keepdims=True) + RMS_EPS) * nw_ref[...]
            o_ref[b, rows, cs[i]] = (y * _silu(zz)).astype(o_ref.dtype)
        return carry

    lax.fori_loop(0, nc, chunk, 0)


def _delta(qkv, zu, bg, gt, norm_w, B, S):
    T = qkv.shape[0]
    nb = DELTA_BATCH if B % DELTA_BATCH == 0 else 1
    sblk = min(512, S)
    nc = sblk // CHUNK
    seq = lambda a: a.reshape(B, S, a.shape[-1])
    blk = lambda b, s: (b, s, 0)
    y = pl.pallas_call(
        functools.partial(_delta_body, nb, nc),
        grid=(B // nb, S // sblk),
        in_specs=[
            pl.BlockSpec((nb, sblk, DN_WIDTH), blk),
            pl.BlockSpec((nb, sblk, DN_WIDTH), lambda b, s: (b, s, 1)),
            pl.BlockSpec((nb, sblk, DN_WIDTH), lambda b, s: (b, s, 2)),
            pl.BlockSpec((nb, sblk, DN_WIDTH), blk),
            pl.BlockSpec((nb, sblk, LANES), blk),
            pl.BlockSpec((nb, nc, DN_HEADS, CHUNK), lambda b, s: (b, s, 0, 0)),
            pl.BlockSpec((1, HEAD_DIM), lambda b, s: (0, 0)),
        ],
        out_specs=pl.BlockSpec((nb, sblk, DN_WIDTH), blk),
        out_shape=jax.ShapeDtypeStruct((B, S, DN_WIDTH), BF16),
        scratch_shapes=[pltpu.VMEM((nb * DN_HEADS, HEAD_DIM, HEAD_DIM), F32)],
        compiler_params=_cparams(("parallel", "arbitrary")),
        name="delta",
    )(seq(qkv), seq(qkv), seq(qkv), seq(zu), seq(bg), gt, norm_w.astype(F32).reshape(1, HEAD_DIM))
    return y.reshape(T, DN_WIDTH)


def _pool_body(tiles_per_seq, tm, u_ref, halo_ref, w_ref, sc_ref, o_ref, us_ref):
    t_in_seq = (pl.program_id(0) % tiles_per_seq) * tm
    first = t_in_seq == 0
    us_ref[0:POOL_HALO, :] = jnp.where(first, 0.0, halo_ref[...])
    us_ref[POOL_HALO:POOL_HALO + tm, :] = u_ref[...]
    tpos = (t_in_seq + lax.broadcasted_iota(I32, (tm, 1), 0) + 1).astype(F32)
    for gi, win in enumerate(POOL_WINDOWS):
        cs = slice(gi * POOL_GROUP_DIM, (gi + 1) * POOL_GROUP_DIM)
        cur = us_ref[POOL_HALO:POOL_HALO + tm, cs]
        wsum = cur
        for j in range(1, win):
            wsum = wsum + us_ref[POOL_HALO - j:POOL_HALO - j + tm, cs]
        d = wsum / jnp.minimum(tpos, float(win)) - cur
        y = _dot(d.astype(BF16), w_ref[gi])
        o_ref[:, cs] = (y * sc_ref[:, cs]).astype(o_ref.dtype)


def _pool(proj, pool_w, pool_scale, S):
    T = proj.shape[0]
    tm = min(256, S)
    ucol = DN_WIDTH // POOL_WIDTH
    halo_blocks = tm // POOL_HALO
    return pl.pallas_call(
        functools.partial(_pool_body, S // tm, tm),
        grid=(T // tm,),
        in_specs=[
            pl.BlockSpec((tm, POOL_WIDTH), lambda i: (i, ucol)),
            pl.BlockSpec((POOL_HALO, POOL_WIDTH), lambda i: (jnp.maximum(i * halo_blocks - 1, 0), ucol)),
            pl.BlockSpec((len(POOL_WINDOWS), POOL_GROUP_DIM, POOL_GROUP_DIM), lambda i: (0, 0, 0)),
            pl.BlockSpec((1, POOL_WIDTH), lambda i: (0, 0)),
        ],
        out_specs=pl.BlockSpec((tm, POOL_WIDTH), lambda i: (i, 0)),
        out_shape=jax.ShapeDtypeStruct((T, POOL_WIDTH), BF16),
        scratch_shapes=[pltpu.VMEM((tm + POOL_HALO, POOL_WIDTH), F32)],
        compiler_params=_cparams(("parallel",)),
        name="pool",
    )(proj, proj, pool_w.astype(BF16), pool_scale.astype(F32).reshape(1, POOL_WIDTH))


OUTPROJ_SPLIT = 2
PACK_COLS = D_MODEL // 2
U32 = jnp.uint32


def _pack_halves(x):
    lo = lax.bitcast_convert_type(x[:, :PACK_COLS].astype(BF16).astype(F32), U32)
    hi = lax.bitcast_convert_type(x[:, PACK_COLS:].astype(BF16).astype(F32), U32)
    return (lo >> 16) | hi


def _unpack_halves(p):
    lo = lax.bitcast_convert_type(p << 16, F32)
    hi = lax.bitcast_convert_type(p & jnp.uint32(0xFFFF0000), F32)
    return lo, hi


def _route_tile(tm, logits, upper_ref, eidx_ref, gate_ref, rank_ref, cnt_ref, carry_ref):
    m = jnp.max(logits, axis=0, keepdims=True)
    e = jnp.exp(logits - m)
    p = e / jnp.sum(e, axis=0, keepdims=True)
    rows = [p[i:i + 1, :] for i in range(N_EXPERTS)]

    scores = []
    for g in range(N_GROUPS):
        a, b, c, d = rows[EXPERTS_PER_GROUP * g:EXPERTS_PER_GROUP * (g + 1)]
        hi1, lo1 = jnp.maximum(a, b), jnp.minimum(a, b)
        hi2, lo2 = jnp.maximum(c, d), jnp.minimum(c, d)
        top1 = jnp.maximum(hi1, hi2)
        top2 = jnp.maximum(jnp.minimum(hi1, hi2), jnp.where(hi1 >= hi2, lo1, lo2))
        scores.append(top1 + top2)
    gsel = jnp.zeros((1, tm), I32)
    best = scores[0]
    for g in range(1, N_GROUPS):
        better = scores[g] > best
        gsel = jnp.where(better, g, gsel)
        best = jnp.where(better, scores[g], best)
    ing = []
    for j in range(EXPERTS_PER_GROUP):
        sel = rows[(N_GROUPS - 1) * EXPERTS_PER_GROUP + j]
        for g in range(N_GROUPS - 2, -1, -1):
            sel = jnp.where(gsel == g, rows[g * EXPERTS_PER_GROUP + j], sel)
        ing.append(sel)
    i1 = jnp.zeros((1, tm), I32)
    p1 = ing[0]
    for j in range(1, EXPERTS_PER_GROUP):
        better = ing[j] > p1
        i1 = jnp.where(better, j, i1)
        p1 = jnp.where(better, ing[j], p1)
    i2 = jnp.zeros((1, tm), I32)
    p2 = jnp.full((1, tm), -1.0, F32)
    for j in range(EXPERTS_PER_GROUP):
        cand = jnp.where(i1 == j, -1.0, ing[j])
        better = cand > p2
        i2 = jnp.where(better, j, i2)
        p2 = jnp.where(better, cand, p2)
    den = p1 + p2
    e0 = gsel * EXPERTS_PER_GROUP + i1
    e1 = gsel * EXPERTS_PER_GROUP + i2
    eidx_ref[0:1, :] = e0
    eidx_ref[1:2, :] = e1
    gate_ref[0:1, :] = p1 / den
    gate_ref[1:2, :] = p2 / den

    er = lax.broadcasted_iota(I32, (N_EXPERTS, tm), 0)
    oh0 = er == e0
    oh1 = er == e1
    oh = jnp.where(oh0 | oh1, 1.0, 0.0)
    before = carry_ref[:, 0:1] + _dot(oh.astype(BF16), upper_ref[...])
    rank_ref[0:1, :] = jnp.sum(jnp.where(oh0, before, 0.0), axis=0, keepdims=True).astype(I32)
    rank_ref[1:2, :] = jnp.sum(jnp.where(oh1, before, 0.0), axis=0, keepdims=True).astype(I32)
    total = carry_ref[...] + jnp.sum(oh, axis=1, keepdims=True)
    carry_ref[...] = total
    cnt_ref[...] = total.astype(I32)


def _outproj_body(tm, ydn_ref, ypool_ref, w_ref, x_ref, g_ref, b_ref, wrh_ref, wrl_ref, br_ref, upper_ref,
                  o_ref, op_ref, eidx_ref, gate_ref, rank_ref, cnt_ref, carry_ref):
    @pl.when(pl.program_id(0) == 0)
    def _():
        carry_ref[...] = jnp.zeros(carry_ref.shape, F32)

    subs = [slice(k * (tm // OUTPROJ_SPLIT), (k + 1) * (tm // OUTPROJ_SPLIT)) for k in range(OUTPROJ_SPLIT)]
    mix = [_dot(ydn_ref[r, :], w_ref[0:DN_WIDTH, :]) + _dot(ypool_ref[r, :], w_ref[DN_WIDTH:, :]) for r in subs]
    x1 = [_layer_norm(ALPHA * x_ref[r, :] + m, g_ref[...], b_ref[...]) for r, m in zip(subs, mix)]
    xh = [x.astype(BF16) for x in x1]
    xl = [(x - h.astype(F32)).astype(BF16) for x, h in zip(x1, xh)]
    lg = [_dot(h, wrh_ref[...]) + (_dot(l, wrh_ref[...]) + _dot(h, wrl_ref[...])) for h, l in zip(xh, xl)]
    for r, x in zip(subs, x1):
        o_ref[r, :] = x
        op_ref[r, :] = _pack_halves(x)
    logits = jnp.concatenate(lg, 0).T[0:N_EXPERTS, :] + br_ref[:, 0:1]
    _route_tile(tm, logits, upper_ref, eidx_ref, gate_ref, rank_ref, cnt_ref, carry_ref)


def _outproj(y_dn, y_pool, w_out, x, g, b, w_router, b_router):
    T = x.shape[0]
    tm = min(512, T)
    r = jnp.arange(tm)
    upper = (r[:, None] < r[None, :]).astype(BF16)
    wr = jnp.pad(w_router.astype(F32), ((0, 0), (0, LANES - N_EXPERTS)))
    wr_hi = wr.astype(BF16)
    wr_lo = (wr - wr_hi.astype(F32)).astype(BF16)
    row = lambda i: (i, 0)
    tok = lambda i: (0, i)
    const = lambda i: (0, 0)
    return pl.pallas_call(
        functools.partial(_outproj_body, tm),
        grid=(T // tm,),
        in_specs=[
            pl.BlockSpec((tm, DN_WIDTH), row),
            pl.BlockSpec((tm, POOL_WIDTH), row),
            pl.BlockSpec((D_MODEL, D_MODEL), const),
            pl.BlockSpec((tm, D_MODEL), row),
            pl.BlockSpec((1, D_MODEL), const),
            pl.BlockSpec((1, D_MODEL), const),
            pl.BlockSpec((D_MODEL, LANES), const),
            pl.BlockSpec((D_MODEL, LANES), const),
            pl.BlockSpec((N_EXPERTS, LANES), const),
            pl.BlockSpec((tm, tm), const),
        ],
        out_specs=[
            pl.BlockSpec((tm, D_MODEL), row),
            pl.BlockSpec((tm, PACK_COLS), row),
            pl.BlockSpec((2, tm), tok),
            pl.BlockSpec((2, tm), tok),
            pl.BlockSpec((2, tm), tok),
            pl.BlockSpec((N_EXPERTS, LANES), const),
        ],
        out_shape=[
            jax.ShapeDtypeStruct((T, D_MODEL), F32),
            jax.ShapeDtypeStruct((T, PACK_COLS), U32),
            jax.ShapeDtypeStruct((2, T), I32),
            jax.ShapeDtypeStruct((2, T), F32),
            jax.ShapeDtypeStruct((2, T), I32),
            jax.ShapeDtypeStruct((N_EXPERTS, LANES), I32),
        ],
        scratch_shapes=[pltpu.VMEM((N_EXPERTS, LANES), F32)],
        compiler_params=_cparams(("arbitrary",)),
        name="outproj",
    )(y_dn, y_pool, w_out.astype(BF16), x, g.astype(F32).reshape(1, D_MODEL), b.astype(F32).reshape(1, D_MODEL),
      wr_hi, wr_lo, jnp.broadcast_to(b_router.astype(F32)[:, None], (N_EXPERTS, LANES)), upper)


DUMP_ROWS = 2 * EXPERT_BLOCK


def _dispatch_body(tm, T, n_blocks, pos_ref, pad_start_ref, pad_len_ref, nb_ref, x_ref, xs_ref, inv_ref,
                   zrow_ref, sem, zsem):
    step = pl.program_id(0)

    def zero_copy(dst):
        return pltpu.make_async_copy(zrow_ref.at[pl.ds(0, 1)], xs_ref.at[pl.ds(dst, 1)], zsem)

    def dump_row(r):
        return 2 * T + lax.rem(r, DUMP_ROWS)

    @pl.when(step == 0)
    def _():
        zrow_ref[...] = jnp.zeros(zrow_ref.shape, zrow_ref.dtype)
        for e in range(N_EXPERTS):
            def fill(r, carry, e=e):
                zero_copy(pad_start_ref[e] + r).start()
                inv_ref[pad_start_ref[e] + r] = dump_row(pad_start_ref[e] + r)
                return carry
            lax.fori_loop(0, pad_len_ref[e], fill, 0)

        def tail_inv(r, carry):
            inv_ref[r] = dump_row(r)
            return carry
        lax.fori_loop(nb_ref[0] * EXPERT_BLOCK, n_blocks * EXPERT_BLOCK, tail_inv, 0)
        for e in range(N_EXPERTS):
            def drain(r, carry):
                zero_copy(0).wait()
                return carry
            lax.fori_loop(0, pad_len_ref[e], drain, 0)

        def tail_copy(blk):
            return pltpu.make_async_copy(zrow_ref, xs_ref.at[pl.ds(blk * EXPERT_BLOCK, EXPERT_BLOCK)], zsem)

        def tail_fill(blk, carry):
            tail_copy(blk).start()
            return carry

        def tail_drain(blk, carry):
            tail_copy(blk).wait()
            return carry
        lax.fori_loop(nb_ref[0], n_blocks, tail_fill, 0)
        lax.fori_loop(nb_ref[0], n_blocks, tail_drain, 0)

    base = step * tm
    for r in range(tm):
        for k in range(2):
            dst = pos_ref[k * T + base + r]
            inv_ref[dst] = k * T + base + r
            pltpu.make_async_copy(x_ref.at[pl.ds(r, 1)], xs_ref.at[pl.ds(dst, 1)], sem).start(priority=k)
    for k in range(2):
        pltpu.make_async_copy(x_ref, xs_ref.at[pl.ds(0, tm)], sem).wait()


def _dispatch(pos_flat, pad_start, pad_len, nb, x1, n_rows):
    T = x1.shape[0]
    tm = min(256, T)
    return pl.pallas_call(
        functools.partial(_dispatch_body, tm, T, n_rows // EXPERT_BLOCK),
        grid_spec=pltpu.PrefetchScalarGridSpec(
            num_scalar_prefetch=4,
            grid=(T // tm,),
            in_specs=[pl.BlockSpec((tm, PACK_COLS), lambda i, *_: (i, 0))],
            out_specs=[pl.BlockSpec(memory_space=pl.ANY), pl.BlockSpec(memory_space=pltpu.SMEM)],
            scratch_shapes=[pltpu.VMEM((EXPERT_BLOCK, PACK_COLS), U32), pltpu.SemaphoreType.DMA,
                            pltpu.SemaphoreType.DMA],
        ),
        out_shape=[jax.ShapeDtypeStruct((n_rows, PACK_COLS), U32), jax.ShapeDtypeStruct((n_rows,), I32)],
        compiler_params=_cparams(("arbitrary",)),
        name="dispatch",
    )(pos_flat, pad_start, pad_len, nb, x1)


EXPERT_SPLIT = 2
W_CHUNKS = 8
N_CHUNKS = 3 * W_CHUNKS
N_STAGE = 4
GU_ROWS = D_MODEL // W_CHUNKS
DN_ROWS = D_EXPERT // W_CHUNKS


def _experts_body(layer, be_ref, nb_ref, slot_ref, c0_ref, c1_ref, nxt_ref, first_ref, inv_ref,
                  xs_ref, wg_hbm, wu_hbm, wd_hbm, yk_ref, wgu_ref, wdn_ref, stg_a, stg_d, ybuf_ref,
                  sem_a, sem_d, osem):
    i = pl.program_id(0)

    def chunk_copy(kind, e, idx, b):
        if kind == 0:
            return pltpu.make_async_copy(wg_hbm.at[layer, e, pl.ds(idx * GU_ROWS, GU_ROWS), :], stg_a.at[b], sem_a.at[b])
        if kind == 1:
            return pltpu.make_async_copy(wu_hbm.at[layer, e, pl.ds(idx * GU_ROWS, GU_ROWS), :], stg_a.at[b], sem_a.at[b])
        return pltpu.make_async_copy(wd_hbm.at[layer, e, pl.ds(idx * DN_ROWS, DN_ROWS), :], stg_d.at[b], sem_d.at[b])

    def start_chunk(e, c):
        for kind in range(3):
            @pl.when(c // W_CHUNKS == kind)
            def _():
                chunk_copy(kind, e, c % W_CHUNKS, c % N_STAGE).start()

    def finish_chunk(e, c, slot):
        for kind in range(3):
            @pl.when(c // W_CHUNKS == kind)
            def _():
                idx = c % W_CHUNKS
                b = c % N_STAGE
                chunk_copy(kind, e, idx, b).wait()
                if kind < 2:
                    rows = pl.ds(pl.multiple_of(idx * GU_ROWS, GU_ROWS), GU_ROWS)
                    wgu_ref[slot, kind, rows, :] = stg_a[b].astype(BF16)
                else:
                    rows = pl.ds(pl.multiple_of(idx * DN_ROWS, DN_ROWS), DN_ROWS)
                    wdn_ref[slot, rows, :] = stg_d[b].astype(BF16)

    def stream(e, slot, lo, hi):
        def body(c, carry):
            finish_chunk(e, c, slot)

            @pl.when(c + N_STAGE < N_CHUNKS)
            def _():
                start_chunk(e, c + N_STAGE)
            return carry
        lax.fori_loop(lo, hi, body, 0)

    def prime(e):
        for c in range(N_STAGE):
            start_chunk(e, jnp.int32(c))

    dump = yk_ref.shape[0] - DUMP_ROWS

    def out_copy(slot, r, dst):
        return pltpu.make_async_copy(ybuf_ref.at[slot, pl.ds(r, 1)], yk_ref.at[pl.ds(dst, 1)], osem.at[slot])

    def wait_out(slot):
        pltpu.make_async_copy(ybuf_ref.at[slot], yk_ref.at[pl.ds(0, EXPERT_BLOCK)], osem.at[slot]).wait()

    @pl.when(i == 0)
    def _():
        prime(be_ref[0])
        stream(be_ref[0], 0, 0, N_CHUNKS)
        ybuf_ref[...] = jnp.zeros(ybuf_ref.shape, ybuf_ref.dtype)

        def fill(r, carry):
            out_copy(0, r, dump + r).start()
            return carry
        lax.fori_loop(0, EXPERT_BLOCK, fill, 0)

    @pl.when(i < nb_ref[0])
    def _():
        s = slot_ref[i]
        ne = nxt_ref[i]
        par = i % 2

        @pl.when((first_ref[i] == 1) & (ne >= 0))
        def _():
            prime(ne)

        stream(ne, 1 - s, c0_ref[i], c1_ref[i])
        wait_out(par)
        prev = jnp.maximum(i - 1, 0) * EXPERT_BLOCK
        for r in range(EXPERT_BLOCK):
            dst = jnp.where(i >= 1, inv_ref[prev + r], dump + EXPERT_BLOCK + r)
            out_copy(1 - par, r, dst).start(priority=r % 2)
        sub = EXPERT_BLOCK // EXPERT_SPLIT
        hid = []
        for k in range(EXPERT_SPLIT):
            lo, hi = _unpack_halves(xs_ref[k * sub:(k + 1) * sub, :])
            lo = lo.astype(BF16)
            hi = hi.astype(BF16)
            g = _dot(lo, wgu_ref[s, 0, 0:PACK_COLS, :]) + _dot(hi, wgu_ref[s, 0, PACK_COLS:, :])
            u = _dot(lo, wgu_ref[s, 1, 0:PACK_COLS, :]) + _dot(hi, wgu_ref[s, 1, PACK_COLS:, :])
            hid.append((_silu(g) * u).astype(BF16))
        y = [_dot(h, wdn_ref[s]) for h in hid]
        for k in range(EXPERT_SPLIT):
            ybuf_ref[par, k * sub:(k + 1) * sub, :] = _pack_halves(y[k])

    @pl.when(i == nb_ref[0])
    def _():
        par = i % 2
        wait_out(par)
        last = (i - 1) * EXPERT_BLOCK

        def send(r, carry):
            out_copy(1 - par, r, inv_ref[last + r]).start()
            return carry
        lax.fori_loop(0, EXPERT_BLOCK, send, 0)
        wait_out(1 - par)


def _expert_schedule(block_e, nb, n_blocks):
    idx = jnp.arange(n_blocks, dtype=I32)
    valid = idx < nb[0]
    prev = jnp.concatenate([block_e[:1] - 1, block_e[:-1]])
    first = valid & ((idx == 0) | (block_e != prev))
    run_id = jnp.cumsum(first.astype(I32)) - 1
    run_start = lax.cummax(jnp.where(first, idx, 0))
    run_len = jnp.sum((run_id[:, None] == run_id[None, :]) & valid[None, :], -1).astype(I32)
    j = idx - run_start
    nxt_idx = run_start + run_len
    nxt = jnp.where(valid & (nxt_idx < nb[0]), block_e[jnp.minimum(nxt_idx, n_blocks - 1)], -1).astype(I32)
    n = jnp.maximum(run_len, 1)
    has = nxt >= 0
    c0 = jnp.where(has, j * N_CHUNKS // n, 0).astype(I32)
    c1 = jnp.where(has, (j + 1) * N_CHUNKS // n, 0).astype(I32)
    return (run_id % 2).astype(I32), c0, c1, nxt, first.astype(I32)


def _experts(layer, block_e, nb, inv, xs, wg, wu, wd, T):
    n_rows = xs.shape[0]
    bm = EXPERT_BLOCK
    n_blocks = n_rows // bm
    slot, c0, c1, nxt, first = _expert_schedule(block_e, nb, n_blocks)
    blk = lambda i, be, nb, *_: (jnp.maximum(jnp.minimum(i, nb[0] - 1), 0), 0)
    hbm = pl.BlockSpec(memory_space=pl.ANY)
    return pl.pallas_call(
        functools.partial(_experts_body, layer),
        grid_spec=pltpu.PrefetchScalarGridSpec(
            num_scalar_prefetch=8,
            grid=(n_blocks,),
            in_specs=[pl.BlockSpec((bm, PACK_COLS), blk), hbm, hbm, hbm],
            out_specs=hbm,
            scratch_shapes=[
                pltpu.VMEM((2, 2, D_MODEL, D_EXPERT), BF16),
                pltpu.VMEM((2, D_EXPERT, D_MODEL), BF16),
                pltpu.VMEM((N_STAGE, GU_ROWS, D_EXPERT), F32),
                pltpu.VMEM((N_STAGE, DN_ROWS, D_MODEL), F32),
                pltpu.VMEM((2, bm, PACK_COLS), U32),
                pltpu.SemaphoreType.DMA((N_STAGE,)),
                pltpu.SemaphoreType.DMA((N_STAGE,)),
                pltpu.SemaphoreType.DMA((2,)),
            ],
        ),
        out_shape=jax.ShapeDtypeStruct((2 * T + DUMP_ROWS, PACK_COLS), U32),
        compiler_params=_cparams(("arbitrary",)),
        name="experts",
    )(block_e, nb, slot, c0, c1, nxt, first, inv, xs, wg, wu, wd)


COMBINE_SPLIT = 2


def _combine_body(tm, y0_ref, y1_ref, gate_ref, x1_ref, p_ref, wpg_ref, wpp_ref, g_ref, b_ref, o_ref, ob_ref):
    sub = tm // COMBINE_SPLIT
    subs = [slice(h * sub, (h + 1) * sub) for h in range(COMBINE_SPLIT)]
    x2, gl = [], []
    for rows in subs:
        gate = gate_ref[rows, :]
        lo0, hi0 = _unpack_halves(y0_ref[rows, :])
        lo1, hi1 = _unpack_halves(y1_ref[rows, :])
        g0 = gate[:, 0:1]
        g1 = gate[:, 1:2]
        ffn = jnp.concatenate([lo0 * g0 + lo1 * g1, hi0 * g0 + hi1 * g1], 1)
        x2.append(_layer_norm(ALPHA * x1_ref[rows, :] + ffn, g_ref[...], b_ref[...]))
        gl.append(_dot(x2[-1].astype(BF16), wpg_ref[...]))
    pp = [_dot(p_ref[rows, :].astype(BF16), wpp_ref[...]) for rows in subs]
    for rows, x, a, b in zip(subs, x2, gl, pp):
        x3 = x + jax.nn.sigmoid(a) * b
        o_ref[rows, :] = x3
        ob_ref[rows, :] = x3.astype(BF16)


def _combine(yk, gate_t, x1, p, w_ple_gate, w_ple_proj, g, b):
    T = x1.shape[0]
    tm = min(512, T)
    row = lambda i: (i, 0)
    const = lambda i: (0, 0)
    second = T // tm
    return pl.pallas_call(
        functools.partial(_combine_body, tm),
        grid=(T // tm,),
        in_specs=[
            pl.BlockSpec((tm, PACK_COLS), row),
            pl.BlockSpec((tm, PACK_COLS), lambda i: (second + i, 0)),
            pl.BlockSpec((tm, 2), row),
            pl.BlockSpec((tm, D_MODEL), row),
            pl.BlockSpec((tm, PLE_DIM), row),
            pl.BlockSpec((D_MODEL, D_MODEL), const),
            pl.BlockSpec((PLE_DIM, D_MODEL), const),
            pl.BlockSpec((1, D_MODEL), const),
            pl.BlockSpec((1, D_MODEL), const),
        ],
        out_specs=[
            pl.BlockSpec((tm, D_MODEL), row),
            pl.BlockSpec((tm, D_MODEL), row),
        ],
        out_shape=[
            jax.ShapeDtypeStruct((T, D_MODEL), F32),
            jax.ShapeDtypeStruct((T, D_MODEL), BF16),
        ],
        compiler_params=_cparams(("parallel",)),
        name="combine",
    )(yk, yk, gate_t, x1, p, w_ple_gate.astype(BF16), w_ple_proj.astype(BF16),
      g.astype(F32).reshape(1, D_MODEL), b.astype(F32).reshape(1, D_MODEL))


def _route_plan(eidx, rank, cnt, n_blocks):
    bm = EXPERT_BLOCK
    counts = cnt[:, 0]
    pcounts = (counts + bm - 1) // bm * bm
    pend = jnp.cumsum(pcounts)
    pstart = pend - pcounts
    onehot = eidx[:, :, None] == jnp.arange(N_EXPERTS, dtype=I32)
    pos = jnp.sum(jnp.where(onehot, pstart, 0), -1) + rank
    nb = (pend[-1] // bm).astype(I32).reshape(1)
    block_start = jnp.arange(n_blocks, dtype=I32) * bm
    block_e = jnp.minimum(jnp.sum(block_start[:, None] >= pend[None, :], -1), N_EXPERTS - 1).astype(I32)
    pad_start = (pstart + counts).astype(I32)
    pad_len = (pcounts - counts).astype(I32)
    return pos.reshape(-1).astype(I32), block_e, nb, pad_start, pad_len


def kernel(x, p, w_in, conv_w, a_log, dt_bias, dn_norm_w, pool_w, pool_scale, w_out, ln1_g, ln1_b,
           w_router, b_router, w_e_gate, w_e_up, w_e_down, ln2_g, ln2_b, w_ple_proj, w_ple_gate):
    B, S, D = x.shape
    T = B * S
    n_rows = 2 * T + N_EXPERTS * EXPERT_BLOCK
    xf = x.reshape(T, D).astype(F32)
    xb = xf
    for i in range(DEPTH):
        wi = w_in[i]
        w_qkv = wi[:, :QKV_COLS].astype(BF16)
        w_zu = jnp.concatenate([wi[:, QKV_COLS:4 * DN_WIDTH], wi[:, 4 * DN_WIDTH + 2 * DN_HEADS:]], 1).astype(BF16)
        w_ba = jnp.pad(wi[:, 4 * DN_WIDTH:4 * DN_WIDTH + 2 * DN_HEADS],
                       ((0, 0), (0, LANES - 2 * DN_HEADS))).astype(BF16)
        qkv = _proj_qkv(xb, w_qkv, conv_w[i], S)
        zu, bg = _proj_zu(xb, w_zu, w_ba, a_log[i], dt_bias[i])
        gt = bg[:, DN_HEADS:2 * DN_HEADS].reshape(B, S // CHUNK, CHUNK, DN_HEADS).transpose(0, 1, 3, 2)
        y_dn = _delta(qkv, zu, bg, gt, dn_norm_w[i], B, S)
        y_pool = _pool(zu, pool_w[i], pool_scale[i], S)
        x1, x1p, eidx, gate, rank, cnt = _outproj(y_dn, y_pool, w_out[i], xf, ln1_g[i], ln1_b[i],
                                                  w_router, b_router)
        pos_flat, block_e, nb, pad_start, pad_len = _route_plan(eidx, rank, cnt, n_rows // EXPERT_BLOCK)
        xs, inv = _dispatch(pos_flat, pad_start, pad_len, nb, x1p, n_rows)
        yk = _experts(i, block_e, nb, inv, xs, w_e_gate, w_e_up, w_e_down, T)
        xf, xb = _combine(yk, gate.T, x1, p[i].reshape(T, PLE_DIM), w_ple_gate[i], w_ple_proj[i],
                          ln2_g[i], ln2_b[i])
    return xf.reshape(B, S, D).astype(x.dtype)
```

```python
import functools

import jax
import jax.numpy as jnp
from jax import lax
from jax.experimental import pallas as pl
from jax.experimental.pallas import tpu as pltpu

F32 = jnp.float32
BF16 = jnp.bfloat16
I32 = jnp.int32
HIGHEST = lax.Precision.HIGHEST

D_MODEL = 2048
DN_HEADS = 8
HEAD_DIM = 128
DN_WIDTH = DN_HEADS * HEAD_DIM
CONV_WIDTH = 4
CHUNK = 64
POOL_WINDOWS = (2, 4, 8, 16)
POOL_GROUP_DIM = 256
POOL_WIDTH = 1024
N_EXPERTS = 16
N_GROUPS = 4
EXPERTS_PER_GROUP = 4
D_EXPERT = 1024
PLE_DIM = 256
DEPTH = 2
ALPHA = (2.0 * DEPTH) ** 0.25
LN_EPS = 1e-5
RMS_EPS = 1e-6

LANES = 128
MAIN_COLS = 4 * DN_WIDTH + POOL_WIDTH
CONV_HALO = 8
POOL_HALO = 16
EXPERT_BLOCK = 256
VMEM_LIMIT = 56 * 1024 * 1024


def _cparams(sem):
    return pltpu.CompilerParams(dimension_semantics=sem, vmem_limit_bytes=VMEM_LIMIT)


def _dot(a, b):
    return jnp.dot(a, b, preferred_element_type=F32)


def _dot_hi(a, b):
    return jnp.dot(a, b, preferred_element_type=F32, precision=HIGHEST)


def _silu(x):
    h = 0.5 * x
    return h + h * jnp.tanh(h)


def _layer_norm(h, g, b):
    mu = jnp.mean(h, -1, keepdims=True)
    d = h - mu
    var = jnp.mean(d * d, -1, keepdims=True)
    return d * lax.rsqrt(var + LN_EPS) * g + b


QKV_COLS = 3 * DN_WIDTH
ZU_COLS = DN_WIDTH + POOL_WIDTH


def _proj_all_body(tiles_per_seq, tm, x_ref, w_hbm, wba_ref, cw_ref, gp_ref, tri_ref,
                   qkv_ref, zu_ref, bg_ref, w_ref, halo_ref, raw_q, raw_k, raw_v, wsem):
    i = pl.program_id(0)

    @pl.when(i == 0)
    def _():
        cp = pltpu.make_async_copy(w_hbm, w_ref, wsem)
        cp.start()
        cp.wait()

    first = (i % tiles_per_seq) == 0
    x = x_ref[...].astype(BF16)
    raws = (raw_q, raw_k, raw_v)
    for j in range(3):
        raws[j][0:CONV_HALO, :] = jnp.where(first, 0.0, halo_ref[j])
        r = _dot(x, w_ref[:, j * DN_WIDTH:(j + 1) * DN_WIDTH])
        raws[j][CONV_HALO:, :] = r
        halo_ref[j] = r[tm - CONV_HALO:, :]
    for j in range(ZU_COLS // DN_WIDTH):
        zu_ref[:, j * DN_WIDTH:(j + 1) * DN_WIDTH] = _dot(x, w_ref[:, QKV_COLS + j * DN_WIDTH:QKV_COLS + (j + 1) * DN_WIDTH])
    ba = _dot(x, wba_ref[...])
    beta = jax.nn.sigmoid(ba)
    xx = ba + gp_ref[1:2, :]
    softplus = jnp.maximum(xx, 0.0) + jnp.log1p(jnp.exp(-jnp.abs(xx)))
    g = -jnp.exp(gp_ref[0:1, :]) * softplus
    tri = tri_ref[...]
    tb = tri.shape[0]
    gam = jnp.concatenate([_dot_hi(tri, g[r:r + tb, :]) for r in range(0, tm, tb)], 0)
    lane = lax.broadcasted_iota(I32, ba.shape, 1)
    bg_ref[...] = jnp.where(lane < DN_HEADS, beta, gam)

    for j in range(3):
        for cb in range(DN_HEADS):
            cs = slice(cb * HEAD_DIM, (cb + 1) * HEAD_DIM)
            blk = raws[j][:, cs]
            wcs = slice(j * DN_WIDTH + cb * HEAD_DIM, j * DN_WIDTH + (cb + 1) * HEAD_DIM)
            acc = blk * cw_ref[CONV_WIDTH - 1:CONV_WIDTH, wcs]
            for s in range(1, CONV_WIDTH):
                acc = acc + pltpu.roll(blk, s, 0) * cw_ref[CONV_WIDTH - 1 - s:CONV_WIDTH - s, wcs]
            y = _silu(acc[CONV_HALO:, :])
            if j < 2:
                y = y * lax.rsqrt(jnp.sum(y * y, -1, keepdims=True) + RMS_EPS)
            if j == 0:
                y = y * (HEAD_DIM ** -0.5)
            qkv_ref[:, wcs] = y


def _proj_all(x, w_all, w_ba, conv_w, a_log, dt_bias, S):
    T = x.shape[0]
    tm = min(512, S)
    tb = min(256, tm)
    pad = LANES - 2 * DN_HEADS
    gp = jnp.stack([
        jnp.pad(a_log.astype(F32), (DN_HEADS, pad)),
        jnp.pad(dt_bias.astype(F32), (DN_HEADS, pad)),
    ])
    r = jnp.arange(tb)
    tri = ((r[:, None] >= r[None, :]) & (r[:, None] // CHUNK == r[None, :] // CHUNK)).astype(F32)
    row = lambda i: (i, 0)
    const = lambda i: (0, 0)
    raw = pltpu.VMEM((tm + CONV_HALO, DN_WIDTH), F32)
    return pl.pallas_call(
        functools.partial(_proj_all_body, S // tm, tm),
        grid=(T // tm,),
        in_specs=[
            pl.BlockSpec((tm, D_MODEL), row),
            pl.BlockSpec(memory_space=pl.ANY),
            pl.BlockSpec((D_MODEL, LANES), const),
            pl.BlockSpec((CONV_WIDTH, QKV_COLS), const),
            pl.BlockSpec((2, LANES), const),
            pl.BlockSpec((tb, tb), const),
        ],
        out_specs=[
            pl.BlockSpec((tm, QKV_COLS), row),
            pl.BlockSpec((tm, ZU_COLS), row),
            pl.BlockSpec((tm, LANES), row),
        ],
        out_shape=[
            jax.ShapeDtypeStruct((T, QKV_COLS), F32),
            jax.ShapeDtypeStruct((T, ZU_COLS), F32),
            jax.ShapeDtypeStruct((T, LANES), F32),
        ],
        scratch_shapes=[pltpu.VMEM((D_MODEL, MAIN_COLS), BF16), pltpu.VMEM((3, CONV_HALO, DN_WIDTH), F32),
                        raw, raw, raw, pltpu.SemaphoreType.DMA],
        compiler_params=_cparams(("arbitrary",)),
        name="proj",
    )(x, w_all, w_ba, conv_w.astype(F32), gp, tri)


DELTA_BATCH = 2


def _delta_body(nb, nc, q_ref, k_ref, v_ref, z_ref, bg_ref, gt_ref, nw_ref, o_ref, state_ref):
    @pl.when(pl.program_id(1) == 0)
    def _():
        state_ref[...] = jnp.zeros(state_ref.shape, F32)

    ii = lax.broadcasted_iota(I32, (CHUNK, CHUNK), 0)
    jj = lax.broadcasted_iota(I32, (CHUNK, CHUNK), 1)
    incl = ii >= jj
    strict = ii > jj
    nt = (((1,), (1,)), ((), ()))
    chains = [(b, h) for b in range(nb) for h in range(DN_HEADS)]
    cs = [slice(h * HEAD_DIM, (h + 1) * HEAD_DIM) for _, h in chains]
    n = range(len(chains))

    def chunk(c, carry):
        r0 = pl.multiple_of(c * CHUNK, CHUNK)
        rows = pl.ds(r0, CHUNK)
        bg = [bg_ref[b, rows, :] for b in range(nb)]
        gt = [gt_ref[b, c] for b in range(nb)]
        kh = [k_ref[b, rows, cs[i]] for i, (b, _) in enumerate(chains)]
        qh = [q_ref[b, rows, cs[i]] for i, (b, _) in enumerate(chains)]
        bcol = [bg[b][:, h:h + 1] for b, h in chains]
        gcol = [bg[b][:, DN_HEADS + h:DN_HEADS + h + 1] for b, h in chains]
        grow = [gt[b][h:h + 1, :] for b, h in chains]
        glast = [grow[i][:, CHUNK - 1:CHUNK] for i in n]
        kb = [kh[i] * bcol[i] for i in n]
        s = [lax.dot_general(jnp.concatenate([kb[i], qh[i]], 0).astype(BF16), kh[i].astype(BF16), nt,
                             preferred_element_type=F32) for i in n]
        decay = [jnp.where(incl, jnp.exp(jnp.where(incl, gcol[i] - grow[i], 0.0)), 0.0) for i in n]
        aqk = [(s[i][CHUNK:] * decay[i]).astype(BF16) for i in n]
        pw = [jnp.where(strict, -s[i][:CHUNK] * decay[i], 0.0) for i in n]
        qs = pw
        pwb = [pw[i].astype(BF16) for i in n]
        pw = [_dot(pwb[i], pwb[i]) for i in n]
        for _ in range(4):
            pwb = [pw[i].astype(BF16) for i in n]
            both = [_dot(jnp.concatenate([pwb[i], qs[i].astype(BF16)], 0), pwb[i]) for i in n]
            qs = [qs[i] + pw[i] + both[i][CHUNK:] for i in n]
            pw = [both[i][:CHUNK] for i in n]
        qp = [_dot(qs[i].astype(BF16), pw[i].astype(BF16)) for i in n]
        qs = [qs[i] + pw[i] + qp[i] for i in n]
        eg = [jnp.exp(gcol[i]) for i in n]
        rhs = [jnp.concatenate([v_ref[b, rows, cs[i]] * bcol[i], kb[i] * eg[i]], 1)
               for i, (b, _) in enumerate(chains)]
        sol = [rhs[i] + _dot(qs[i].astype(BF16), rhs[i].astype(BF16)) for i in n]
        st = [state_ref[i] for i in n]
        r = [_dot(jnp.concatenate([sol[i][:, HEAD_DIM:], qh[i] * eg[i]], 0).astype(BF16), st[i].astype(BF16))
             for i in n]
        v_new = [(sol[i][:, :HEAD_DIM] - r[i][:CHUNK]).astype(BF16) for i in n]
        kdt = [(kh[i] * jnp.exp(glast[i] - gcol[i])).T.astype(BF16) for i in n]
        ou = [_dot(jnp.concatenate([aqk[i], kdt[i]], 0), v_new[i]) for i in n]
        for i, (b, _) in enumerate(chains):
            state_ref[i] = st[i] * jnp.exp(glast[i]) + ou[i][CHUNK:]
            zz = z_ref[b, rows, cs[i]]
            oi = r[i][CHUNK:] + ou[i][:CHUNK]
            y = oi * lax.rsqrt(jnp.mean(oi * oi, -1, keepdims=True) + RMS_EPS) * nw_ref[...]
            o_ref[b, rows, cs[i]] = (y * _silu(zz)).astype(o_ref.dtype)
        return carry

    lax.fori_loop(0, nc, chunk, 0)


def _delta(qkv, zu, bg, gt, norm_w, B, S):
    T = qkv.shape[0]
    nb = DELTA_BATCH if B % DELTA_BATCH == 0 else 1
    sblk = min(1024 // nb, S)
    nc = sblk // CHUNK
    seq = lambda a: a.reshape(B, S, a.shape[-1])
    blk = lambda b, s: (b, s, 0)
    y = pl.pallas_call(
        functools.partial(_delta_body, nb, nc),
        grid=(B // nb, S // sblk),
        in_specs=[
            pl.BlockSpec((nb, sblk, DN_WIDTH), blk),
            pl.BlockSpec((nb, sblk, DN_WIDTH), lambda b, s: (b, s, 1)),
            pl.BlockSpec((nb, sblk, DN_WIDTH), lambda b, s: (b, s, 2)),
            pl.BlockSpec((nb, sblk, DN_WIDTH), blk),
            pl.BlockSpec((nb, sblk, LANES), blk),
            pl.BlockSpec((nb, nc, DN_HEADS, CHUNK), lambda b, s: (b, s, 0, 0)),
            pl.BlockSpec((1, HEAD_DIM), lambda b, s: (0, 0)),
        ],
        out_specs=pl.BlockSpec((nb, sblk, DN_WIDTH), blk),
        out_shape=jax.ShapeDtypeStruct((B, S, DN_WIDTH), BF16),
        scratch_shapes=[pltpu.VMEM((nb * DN_HEADS, HEAD_DIM, HEAD_DIM), F32)],
        compiler_params=_cparams(("parallel", "arbitrary")),
        name="delta",
    )(seq(qkv), seq(qkv), seq(qkv), seq(zu), seq(bg), gt, norm_w.astype(F32).reshape(1, HEAD_DIM))
    return y.reshape(T, DN_WIDTH)


def _pool_body(tiles_per_seq, tm, u_ref, halo_ref, w_ref, sc_ref, o_ref, us_ref):
    t_in_seq = (pl.program_id(0) % tiles_per_seq) * tm
    first = t_in_seq == 0
    us_ref[0:POOL_HALO, :] = jnp.where(first, 0.0, halo_ref[...])
    us_ref[POOL_HALO:POOL_HALO + tm, :] = u_ref[...]
    tpos = (t_in_seq + lax.broadcasted_iota(I32, (tm, 1), 0) + 1).astype(F32)
    for gi, win in enumerate(POOL_WINDOWS):
        cs = slice(gi * POOL_GROUP_DIM, (gi + 1) * POOL_GROUP_DIM)
        blk = us_ref[:, cs]
        wsum = blk
        span = 1
        while span < win:
            wsum = wsum + pltpu.roll(wsum, span, 0)
            span *= 2
        cur = blk[POOL_HALO:, :]
        d = wsum[POOL_HALO:, :] / jnp.minimum(tpos, float(win)) - cur
        y = _dot(d.astype(BF16), w_ref[gi])
        o_ref[:, cs] = (y * sc_ref[:, cs]).astype(o_ref.dtype)


def _pool(proj, pool_w, pool_scale, S):
    T = proj.shape[0]
    tm = min(256, S)
    ucol = DN_WIDTH // POOL_WIDTH
    halo_blocks = tm // POOL_HALO
    return pl.pallas_call(
        functools.partial(_pool_body, S // tm, tm),
        grid=(T // tm,),
        in_specs=[
            pl.BlockSpec((tm, POOL_WIDTH), lambda i: (i, ucol)),
            pl.BlockSpec((POOL_HALO, POOL_WIDTH), lambda i: (jnp.maximum(i * halo_blocks - 1, 0), ucol)),
            pl.BlockSpec((len(POOL_WINDOWS), POOL_GROUP_DIM, POOL_GROUP_DIM), lambda i: (0, 0, 0)),
            pl.BlockSpec((1, POOL_WIDTH), lambda i: (0, 0)),
        ],
        out_specs=pl.BlockSpec((tm, POOL_WIDTH), lambda i: (i, 0)),
        out_shape=jax.ShapeDtypeStruct((T, POOL_WIDTH), BF16),
        scratch_shapes=[pltpu.VMEM((tm + POOL_HALO, POOL_WIDTH), F32)],
        compiler_params=_cparams(("parallel",)),
        name="pool",
    )(proj, proj, pool_w.astype(BF16), pool_scale.astype(F32).reshape(1, POOL_WIDTH))


OUTPROJ_SPLIT = 2
PACK_COLS = D_MODEL // 2
U32 = jnp.uint32


def _pack_halves(x):
    lo = lax.bitcast_convert_type(x[:, :PACK_COLS].astype(BF16).astype(F32), U32)
    hi = lax.bitcast_convert_type(x[:, PACK_COLS:].astype(BF16).astype(F32), U32)
    return (lo >> 16) | hi


def _unpack_halves(p):
    lo = lax.bitcast_convert_type(p << 16, F32)
    hi = lax.bitcast_convert_type(p & jnp.uint32(0xFFFF0000), F32)
    return lo, hi


def _route_tile(tm, logits, upper_ref, eidx_ref, gate_ref, rank_ref, cnt_ref, carry_ref):
    m = jnp.max(logits, axis=0, keepdims=True)
    e = jnp.exp(logits - m)
    p = e / jnp.sum(e, axis=0, keepdims=True)
    rows = [p[i:i + 1, :] for i in range(N_EXPERTS)]

    scores = []
    for g in range(N_GROUPS):
        a, b, c, d = rows[EXPERTS_PER_GROUP * g:EXPERTS_PER_GROUP * (g + 1)]
        hi1, lo1 = jnp.maximum(a, b), jnp.minimum(a, b)
        hi2, lo2 = jnp.maximum(c, d), jnp.minimum(c, d)
        top1 = jnp.maximum(hi1, hi2)
        top2 = jnp.maximum(jnp.minimum(hi1, hi2), jnp.where(hi1 >= hi2, lo1, lo2))
        scores.append(top1 + top2)
    gsel = jnp.zeros((1, tm), I32)
    best = scores[0]
    for g in range(1, N_GROUPS):
        better = scores[g] > best
        gsel = jnp.where(better, g, gsel)
        best = jnp.where(better, scores[g], best)
    ing = []
    for j in range(EXPERTS_PER_GROUP):
        sel = rows[(N_GROUPS - 1) * EXPERTS_PER_GROUP + j]
        for g in range(N_GROUPS - 2, -1, -1):
            sel = jnp.where(gsel == g, rows[g * EXPERTS_PER_GROUP + j], sel)
        ing.append(sel)
    i1 = jnp.zeros((1, tm), I32)
    p1 = ing[0]
    for j in range(1, EXPERTS_PER_GROUP):
        better = ing[j] > p1
        i1 = jnp.where(better, j, i1)
        p1 = jnp.where(better, ing[j], p1)
    i2 = jnp.zeros((1, tm), I32)
    p2 = jnp.full((1, tm), -1.0, F32)
    for j in range(EXPERTS_PER_GROUP):
        cand = jnp.where(i1 == j, -1.0, ing[j])
        better = cand > p2
        i2 = jnp.where(better, j, i2)
        p2 = jnp.where(better, cand, p2)
    den = p1 + p2
    e0 = gsel * EXPERTS_PER_GROUP + i1
    e1 = gsel * EXPERTS_PER_GROUP + i2
    eidx_ref[0:1, :] = e0
    eidx_ref[1:2, :] = e1
    gate_ref[0:1, :] = p1 / den
    gate_ref[1:2, :] = p2 / den

    er = lax.broadcasted_iota(I32, (N_EXPERTS, tm), 0)
    oh0 = er == e0
    oh1 = er == e1
    oh = jnp.where(oh0 | oh1, 1.0, 0.0)
    before = carry_ref[:, 0:1] + _dot(oh.astype(BF16), upper_ref[...])
    rank_ref[0:1, :] = jnp.sum(jnp.where(oh0, before, 0.0), axis=0, keepdims=True).astype(I32)
    rank_ref[1:2, :] = jnp.sum(jnp.where(oh1, before, 0.0), axis=0, keepdims=True).astype(I32)
    total = carry_ref[...] + jnp.sum(oh, axis=1, keepdims=True)
    carry_ref[...] = total
    cnt_ref[...] = total.astype(I32)


def _outproj_body(tm, ydn_ref, ypool_ref, w_ref, x_ref, g_ref, b_ref, wrh_ref, wrl_ref, br_ref, upper_ref,
                  o_ref, op_ref, eidx_ref, gate_ref, rank_ref, cnt_ref, carry_ref):
    @pl.when(pl.program_id(0) == 0)
    def _():
        carry_ref[...] = jnp.zeros(carry_ref.shape, F32)

    subs = [slice(k * (tm // OUTPROJ_SPLIT), (k + 1) * (tm // OUTPROJ_SPLIT)) for k in range(OUTPROJ_SPLIT)]
    mix = [_dot(ydn_ref[r, :], w_ref[0:DN_WIDTH, :]) + _dot(ypool_ref[r, :], w_ref[DN_WIDTH:, :]) for r in subs]
    x1 = [_layer_norm(ALPHA * x_ref[r, :] + m, g_ref[...], b_ref[...]) for r, m in zip(subs, mix)]
    xh = [x.astype(BF16) for x in x1]
    xl = [(x - h.astype(F32)).astype(BF16) for x, h in zip(x1, xh)]
    lg = [_dot(h, wrh_ref[...]) + (_dot(l, wrh_ref[...]) + _dot(h, wrl_ref[...])) for h, l in zip(xh, xl)]
    for r, x in zip(subs, x1):
        o_ref[r, :] = x
        op_ref[r, :] = _pack_halves(x)
    logits = jnp.concatenate(lg, 0).T[0:N_EXPERTS, :] + br_ref[:, 0:1]
    _route_tile(tm, logits, upper_ref, eidx_ref, gate_ref, rank_ref, cnt_ref, carry_ref)


def _outproj(y_dn, y_pool, w_out, x, g, b, w_router, b_router):
    T = x.shape[0]
    tm = min(512, T)
    r = jnp.arange(tm)
    upper = (r[:, None] < r[None, :]).astype(BF16)
    wr = jnp.pad(w_router.astype(F32), ((0, 0), (0, LANES - N_EXPERTS)))
    wr_hi = wr.astype(BF16)
    wr_lo = (wr - wr_hi.astype(F32)).astype(BF16)
    row = lambda i: (i, 0)
    tok = lambda i: (0, i)
    const = lambda i: (0, 0)
    return pl.pallas_call(
        functools.partial(_outproj_body, tm),
        grid=(T // tm,),
        in_specs=[
            pl.BlockSpec((tm, DN_WIDTH), row),
            pl.BlockSpec((tm, POOL_WIDTH), row),
            pl.BlockSpec((D_MODEL, D_MODEL), const),
            pl.BlockSpec((tm, D_MODEL), row),
            pl.BlockSpec((1, D_MODEL), const),
            pl.BlockSpec((1, D_MODEL), const),
            pl.BlockSpec((D_MODEL, LANES), const),
            pl.BlockSpec((D_MODEL, LANES), const),
            pl.BlockSpec((N_EXPERTS, LANES), const),
            pl.BlockSpec((tm, tm), const),
        ],
        out_specs=[
            pl.BlockSpec((tm, D_MODEL), row),
            pl.BlockSpec((tm, PACK_COLS), row),
            pl.BlockSpec((2, tm), tok),
            pl.BlockSpec((2, tm), tok),
            pl.BlockSpec((2, tm), tok),
            pl.BlockSpec((N_EXPERTS, LANES), const),
        ],
        out_shape=[
            jax.ShapeDtypeStruct((T, D_MODEL), F32),
            jax.ShapeDtypeStruct((T, PACK_COLS), U32),
            jax.ShapeDtypeStruct((2, T), I32),
            jax.ShapeDtypeStruct((2, T), F32),
            jax.ShapeDtypeStruct((2, T), I32),
            jax.ShapeDtypeStruct((N_EXPERTS, LANES), I32),
        ],
        scratch_shapes=[pltpu.VMEM((N_EXPERTS, LANES), F32)],
        compiler_params=_cparams(("arbitrary",)),
        name="outproj",
    )(y_dn, y_pool, w_out.astype(BF16), x, g.astype(F32).reshape(1, D_MODEL), b.astype(F32).reshape(1, D_MODEL),
      wr_hi, wr_lo, jnp.broadcast_to(b_router.astype(F32)[:, None], (N_EXPERTS, LANES)), upper)


DUMP_ROWS = 2 * EXPERT_BLOCK


def _dispatch_body(tm, T, n_blocks, pos_ref, pad_start_ref, pad_len_ref, nb_ref, x_ref, xs_ref, inv_ref,
                   zrow_ref, sem, zsem):
    step = pl.program_id(0)

    def zero_copy(dst):
        return pltpu.make_async_copy(zrow_ref.at[pl.ds(0, 1)], xs_ref.at[pl.ds(dst, 1)], zsem)

    def dump_row(r):
        return 2 * T + lax.rem(r, DUMP_ROWS)

    @pl.when(step == 0)
    def _():
        zrow_ref[...] = jnp.zeros(zrow_ref.shape, zrow_ref.dtype)
        for e in range(N_EXPERTS):
            def fill(r, carry, e=e):
                zero_copy(pad_start_ref[e] + r).start()
                inv_ref[pad_start_ref[e] + r] = dump_row(pad_start_ref[e] + r)
                return carry
            lax.fori_loop(0, pad_len_ref[e], fill, 0)

        def tail_inv(r, carry):
            inv_ref[r] = dump_row(r)
            return carry
        lax.fori_loop(nb_ref[0] * EXPERT_BLOCK, n_blocks * EXPERT_BLOCK, tail_inv, 0)
        for e in range(N_EXPERTS):
            def drain(r, carry):
                zero_copy(0).wait()
                return carry
            lax.fori_loop(0, pad_len_ref[e], drain, 0)

        def tail_copy(blk):
            return pltpu.make_async_copy(zrow_ref, xs_ref.at[pl.ds(blk * EXPERT_BLOCK, EXPERT_BLOCK)], zsem)

        def tail_fill(blk, carry):
            tail_copy(blk).start()
            return carry

        def tail_drain(blk, carry):
            tail_copy(blk).wait()
            return carry
        lax.fori_loop(nb_ref[0], n_blocks, tail_fill, 0)
        lax.fori_loop(nb_ref[0], n_blocks, tail_drain, 0)

    base = step * tm
    for r in range(tm):
        for k in range(2):
            dst = pos_ref[k * T + base + r]
            inv_ref[dst] = k * T + base + r
            pltpu.make_async_copy(x_ref.at[pl.ds(r, 1)], xs_ref.at[pl.ds(dst, 1)], sem).start(priority=k)
    for k in range(2):
        pltpu.make_async_copy(x_ref, xs_ref.at[pl.ds(0, tm)], sem).wait()


def _dispatch(pos_flat, pad_start, pad_len, nb, x1, n_rows):
    T = x1.shape[0]
    tm = min(512, T)
    return pl.pallas_call(
        functools.partial(_dispatch_body, tm, T, n_rows // EXPERT_BLOCK),
        grid_spec=pltpu.PrefetchScalarGridSpec(
            num_scalar_prefetch=4,
            grid=(T // tm,),
            in_specs=[pl.BlockSpec((tm, PACK_COLS), lambda i, *_: (i, 0))],
            out_specs=[pl.BlockSpec(memory_space=pl.ANY), pl.BlockSpec(memory_space=pltpu.SMEM)],
            scratch_shapes=[pltpu.VMEM((EXPERT_BLOCK, PACK_COLS), U32), pltpu.SemaphoreType.DMA,
                            pltpu.SemaphoreType.DMA],
        ),
        out_shape=[jax.ShapeDtypeStruct((n_rows, PACK_COLS), U32), jax.ShapeDtypeStruct((n_rows,), I32)],
        compiler_params=_cparams(("arbitrary",)),
        name="dispatch",
    )(pos_flat, pad_start, pad_len, nb, x1)


EXPERT_SPLIT = 2
W_CHUNKS = 8
N_CHUNKS = 3 * W_CHUNKS
N_STAGE = 4
GU_ROWS = D_MODEL // W_CHUNKS
DN_ROWS = D_EXPERT // W_CHUNKS


def _experts_body(layer, be_ref, nb_ref, slot_ref, c0_ref, c1_ref, nxt_ref, first_ref, inv_ref,
                  xs_ref, wg_hbm, wu_hbm, wd_hbm, yk_ref, wgu_ref, wdn_ref, stg_a, stg_d, ybuf_ref,
                  sem_a, sem_d, osem):
    i = pl.program_id(0)

    def chunk_copy(kind, e, idx, b):
        if kind == 0:
            return pltpu.make_async_copy(wg_hbm.at[layer, e, pl.ds(idx * GU_ROWS, GU_ROWS), :], stg_a.at[b], sem_a.at[b])
        if kind == 1:
            return pltpu.make_async_copy(wu_hbm.at[layer, e, pl.ds(idx * GU_ROWS, GU_ROWS), :], stg_a.at[b], sem_a.at[b])
        return pltpu.make_async_copy(wd_hbm.at[layer, e, pl.ds(idx * DN_ROWS, DN_ROWS), :], stg_d.at[b], sem_d.at[b])

    def start_chunk(e, c):
        for kind in range(3):
            @pl.when(c // W_CHUNKS == kind)
            def _():
                chunk_copy(kind, e, c % W_CHUNKS, c % N_STAGE).start()

    def finish_chunk(e, c, slot):
        for kind in range(3):
            @pl.when(c // W_CHUNKS == kind)
            def _():
                idx = c % W_CHUNKS
                b = c % N_STAGE
                chunk_copy(kind, e, idx, b).wait()
                if kind < 2:
                    rows = pl.ds(pl.multiple_of(idx * GU_ROWS, GU_ROWS), GU_ROWS)
                    wgu_ref[slot, kind, rows, :] = stg_a[b].astype(BF16)
                else:
                    rows = pl.ds(pl.multiple_of(idx * DN_ROWS, DN_ROWS), DN_ROWS)
                    wdn_ref[slot, rows, :] = stg_d[b].astype(BF16)

    def stream(e, slot, lo, hi):
        def body(c, carry):
            finish_chunk(e, c, slot)

            @pl.when(c + N_STAGE < N_CHUNKS)
            def _():
                start_chunk(e, c + N_STAGE)
            return carry
        lax.fori_loop(lo, hi, body, 0)

    def prime(e):
        for c in range(N_STAGE):
            start_chunk(e, jnp.int32(c))

    dump = yk_ref.shape[0] - DUMP_ROWS

    def out_copy(slot, r, dst):
        return pltpu.make_async_copy(ybuf_ref.at[slot, pl.ds(r, 1)], yk_ref.at[pl.ds(dst, 1)], osem.at[slot])

    def wait_out(slot):
        pltpu.make_async_copy(ybuf_ref.at[slot], yk_ref.at[pl.ds(0, EXPERT_BLOCK)], osem.at[slot]).wait()

    @pl.when(i == 0)
    def _():
        prime(be_ref[0])
        stream(be_ref[0], 0, 0, N_CHUNKS)
        ybuf_ref[...] = jnp.zeros(ybuf_ref.shape, ybuf_ref.dtype)

        def fill(r, carry):
            out_copy(0, r, dump + r).start()
            return carry
        lax.fori_loop(0, EXPERT_BLOCK, fill, 0)

    @pl.when(i < nb_ref[0])
    def _():
        s = slot_ref[i]
        ne = nxt_ref[i]
        par = i % 2

        @pl.when((first_ref[i] == 1) & (ne >= 0))
        def _():
            prime(ne)

        stream(ne, 1 - s, c0_ref[i], c1_ref[i])
        wait_out(par)
        prev = jnp.maximum(i - 1, 0) * EXPERT_BLOCK
        for r in range(EXPERT_BLOCK):
            dst = jnp.where(i >= 1, inv_ref[prev + r], dump + EXPERT_BLOCK + r)
            out_copy(1 - par, r, dst).start(priority=r % 2)
        sub = EXPERT_BLOCK // EXPERT_SPLIT
        hid = []
        for k in range(EXPERT_SPLIT):
            lo, hi = _unpack_halves(xs_ref[k * sub:(k + 1) * sub, :])
            lo = lo.astype(BF16)
            hi = hi.astype(BF16)
            g = _dot(lo, wgu_ref[s, 0, 0:PACK_COLS, :]) + _dot(hi, wgu_ref[s, 0, PACK_COLS:, :])
            u = _dot(lo, wgu_ref[s, 1, 0:PACK_COLS, :]) + _dot(hi, wgu_ref[s, 1, PACK_COLS:, :])
            hid.append((_silu(g) * u).astype(BF16))
        y = [_dot(h, wdn_ref[s]) for h in hid]
        for k in range(EXPERT_SPLIT):
            ybuf_ref[par, k * sub:(k + 1) * sub, :] = _pack_halves(y[k])

    @pl.when(i == nb_ref[0])
    def _():
        par = i % 2
        wait_out(par)
        last = (i - 1) * EXPERT_BLOCK

        def send(r, carry):
            out_copy(1 - par, r, inv_ref[last + r]).start()
            return carry
        lax.fori_loop(0, EXPERT_BLOCK, send, 0)
        wait_out(1 - par)


def _expert_schedule(block_e, nb, n_blocks):
    idx = jnp.arange(n_blocks, dtype=I32)
    valid = idx < nb[0]
    prev = jnp.concatenate([block_e[:1] - 1, block_e[:-1]])
    first = valid & ((idx == 0) | (block_e != prev))
    run_id = jnp.cumsum(first.astype(I32)) - 1
    run_start = lax.cummax(jnp.where(first, idx, 0))
    run_len = jnp.sum((run_id[:, None] == run_id[None, :]) & valid[None, :], -1).astype(I32)
    j = idx - run_start
    nxt_idx = run_start + run_len
    nxt = jnp.where(valid & (nxt_idx < nb[0]), block_e[jnp.minimum(nxt_idx, n_blocks - 1)], -1).astype(I32)
    n = jnp.maximum(run_len, 1)
    has = nxt >= 0
    c0 = jnp.where(has, j * N_CHUNKS // n, 0).astype(I32)
    c1 = jnp.where(has, (j + 1) * N_CHUNKS // n, 0).astype(I32)
    return (run_id % 2).astype(I32), c0, c1, nxt, first.astype(I32)


def _experts(layer, block_e, nb, inv, xs, wg, wu, wd, T):
    n_rows = xs.shape[0]
    bm = EXPERT_BLOCK
    n_blocks = n_rows // bm
    slot, c0, c1, nxt, first = _expert_schedule(block_e, nb, n_blocks)
    blk = lambda i, be, nb, *_: (jnp.maximum(jnp.minimum(i, nb[0] - 1), 0), 0)
    hbm = pl.BlockSpec(memory_space=pl.ANY)
    return pl.pallas_call(
        functools.partial(_experts_body, layer),
        grid_spec=pltpu.PrefetchScalarGridSpec(
            num_scalar_prefetch=8,
            grid=(n_blocks,),
            in_specs=[pl.BlockSpec((bm, PACK_COLS), blk), hbm, hbm, hbm],
            out_specs=hbm,
            scratch_shapes=[
                pltpu.VMEM((2, 2, D_MODEL, D_EXPERT), BF16),
                pltpu.VMEM((2, D_EXPERT, D_MODEL), BF16),
                pltpu.VMEM((N_STAGE, GU_ROWS, D_EXPERT), F32),
                pltpu.VMEM((N_STAGE, DN_ROWS, D_MODEL), F32),
                pltpu.VMEM((2, bm, PACK_COLS), U32),
                pltpu.SemaphoreType.DMA((N_STAGE,)),
                pltpu.SemaphoreType.DMA((N_STAGE,)),
                pltpu.SemaphoreType.DMA((2,)),
            ],
        ),
        out_shape=jax.ShapeDtypeStruct((2 * T + DUMP_ROWS, PACK_COLS), U32),
        compiler_params=_cparams(("arbitrary",)),
        name="experts",
    )(block_e, nb, slot, c0, c1, nxt, first, inv, xs, wg, wu, wd)


COMBINE_SPLIT = 2


def _combine_body(tm, y0_ref, y1_ref, gate_ref, x1_ref, p_ref, wpg_ref, wpp_ref, g_ref, b_ref, o_ref, ob_ref):
    sub = tm // COMBINE_SPLIT
    subs = [slice(h * sub, (h + 1) * sub) for h in range(COMBINE_SPLIT)]
    x2, gl = [], []
    for rows in subs:
        gate = gate_ref[rows, :]
        lo0, hi0 = _unpack_halves(y0_ref[rows, :])
        lo1, hi1 = _unpack_halves(y1_ref[rows, :])
        g0 = gate[:, 0:1]
        g1 = gate[:, 1:2]
        ffn = jnp.concatenate([lo0 * g0 + lo1 * g1, hi0 * g0 + hi1 * g1], 1)
        x2.append(_layer_norm(ALPHA * x1_ref[rows, :] + ffn, g_ref[...], b_ref[...]))
        gl.append(_dot(x2[-1].astype(BF16), wpg_ref[...]))
    pp = [_dot(p_ref[rows, :].astype(BF16), wpp_ref[...]) for rows in subs]
    for rows, x, a, b in zip(subs, x2, gl, pp):
        x3 = x + jax.nn.sigmoid(a) * b
        o_ref[rows, :] = x3
        ob_ref[rows, :] = x3.astype(BF16)


def _combine(yk, gate_t, x1, p, w_ple_gate, w_ple_proj, g, b):
    T = x1.shape[0]
    tm = min(512, T)
    row = lambda i: (i, 0)
    const = lambda i: (0, 0)
    second = T // tm
    return pl.pallas_call(
        functools.partial(_combine_body, tm),
        grid=(T // tm,),
        in_specs=[
            pl.BlockSpec((tm, PACK_COLS), row),
            pl.BlockSpec((tm, PACK_COLS), lambda i: (second + i, 0)),
            pl.BlockSpec((tm, 2), row),
            pl.BlockSpec((tm, D_MODEL), row),
            pl.BlockSpec((tm, PLE_DIM), row),
            pl.BlockSpec((D_MODEL, D_MODEL), const),
            pl.BlockSpec((PLE_DIM, D_MODEL), const),
            pl.BlockSpec((1, D_MODEL), const),
            pl.BlockSpec((1, D_MODEL), const),
        ],
        out_specs=[
            pl.BlockSpec((tm, D_MODEL), row),
            pl.BlockSpec((tm, D_MODEL), row),
        ],
        out_shape=[
            jax.ShapeDtypeStruct((T, D_MODEL), F32),
            jax.ShapeDtypeStruct((T, D_MODEL), BF16),
        ],
        compiler_params=_cparams(("parallel",)),
        name="combine",
    )(yk, yk, gate_t, x1, p, w_ple_gate.astype(BF16), w_ple_proj.astype(BF16),
      g.astype(F32).reshape(1, D_MODEL), b.astype(F32).reshape(1, D_MODEL))


def _route_plan(eidx, rank, cnt, n_blocks):
    bm = EXPERT_BLOCK
    counts = cnt[:, 0]
    pcounts = (counts + bm - 1) // bm * bm
    pend = jnp.cumsum(pcounts)
    pstart = pend - pcounts
    onehot = eidx[:, :, None] == jnp.arange(N_EXPERTS, dtype=I32)
    pos = jnp.sum(jnp.where(onehot, pstart, 0), -1) + rank
    nb = (pend[-1] // bm).astype(I32).reshape(1)
    block_start = jnp.arange(n_blocks, dtype=I32) * bm
    block_e = jnp.minimum(jnp.sum(block_start[:, None] >= pend[None, :], -1), N_EXPERTS - 1).astype(I32)
    pad_start = (pstart + counts).astype(I32)
    pad_len = (pcounts - counts).astype(I32)
    return pos.reshape(-1).astype(I32), block_e, nb, pad_start, pad_len


def kernel(x, p, w_in, conv_w, a_log, dt_bias, dn_norm_w, pool_w, pool_scale, w_out, ln1_g, ln1_b,
           w_router, b_router, w_e_gate, w_e_up, w_e_down, ln2_g, ln2_b, w_ple_proj, w_ple_gate):
    B, S, D = x.shape
    T = B * S
    n_rows = 2 * T + N_EXPERTS * EXPERT_BLOCK
    xf = x.reshape(T, D).astype(F32)
    xb = xf
    for i in range(DEPTH):
        wi = w_in[i]
        w_all = jnp.concatenate([wi[:, :4 * DN_WIDTH], wi[:, 4 * DN_WIDTH + 2 * DN_HEADS:]], 1).astype(BF16)
        w_ba = jnp.pad(wi[:, 4 * DN_WIDTH:4 * DN_WIDTH + 2 * DN_HEADS],
                       ((0, 0), (0, LANES - 2 * DN_HEADS))).astype(BF16)
        qkv, zu, bg = _proj_all(xb, w_all, w_ba, conv_w[i], a_log[i], dt_bias[i], S)
        gt = bg[:, DN_HEADS:2 * DN_HEADS].reshape(B, S // CHUNK, CHUNK, DN_HEADS).transpose(0, 1, 3, 2)
        y_dn = _delta(qkv, zu, bg, gt, dn_norm_w[i], B, S)
        y_pool = _pool(zu, pool_w[i], pool_scale[i], S)
        x1, x1p, eidx, gate, rank, cnt = _outproj(y_dn, y_pool, w_out[i], xf, ln1_g[i], ln1_b[i],
                                                  w_router, b_router)
        pos_flat, block_e, nb, pad_start, pad_len = _route_plan(eidx, rank, cnt, n_rows // EXPERT_BLOCK)
        xs, inv = _dispatch(pos_flat, pad_start, pad_len, nb, x1p, n_rows)
        yk = _experts(i, block_e, nb, inv, xs, w_e_gate, w_e_up, w_e_down, T)
        xf, xb = _combine(yk, gate.T, x1, p[i].reshape(T, PLE_DIM), w_ple_gate[i], w_ple_proj[i],
                          ln2_g[i], ln2_b[i])
    return xf.reshape(B, S, D).astype(x.dtype)
```

```python
import functools

import jax
import jax.numpy as jnp
from jax import lax
from jax.experimental import pallas as pl
from jax.experimental.pallas import tpu as pltpu

F32 = jnp.float32
BF16 = jnp.bfloat16
I32 = jnp.int32
HIGHEST = lax.Precision.HIGHEST

D_MODEL = 2048
DN_HEADS = 8
HEAD_DIM = 128
DN_WIDTH = DN_HEADS * HEAD_DIM
CONV_WIDTH = 4
CHUNK = 64
POOL_WINDOWS = (2, 4, 8, 16)
POOL_GROUP_DIM = 256
POOL_WIDTH = 1024
N_EXPERTS = 16
N_GROUPS = 4
EXPERTS_PER_GROUP = 4
D_EXPERT = 1024
PLE_DIM = 256
DEPTH = 2
ALPHA = (2.0 * DEPTH) ** 0.25
LN_EPS = 1e-5
RMS_EPS = 1e-6

LANES = 128
MAIN_COLS = 4 * DN_WIDTH + POOL_WIDTH
CONV_HALO = 8
POOL_HALO = 16
EXPERT_BLOCK = 256
V7X_VMEM_MIB = 64
VMEM_MAX_MIB = V7X_VMEM_MIB - 8
VMEM_MID_MIB = V7X_VMEM_MIB - 16
VMEM_SMALL_MIB = V7X_VMEM_MIB // 2


def _cparams(sem, vmem_mib):
    return pltpu.CompilerParams(dimension_semantics=sem, vmem_limit_bytes=vmem_mib * 1024 * 1024)


def _dot(a, b):
    return jnp.dot(a, b, preferred_element_type=F32)


def _dot_hi(a, b):
    return jnp.dot(a, b, preferred_element_type=F32, precision=HIGHEST)


def _silu(x):
    h = 0.5 * x
    return h + h * jnp.tanh(h)


W_STAGE_ROWS = 256


def _load_weight_bf16(w_hbm, layer, w_ref, stage_ref, sem):
    rows = stage_ref.shape[1]
    n = w_ref.shape[0] // rows

    def chunk(c):
        return pltpu.make_async_copy(w_hbm.at[layer, pl.ds(c * rows, rows), :], stage_ref.at[c % 2], sem.at[c % 2])

    chunk(0).start()
    for c in range(n):
        if c + 1 < n:
            chunk(c + 1).start()
        chunk(c).wait()
        w_ref[c * rows:(c + 1) * rows, :] = stage_ref[c % 2].astype(BF16)


def _weight_scratch(rows, cols):
    return [pltpu.VMEM((rows, cols), BF16), pltpu.VMEM((2, W_STAGE_ROWS, cols), F32), pltpu.SemaphoreType.DMA((2,))]


def _layer_norm(h, g, b):
    mu = jnp.mean(h, -1, keepdims=True)
    d = h - mu
    var = jnp.mean(d * d, -1, keepdims=True)
    return d * lax.rsqrt(var + LN_EPS) * g + b


QKV_COLS = 3 * DN_WIDTH
ZU_COLS = DN_WIDTH + POOL_WIDTH


def _proj_all_body(tiles_per_seq, tm, x_ref, w_hbm, wba_ref, cw_ref, gp_ref, tri_ref,
                   qkv_ref, zu_ref, bg_ref, w_ref, halo_ref, raw_q, raw_k, raw_v, wsem):
    i = pl.program_id(0)

    @pl.when(i == 0)
    def _():
        cp = pltpu.make_async_copy(w_hbm, w_ref, wsem)
        cp.start()
        cp.wait()

    first = (i % tiles_per_seq) == 0
    x = x_ref[...].astype(BF16)
    raws = (raw_q, raw_k, raw_v)
    for j in range(3):
        raws[j][0:CONV_HALO, :] = jnp.where(first, 0.0, halo_ref[j])
        r = _dot(x, w_ref[:, j * DN_WIDTH:(j + 1) * DN_WIDTH])
        raws[j][CONV_HALO:, :] = r
        halo_ref[j] = r[tm - CONV_HALO:, :]
    for j in range(ZU_COLS // DN_WIDTH):
        zu_ref[:, j * DN_WIDTH:(j + 1) * DN_WIDTH] = _dot(x, w_ref[:, QKV_COLS + j * DN_WIDTH:QKV_COLS + (j + 1) * DN_WIDTH])
    ba = _dot(x, wba_ref[...])
    beta = jax.nn.sigmoid(ba)
    xx = ba + gp_ref[1:2, :]
    softplus = jnp.maximum(xx, 0.0) + jnp.log1p(jnp.exp(-jnp.abs(xx)))
    g = -jnp.exp(gp_ref[0:1, :]) * softplus
    tri = tri_ref[...]
    tb = tri.shape[0]
    gam = jnp.concatenate([_dot_hi(tri, g[r:r + tb, :]) for r in range(0, tm, tb)], 0)
    lane = lax.broadcasted_iota(I32, ba.shape, 1)
    bg_ref[...] = jnp.where(lane < DN_HEADS, beta, gam)

    for j in range(3):
        for cb in range(DN_HEADS):
            cs = slice(cb * HEAD_DIM, (cb + 1) * HEAD_DIM)
            blk = raws[j][:, cs]
            wcs = slice(j * DN_WIDTH + cb * HEAD_DIM, j * DN_WIDTH + (cb + 1) * HEAD_DIM)
            acc = blk * cw_ref[CONV_WIDTH - 1:CONV_WIDTH, wcs]
            for s in range(1, CONV_WIDTH):
                acc = acc + pltpu.roll(blk, s, 0) * cw_ref[CONV_WIDTH - 1 - s:CONV_WIDTH - s, wcs]
            y = _silu(acc[CONV_HALO:, :])
            if j < 2:
                y = y * lax.rsqrt(jnp.sum(y * y, -1, keepdims=True) + RMS_EPS)
            if j == 0:
                y = y * (HEAD_DIM ** -0.5)
            qkv_ref[:, wcs] = y


def _proj_all(x, w_all, w_ba, conv_w, a_log, dt_bias, S):
    T = x.shape[0]
    tm = min(512, S)
    tb = min(256, tm)
    pad = LANES - 2 * DN_HEADS
    gp = jnp.stack([
        jnp.pad(a_log.astype(F32), (DN_HEADS, pad)),
        jnp.pad(dt_bias.astype(F32), (DN_HEADS, pad)),
    ])
    r = jnp.arange(tb)
    tri = ((r[:, None] >= r[None, :]) & (r[:, None] // CHUNK == r[None, :] // CHUNK)).astype(F32)
    row = lambda i: (i, 0)
    const = lambda i: (0, 0)
    raw = pltpu.VMEM((tm + CONV_HALO, DN_WIDTH), F32)
    return pl.pallas_call(
        functools.partial(_proj_all_body, S // tm, tm),
        grid=(T // tm,),
        in_specs=[
            pl.BlockSpec((tm, D_MODEL), row),
            pl.BlockSpec(memory_space=pltpu.HBM),
            pl.BlockSpec((D_MODEL, LANES), const),
            pl.BlockSpec((CONV_WIDTH, QKV_COLS), const),
            pl.BlockSpec((2, LANES), const),
            pl.BlockSpec((tb, tb), const),
        ],
        out_specs=[
            pl.BlockSpec((tm, QKV_COLS), row),
            pl.BlockSpec((tm, ZU_COLS), row),
            pl.BlockSpec((tm, LANES), row),
        ],
        out_shape=[
            jax.ShapeDtypeStruct((T, QKV_COLS), F32),
            jax.ShapeDtypeStruct((T, ZU_COLS), F32),
            jax.ShapeDtypeStruct((T, LANES), F32),
        ],
        scratch_shapes=[pltpu.VMEM((D_MODEL, MAIN_COLS), BF16), pltpu.VMEM((3, CONV_HALO, DN_WIDTH), F32),
                        raw, raw, raw, pltpu.SemaphoreType.DMA],
        compiler_params=_cparams(("arbitrary",), VMEM_MAX_MIB),
        name="proj",
    )(x, w_all, w_ba, conv_w.astype(F32), gp, tri)


DELTA_BATCH = 2


def _delta_body(nb, nc, q_ref, k_ref, v_ref, z_ref, bg_ref, gt_ref, nw_ref, o_ref, state_ref):
    @pl.when(pl.program_id(1) == 0)
    def _():
        state_ref[...] = jnp.zeros(state_ref.shape, F32)

    ii = lax.broadcasted_iota(I32, (CHUNK, CHUNK), 0)
    jj = lax.broadcasted_iota(I32, (CHUNK, CHUNK), 1)
    incl = ii >= jj
    strict = ii > jj
    nt = (((1,), (1,)), ((), ()))
    chains = [(b, h) for b in range(nb) for h in range(DN_HEADS)]
    cs = [slice(h * HEAD_DIM, (h + 1) * HEAD_DIM) for _, h in chains]
    n = range(len(chains))

    def chunk(c, carry):
        r0 = pl.multiple_of(c * CHUNK, CHUNK)
        rows = pl.ds(r0, CHUNK)
        bg = [bg_ref[b, rows, :] for b in range(nb)]
        gt = [gt_ref[b, c] for b in range(nb)]
        kh = [k_ref[b, rows, cs[i]] for i, (b, _) in enumerate(chains)]
        qh = [q_ref[b, rows, cs[i]] for i, (b, _) in enumerate(chains)]
        bcol = [bg[b][:, h:h + 1] for b, h in chains]
        gcol = [bg[b][:, DN_HEADS + h:DN_HEADS + h + 1] for b, h in chains]
        grow = [gt[b][h:h + 1, :] for b, h in chains]
        glast = [grow[i][:, CHUNK - 1:CHUNK] for i in n]
        kb = [kh[i] * bcol[i] for i in n]
        s = [lax.dot_general(jnp.concatenate([kb[i], qh[i]], 0).astype(BF16), kh[i].astype(BF16), nt,
                             preferred_element_type=F32) for i in n]
        decay = [jnp.where(incl, jnp.exp(jnp.where(incl, gcol[i] - grow[i], 0.0)), 0.0) for i in n]
        aqk = [(s[i][CHUNK:] * decay[i]).astype(BF16) for i in n]
        pw = [jnp.where(strict, -s[i][:CHUNK] * decay[i], 0.0) for i in n]
        qs = pw
        pwb = [pw[i].astype(BF16) for i in n]
        pw = [_dot(pwb[i], pwb[i]) for i in n]
        for _ in range(4):
            pwb = [pw[i].astype(BF16) for i in n]
            both = [_dot(jnp.concatenate([pwb[i], qs[i].astype(BF16)], 0), pwb[i]) for i in n]
            qs = [qs[i] + pw[i] + both[i][CHUNK:] for i in n]
            pw = [both[i][:CHUNK] for i in n]
        qp = [_dot(qs[i].astype(BF16), pw[i].astype(BF16)) for i in n]
        qs = [qs[i] + pw[i] + qp[i] for i in n]
        eg = [jnp.exp(gcol[i]) for i in n]
        rhs = [jnp.concatenate([v_ref[b, rows, cs[i]] * bcol[i], kb[i] * eg[i]], 1)
               for i, (b, _) in enumerate(chains)]
        sol = [rhs[i] + _dot(qs[i].astype(BF16), rhs[i].astype(BF16)) for i in n]
        st = [state_ref[i] for i in n]
        r = [_dot(jnp.concatenate([sol[i][:, HEAD_DIM:], qh[i] * eg[i]], 0).astype(BF16), st[i].astype(BF16))
             for i in n]
        v_new = [(sol[i][:, :HEAD_DIM] - r[i][:CHUNK]).astype(BF16) for i in n]
        kdt = [(kh[i] * jnp.exp(glast[i] - gcol[i])).T.astype(BF16) for i in n]
        ou = [_dot(jnp.concatenate([aqk[i], kdt[i]], 0), v_new[i]) for i in n]
        for i, (b, _) in enumerate(chains):
            state_ref[i] = st[i] * jnp.exp(glast[i]) + ou[i][CHUNK:]
            zz = z_ref[b, rows, cs[i]]
            oi = r[i][CHUNK:] + ou[i][:CHUNK]
            y = oi * lax.rsqrt(jnp.mean(oi * oi, -1, keepdims=True) + RMS_EPS) * nw_ref[...]
            o_ref[b, rows, cs[i]] = (y * _silu(zz)).astype(o_ref.dtype)
        return carry

    lax.fori_loop(0, nc, chunk, 0)


def _delta(qkv, zu, bg, gt, norm_w, B, S):
    T = qkv.shape[0]
    nb = DELTA_BATCH if B % DELTA_BATCH == 0 else 1
    sblk = min(1024 // nb, S)
    nc = sblk // CHUNK
    seq = lambda a: a.reshape(B, S, a.shape[-1])
    blk = lambda b, s: (b, s, 0)
    y = pl.pallas_call(
        functools.partial(_delta_body, nb, nc),
        grid=(B // nb, S // sblk),
        in_specs=[
            pl.BlockSpec((nb, sblk, DN_WIDTH), blk),
            pl.BlockSpec((nb, sblk, DN_WIDTH), lambda b, s: (b, s, 1)),
            pl.BlockSpec((nb, sblk, DN_WIDTH), lambda b, s: (b, s, 2)),
            pl.BlockSpec((nb, sblk, DN_WIDTH), blk),
            pl.BlockSpec((nb, sblk, LANES), blk),
            pl.BlockSpec((nb, nc, DN_HEADS, CHUNK), lambda b, s: (b, s, 0, 0)),
            pl.BlockSpec((1, HEAD_DIM), lambda b, s: (0, 0)),
        ],
        out_specs=pl.BlockSpec((nb, sblk, DN_WIDTH), blk),
        out_shape=jax.ShapeDtypeStruct((B, S, DN_WIDTH), BF16),
        scratch_shapes=[pltpu.VMEM((nb * DN_HEADS, HEAD_DIM, HEAD_DIM), F32)],
        compiler_params=_cparams(("parallel", "arbitrary"), VMEM_MID_MIB),
        name="delta",
    )(seq(qkv), seq(qkv), seq(qkv), seq(zu), seq(bg), gt, norm_w.astype(F32).reshape(1, HEAD_DIM))
    return y.reshape(T, DN_WIDTH)


def _pool_body(tiles_per_seq, tm, u_ref, halo_ref, w_ref, sc_ref, o_ref, us_ref):
    t_in_seq = (pl.program_id(0) % tiles_per_seq) * tm
    first = t_in_seq == 0
    us_ref[0:POOL_HALO, :] = jnp.where(first, 0.0, halo_ref[...])
    us_ref[POOL_HALO:POOL_HALO + tm, :] = u_ref[...]
    tpos = (t_in_seq + lax.broadcasted_iota(I32, (tm, 1), 0) + 1).astype(F32)
    for gi, win in enumerate(POOL_WINDOWS):
        cs = slice(gi * POOL_GROUP_DIM, (gi + 1) * POOL_GROUP_DIM)
        blk = us_ref[:, cs]
        wsum = blk
        span = 1
        while span < win:
            wsum = wsum + pltpu.roll(wsum, span, 0)
            span *= 2
        cur = blk[POOL_HALO:, :]
        d = wsum[POOL_HALO:, :] / jnp.minimum(tpos, float(win)) - cur
        y = _dot(d.astype(BF16), w_ref[gi])
        o_ref[:, cs] = (y * sc_ref[:, cs]).astype(o_ref.dtype)


def _pool(proj, pool_w, pool_scale, S):
    T = proj.shape[0]
    tm = min(256, S)
    ucol = DN_WIDTH // POOL_WIDTH
    halo_blocks = tm // POOL_HALO
    return pl.pallas_call(
        functools.partial(_pool_body, S // tm, tm),
        grid=(T // tm,),
        in_specs=[
            pl.BlockSpec((tm, POOL_WIDTH), lambda i: (i, ucol)),
            pl.BlockSpec((POOL_HALO, POOL_WIDTH), lambda i: (jnp.maximum(i * halo_blocks - 1, 0), ucol)),
            pl.BlockSpec((len(POOL_WINDOWS), POOL_GROUP_DIM, POOL_GROUP_DIM), lambda i: (0, 0, 0)),
            pl.BlockSpec((1, POOL_WIDTH), lambda i: (0, 0)),
        ],
        out_specs=pl.BlockSpec((tm, POOL_WIDTH), lambda i: (i, 0)),
        out_shape=jax.ShapeDtypeStruct((T, POOL_WIDTH), BF16),
        scratch_shapes=[pltpu.VMEM((tm + POOL_HALO, POOL_WIDTH), F32)],
        compiler_params=_cparams(("parallel",), VMEM_SMALL_MIB),
        name="pool",
    )(proj, proj, pool_w.astype(BF16), pool_scale.astype(F32).reshape(1, POOL_WIDTH))


OUTPROJ_SPLIT = 2
PACK_COLS = D_MODEL // 2
U32 = jnp.uint32


def _pack_halves(x):
    lo = lax.bitcast_convert_type(x[:, :PACK_COLS].astype(BF16).astype(F32), U32)
    hi = lax.bitcast_convert_type(x[:, PACK_COLS:].astype(BF16).astype(F32), U32)
    return (lo >> 16) | hi


def _unpack_halves(p):
    lo = lax.bitcast_convert_type(p << 16, F32)
    hi = lax.bitcast_convert_type(p & jnp.uint32(0xFFFF0000), F32)
    return lo, hi


def _route_tile(tm, logits, upper_ref, eidx_ref, gate_ref, rank_ref, cnt_ref, carry_ref):
    m = jnp.max(logits, axis=0, keepdims=True)
    e = jnp.exp(logits - m)
    p = e / jnp.sum(e, axis=0, keepdims=True)
    rows = [p[i:i + 1, :] for i in range(N_EXPERTS)]

    scores = []
    for g in range(N_GROUPS):
        a, b, c, d = rows[EXPERTS_PER_GROUP * g:EXPERTS_PER_GROUP * (g + 1)]
        hi1, lo1 = jnp.maximum(a, b), jnp.minimum(a, b)
        hi2, lo2 = jnp.maximum(c, d), jnp.minimum(c, d)
        top1 = jnp.maximum(hi1, hi2)
        top2 = jnp.maximum(jnp.minimum(hi1, hi2), jnp.where(hi1 >= hi2, lo1, lo2))
        scores.append(top1 + top2)
    gsel = jnp.zeros((1, tm), I32)
    best = scores[0]
    for g in range(1, N_GROUPS):
        better = scores[g] > best
        gsel = jnp.where(better, g, gsel)
        best = jnp.where(better, scores[g], best)
    ing = []
    for j in range(EXPERTS_PER_GROUP):
        sel = rows[(N_GROUPS - 1) * EXPERTS_PER_GROUP + j]
        for g in range(N_GROUPS - 2, -1, -1):
            sel = jnp.where(gsel == g, rows[g * EXPERTS_PER_GROUP + j], sel)
        ing.append(sel)
    i1 = jnp.zeros((1, tm), I32)
    p1 = ing[0]
    for j in range(1, EXPERTS_PER_GROUP):
        better = ing[j] > p1
        i1 = jnp.where(better, j, i1)
        p1 = jnp.where(better, ing[j], p1)
    i2 = jnp.zeros((1, tm), I32)
    p2 = jnp.full((1, tm), -1.0, F32)
    for j in range(EXPERTS_PER_GROUP):
        cand = jnp.where(i1 == j, -1.0, ing[j])
        better = cand > p2
        i2 = jnp.where(better, j, i2)
        p2 = jnp.where(better, cand, p2)
    den = p1 + p2
    e0 = gsel * EXPERTS_PER_GROUP + i1
    e1 = gsel * EXPERTS_PER_GROUP + i2
    eidx_ref[0:1, :] = e0
    eidx_ref[1:2, :] = e1
    gate_ref[0:1, :] = p1 / den
    gate_ref[1:2, :] = p2 / den

    er = lax.broadcasted_iota(I32, (N_EXPERTS, tm), 0)
    oh0 = er == e0
    oh1 = er == e1
    oh = jnp.where(oh0 | oh1, 1.0, 0.0)
    before = carry_ref[:, 0:1] + _dot(oh.astype(BF16), upper_ref[...])
    rank_ref[0:1, :] = jnp.sum(jnp.where(oh0, before, 0.0), axis=0, keepdims=True).astype(I32)
    rank_ref[1:2, :] = jnp.sum(jnp.where(oh1, before, 0.0), axis=0, keepdims=True).astype(I32)
    total = carry_ref[...] + jnp.sum(oh, axis=1, keepdims=True)
    carry_ref[...] = total
    cnt_ref[...] = total.astype(I32)


def _outproj_body(layer, tm, ydn_ref, ypool_ref, w_hbm, x_ref, g_ref, b_ref, wrh_ref, wrl_ref, br_ref, upper_ref,
                  o_ref, op_ref, eidx_ref, gate_ref, rank_ref, cnt_ref, carry_ref, w_ref, wstage_ref, wsem):
    @pl.when(pl.program_id(0) == 0)
    def _():
        carry_ref[...] = jnp.zeros(carry_ref.shape, F32)
        _load_weight_bf16(w_hbm, layer, w_ref, wstage_ref, wsem)

    subs = [slice(k * (tm // OUTPROJ_SPLIT), (k + 1) * (tm // OUTPROJ_SPLIT)) for k in range(OUTPROJ_SPLIT)]
    mix = [_dot(ydn_ref[r, :], w_ref[0:DN_WIDTH, :]) + _dot(ypool_ref[r, :], w_ref[DN_WIDTH:, :]) for r in subs]
    x1 = [_layer_norm(ALPHA * x_ref[r, :] + m, g_ref[...], b_ref[...]) for r, m in zip(subs, mix)]
    xh = [x.astype(BF16) for x in x1]
    xl = [(x - h.astype(F32)).astype(BF16) for x, h in zip(x1, xh)]
    lg = [_dot(h, wrh_ref[...]) + (_dot(l, wrh_ref[...]) + _dot(h, wrl_ref[...])) for h, l in zip(xh, xl)]
    for r, x in zip(subs, x1):
        o_ref[r, :] = x
        op_ref[r, :] = _pack_halves(x)
    logits = jnp.concatenate(lg, 0).T[0:N_EXPERTS, :] + br_ref[:, 0:1]
    _route_tile(tm, logits, upper_ref, eidx_ref, gate_ref, rank_ref, cnt_ref, carry_ref)


def _outproj(layer, y_dn, y_pool, w_out, x, g, b, w_router, b_router):
    T = x.shape[0]
    tm = min(512, T)
    r = jnp.arange(tm)
    upper = (r[:, None] < r[None, :]).astype(BF16)
    wr = jnp.pad(w_router.astype(F32), ((0, 0), (0, LANES - N_EXPERTS)))
    wr_hi = wr.astype(BF16)
    wr_lo = (wr - wr_hi.astype(F32)).astype(BF16)
    row = lambda i: (i, 0)
    tok = lambda i: (0, i)
    const = lambda i: (0, 0)
    return pl.pallas_call(
        functools.partial(_outproj_body, layer, tm),
        grid=(T // tm,),
        in_specs=[
            pl.BlockSpec((tm, DN_WIDTH), row),
            pl.BlockSpec((tm, POOL_WIDTH), row),
            pl.BlockSpec(memory_space=pltpu.HBM),
            pl.BlockSpec((tm, D_MODEL), row),
            pl.BlockSpec((1, D_MODEL), const),
            pl.BlockSpec((1, D_MODEL), const),
            pl.BlockSpec((D_MODEL, LANES), const),
            pl.BlockSpec((D_MODEL, LANES), const),
            pl.BlockSpec((N_EXPERTS, LANES), const),
            pl.BlockSpec((tm, tm), const),
        ],
        out_specs=[
            pl.BlockSpec((tm, D_MODEL), row),
            pl.BlockSpec((tm, PACK_COLS), row),
            pl.BlockSpec((2, tm), tok),
            pl.BlockSpec((2, tm), tok),
            pl.BlockSpec((2, tm), tok),
            pl.BlockSpec((N_EXPERTS, LANES), const),
        ],
        out_shape=[
            jax.ShapeDtypeStruct((T, D_MODEL), F32),
            jax.ShapeDtypeStruct((T, PACK_COLS), U32),
            jax.ShapeDtypeStruct((2, T), I32),
            jax.ShapeDtypeStruct((2, T), F32),
            jax.ShapeDtypeStruct((2, T), I32),
            jax.ShapeDtypeStruct((N_EXPERTS, LANES), I32),
        ],
        scratch_shapes=[pltpu.VMEM((N_EXPERTS, LANES), F32)] + _weight_scratch(D_MODEL, D_MODEL),
        compiler_params=_cparams(("arbitrary",), VMEM_MID_MIB),
        name="outproj",
    )(y_dn, y_pool, w_out.astype(F32), x, g.astype(F32).reshape(1, D_MODEL), b.astype(F32).reshape(1, D_MODEL),
      wr_hi, wr_lo, jnp.broadcast_to(b_router.astype(F32)[:, None], (N_EXPERTS, LANES)), upper)


DUMP_ROWS = 2 * EXPERT_BLOCK


def _dispatch_body(tm, T, n_blocks, pos_ref, pad_start_ref, pad_len_ref, nb_ref, x_ref, xs_ref, inv_ref,
                   zrow_ref, sem, zsem):
    step = pl.program_id(0)

    def zero_copy(dst):
        return pltpu.make_async_copy(zrow_ref.at[pl.ds(0, 1)], xs_ref.at[pl.ds(dst, 1)], zsem)

    def dump_row(r):
        return 2 * T + lax.rem(r, DUMP_ROWS)

    @pl.when(step == 0)
    def _():
        zrow_ref[...] = jnp.zeros(zrow_ref.shape, zrow_ref.dtype)
        for e in range(N_EXPERTS):
            def fill(r, carry, e=e):
                zero_copy(pad_start_ref[e] + r).start()
                inv_ref[pad_start_ref[e] + r] = dump_row(pad_start_ref[e] + r)
                return carry
            lax.fori_loop(0, pad_len_ref[e], fill, 0)

        def tail_inv(r, carry):
            inv_ref[r] = dump_row(r)
            return carry
        lax.fori_loop(nb_ref[0] * EXPERT_BLOCK, n_blocks * EXPERT_BLOCK, tail_inv, 0)
        for e in range(N_EXPERTS):
            def drain(r, carry):
                zero_copy(0).wait()
                return carry
            lax.fori_loop(0, pad_len_ref[e], drain, 0)

        def tail_copy(blk):
            return pltpu.make_async_copy(zrow_ref, xs_ref.at[pl.ds(blk * EXPERT_BLOCK, EXPERT_BLOCK)], zsem)

        def tail_fill(blk, carry):
            tail_copy(blk).start()
            return carry

        def tail_drain(blk, carry):
            tail_copy(blk).wait()
            return carry
        lax.fori_loop(nb_ref[0], n_blocks, tail_fill, 0)
        lax.fori_loop(nb_ref[0], n_blocks, tail_drain, 0)

    base = step * tm
    for r in range(tm):
        for k in range(2):
            dst = pos_ref[k * T + base + r]
            inv_ref[dst] = k * T + base + r
            pltpu.make_async_copy(x_ref.at[pl.ds(r, 1)], xs_ref.at[pl.ds(dst, 1)], sem).start(priority=k)
    for k in range(2):
        pltpu.make_async_copy(x_ref, xs_ref.at[pl.ds(0, tm)], sem).wait()


def _dispatch(pos_flat, pad_start, pad_len, nb, x1, n_rows):
    T = x1.shape[0]
    tm = min(512, T)
    return pl.pallas_call(
        functools.partial(_dispatch_body, tm, T, n_rows // EXPERT_BLOCK),
        grid_spec=pltpu.PrefetchScalarGridSpec(
            num_scalar_prefetch=4,
            grid=(T // tm,),
            in_specs=[pl.BlockSpec((tm, PACK_COLS), lambda i, *_: (i, 0))],
            out_specs=[pl.BlockSpec(memory_space=pltpu.HBM), pl.BlockSpec(memory_space=pltpu.SMEM)],
            scratch_shapes=[pltpu.VMEM((EXPERT_BLOCK, PACK_COLS), U32), pltpu.SemaphoreType.DMA,
                            pltpu.SemaphoreType.DMA],
        ),
        out_shape=[jax.ShapeDtypeStruct((n_rows, PACK_COLS), U32), jax.ShapeDtypeStruct((n_rows,), I32)],
        compiler_params=_cparams(("arbitrary",), VMEM_SMALL_MIB),
        name="dispatch",
    )(pos_flat, pad_start, pad_len, nb, x1)


EXPERT_SPLIT = 2
W_CHUNKS = 8
N_CHUNKS = 3 * W_CHUNKS
N_STAGE = 4
GU_ROWS = D_MODEL // W_CHUNKS
DN_ROWS = D_EXPERT // W_CHUNKS


def _experts_body(layer, be_ref, nb_ref, slot_ref, c0_ref, c1_ref, nxt_ref, first_ref, inv_ref,
                  xs_ref, wg_hbm, wu_hbm, wd_hbm, yk_ref, wgu_ref, wdn_ref, stg_a, stg_d, ybuf_ref,
                  sem_a, sem_d, osem):
    i = pl.program_id(0)

    def chunk_copy(kind, e, idx, b):
        if kind == 0:
            return pltpu.make_async_copy(wg_hbm.at[layer, e, pl.ds(idx * GU_ROWS, GU_ROWS), :], stg_a.at[b], sem_a.at[b])
        if kind == 1:
            return pltpu.make_async_copy(wu_hbm.at[layer, e, pl.ds(idx * GU_ROWS, GU_ROWS), :], stg_a.at[b], sem_a.at[b])
        return pltpu.make_async_copy(wd_hbm.at[layer, e, pl.ds(idx * DN_ROWS, DN_ROWS), :], stg_d.at[b], sem_d.at[b])

    def start_chunk(e, c):
        for kind in range(3):
            @pl.when(c // W_CHUNKS == kind)
            def _():
                chunk_copy(kind, e, c % W_CHUNKS, c % N_STAGE).start()

    def finish_chunk(e, c, slot):
        for kind in range(3):
            @pl.when(c // W_CHUNKS == kind)
            def _():
                idx = c % W_CHUNKS
                b = c % N_STAGE
                chunk_copy(kind, e, idx, b).wait()
                if kind < 2:
                    rows = pl.ds(pl.multiple_of(idx * GU_ROWS, GU_ROWS), GU_ROWS)
                    wgu_ref[slot, kind, rows, :] = stg_a[b].astype(BF16)
                else:
                    rows = pl.ds(pl.multiple_of(idx * DN_ROWS, DN_ROWS), DN_ROWS)
                    wdn_ref[slot, rows, :] = stg_d[b].astype(BF16)

    def stream(e, slot, lo, hi):
        def body(c, carry):
            finish_chunk(e, c, slot)

            @pl.when(c + N_STAGE < N_CHUNKS)
            def _():
                start_chunk(e, c + N_STAGE)
            return carry
        lax.fori_loop(lo, hi, body, 0)

    def prime(e):
        for c in range(N_STAGE):
            start_chunk(e, jnp.int32(c))

    dump = yk_ref.shape[0] - DUMP_ROWS

    def out_copy(slot, r, dst):
        return pltpu.make_async_copy(ybuf_ref.at[slot, pl.ds(r, 1)], yk_ref.at[pl.ds(dst, 1)], osem.at[slot])

    def wait_out(slot):
        pltpu.make_async_copy(ybuf_ref.at[slot], yk_ref.at[pl.ds(0, EXPERT_BLOCK)], osem.at[slot]).wait()

    @pl.when(i == 0)
    def _():
        prime(be_ref[0])
        stream(be_ref[0], 0, 0, N_CHUNKS)
        ybuf_ref[...] = jnp.zeros(ybuf_ref.shape, ybuf_ref.dtype)

        def fill(r, carry):
            out_copy(0, r, dump + r).start()
            return carry
        lax.fori_loop(0, EXPERT_BLOCK, fill, 0)

    @pl.when(i < nb_ref[0])
    def _():
        s = slot_ref[i]
        ne = nxt_ref[i]
        par = i % 2

        @pl.when((first_ref[i] == 1) & (ne >= 0))
        def _():
            prime(ne)

        stream(ne, 1 - s, c0_ref[i], c1_ref[i])
        wait_out(par)
        prev = jnp.maximum(i - 1, 0) * EXPERT_BLOCK
        for r in range(EXPERT_BLOCK):
            dst = jnp.where(i >= 1, inv_ref[prev + r], dump + EXPERT_BLOCK + r)
            out_copy(1 - par, r, dst).start(priority=r % 2)
        sub = EXPERT_BLOCK // EXPERT_SPLIT
        hid = []
        for k in range(EXPERT_SPLIT):
            lo, hi = _unpack_halves(xs_ref[k * sub:(k + 1) * sub, :])
            lo = lo.astype(BF16)
            hi = hi.astype(BF16)
            g = _dot(lo, wgu_ref[s, 0, 0:PACK_COLS, :]) + _dot(hi, wgu_ref[s, 0, PACK_COLS:, :])
            u = _dot(lo, wgu_ref[s, 1, 0:PACK_COLS, :]) + _dot(hi, wgu_ref[s, 1, PACK_COLS:, :])
            hid.append((_silu(g) * u).astype(BF16))
        y = [_dot(h, wdn_ref[s]) for h in hid]
        for k in range(EXPERT_SPLIT):
            ybuf_ref[par, k * sub:(k + 1) * sub, :] = _pack_halves(y[k])

    @pl.when(i == nb_ref[0])
    def _():
        par = i % 2
        wait_out(par)
        last = (i - 1) * EXPERT_BLOCK

        def send(r, carry):
            out_copy(1 - par, r, inv_ref[last + r]).start()
            return carry
        lax.fori_loop(0, EXPERT_BLOCK, send, 0)
        wait_out(1 - par)


def _expert_schedule(block_e, nb, n_blocks):
    idx = jnp.arange(n_blocks, dtype=I32)
    valid = idx < nb[0]
    prev = jnp.concatenate([block_e[:1] - 1, block_e[:-1]])
    first = valid & ((idx == 0) | (block_e != prev))
    run_id = jnp.cumsum(first.astype(I32)) - 1
    run_start = lax.cummax(jnp.where(first, idx, 0))
    run_len = jnp.sum((run_id[:, None] == run_id[None, :]) & valid[None, :], -1).astype(I32)
    j = idx - run_start
    nxt_idx = run_start + run_len
    nxt = jnp.where(valid & (nxt_idx < nb[0]), block_e[jnp.minimum(nxt_idx, n_blocks - 1)], -1).astype(I32)
    n = jnp.maximum(run_len, 1)
    has = nxt >= 0
    c0 = jnp.where(has, j * N_CHUNKS // n, 0).astype(I32)
    c1 = jnp.where(has, (j + 1) * N_CHUNKS // n, 0).astype(I32)
    return (run_id % 2).astype(I32), c0, c1, nxt, first.astype(I32)


def _experts(layer, block_e, nb, inv, xs, wg, wu, wd, T):
    n_rows = xs.shape[0]
    bm = EXPERT_BLOCK
    n_blocks = n_rows // bm
    slot, c0, c1, nxt, first = _expert_schedule(block_e, nb, n_blocks)
    blk = lambda i, be, nb, *_: (jnp.maximum(jnp.minimum(i, nb[0] - 1), 0), 0)
    hbm = pl.BlockSpec(memory_space=pltpu.HBM)
    return pl.pallas_call(
        functools.partial(_experts_body, layer),
        grid_spec=pltpu.PrefetchScalarGridSpec(
            num_scalar_prefetch=8,
            grid=(n_blocks,),
            in_specs=[pl.BlockSpec((bm, PACK_COLS), blk), hbm, hbm, hbm],
            out_specs=hbm,
            scratch_shapes=[
                pltpu.VMEM((2, 2, D_MODEL, D_EXPERT), BF16),
                pltpu.VMEM((2, D_EXPERT, D_MODEL), BF16),
                pltpu.VMEM((N_STAGE, GU_ROWS, D_EXPERT), F32),
                pltpu.VMEM((N_STAGE, DN_ROWS, D_MODEL), F32),
                pltpu.VMEM((2, bm, PACK_COLS), U32),
                pltpu.SemaphoreType.DMA((N_STAGE,)),
                pltpu.SemaphoreType.DMA((N_STAGE,)),
                pltpu.SemaphoreType.DMA((2,)),
            ],
        ),
        out_shape=jax.ShapeDtypeStruct((2 * T + DUMP_ROWS, PACK_COLS), U32),
        compiler_params=_cparams(("arbitrary",), VMEM_MID_MIB),
        name="experts",
    )(block_e, nb, slot, c0, c1, nxt, first, inv, xs, wg, wu, wd)


COMBINE_SPLIT = 2


def _combine_body(layer, tm, y0_ref, y1_ref, gate_ref, x1_ref, p_ref, wpg_hbm, wpp_ref, g_ref, b_ref, o_ref, ob_ref,
                  wpg_ref, wstage_ref, wsem):
    @pl.when(pl.program_id(0) == 0)
    def _():
        _load_weight_bf16(wpg_hbm, layer, wpg_ref, wstage_ref, wsem)

    sub = tm // COMBINE_SPLIT
    subs = [slice(h * sub, (h + 1) * sub) for h in range(COMBINE_SPLIT)]
    x2, gl = [], []
    for rows in subs:
        gate = gate_ref[rows, :]
        lo0, hi0 = _unpack_halves(y0_ref[rows, :])
        lo1, hi1 = _unpack_halves(y1_ref[rows, :])
        g0 = gate[:, 0:1]
        g1 = gate[:, 1:2]
        ffn = jnp.concatenate([lo0 * g0 + lo1 * g1, hi0 * g0 + hi1 * g1], 1)
        x2.append(_layer_norm(ALPHA * x1_ref[rows, :] + ffn, g_ref[...], b_ref[...]))
        gl.append(_dot(x2[-1].astype(BF16), wpg_ref[...]))
    pp = [_dot(p_ref[rows, :].astype(BF16), wpp_ref[...]) for rows in subs]
    for rows, x, a, b in zip(subs, x2, gl, pp):
        x3 = x + jax.nn.sigmoid(a) * b
        o_ref[rows, :] = x3
        ob_ref[rows, :] = x3.astype(BF16)


def _combine(layer, yk, gate_t, x1, p_all, w_ple_gate, w_ple_proj, g, b):
    T = x1.shape[0]
    tm = min(256, T)
    row = lambda i: (i, 0)
    const = lambda i: (0, 0)
    second = T // tm
    p_first = layer * (T // tm)
    return pl.pallas_call(
        functools.partial(_combine_body, layer, tm),
        grid=(T // tm,),
        in_specs=[
            pl.BlockSpec((tm, PACK_COLS), row),
            pl.BlockSpec((tm, PACK_COLS), lambda i: (second + i, 0)),
            pl.BlockSpec((tm, 2), row),
            pl.BlockSpec((tm, D_MODEL), row),
            pl.BlockSpec((tm, PLE_DIM), lambda i: (p_first + i, 0)),
            pl.BlockSpec(memory_space=pltpu.HBM),
            pl.BlockSpec((PLE_DIM, D_MODEL), const),
            pl.BlockSpec((1, D_MODEL), const),
            pl.BlockSpec((1, D_MODEL), const),
        ],
        out_specs=[
            pl.BlockSpec((tm, D_MODEL), row),
            pl.BlockSpec((tm, D_MODEL), row),
        ],
        out_shape=[
            jax.ShapeDtypeStruct((T, D_MODEL), F32),
            jax.ShapeDtypeStruct((T, D_MODEL), BF16),
        ],
        scratch_shapes=_weight_scratch(D_MODEL, D_MODEL),
        compiler_params=_cparams(("arbitrary",), VMEM_MID_MIB),
        name="combine",
    )(yk, yk, gate_t, x1, p_all, w_ple_gate.astype(F32), w_ple_proj.astype(BF16),
      g.astype(F32).reshape(1, D_MODEL), b.astype(F32).reshape(1, D_MODEL))


def _route_plan(eidx, rank, cnt, n_blocks):
    bm = EXPERT_BLOCK
    counts = cnt[:, 0]
    pcounts = (counts + bm - 1) // bm * bm
    pend = jnp.cumsum(pcounts)
    pstart = pend - pcounts
    onehot = eidx[:, :, None] == jnp.arange(N_EXPERTS, dtype=I32)
    pos = jnp.sum(jnp.where(onehot, pstart, 0), -1) + rank
    nb = (pend[-1] // bm).astype(I32).reshape(1)
    block_start = jnp.arange(n_blocks, dtype=I32) * bm
    block_e = jnp.minimum(jnp.sum(block_start[:, None] >= pend[None, :], -1), N_EXPERTS - 1).astype(I32)
    pad_start = (pstart + counts).astype(I32)
    pad_len = (pcounts - counts).astype(I32)
    return pos.reshape(-1).astype(I32), block_e, nb, pad_start, pad_len


def kernel(x, p, w_in, conv_w, a_log, dt_bias, dn_norm_w, pool_w, pool_scale, w_out, ln1_g, ln1_b,
           w_router, b_router, w_e_gate, w_e_up, w_e_down, ln2_g, ln2_b, w_ple_proj, w_ple_gate):
    B, S, D = x.shape
    T = B * S
    n_rows = 2 * T + N_EXPERTS * EXPERT_BLOCK
    xf = x.reshape(T, D).astype(F32)
    xb = xf
    for i in range(DEPTH):
        wi = w_in[i]
        w_all = jnp.concatenate([wi[:, :4 * DN_WIDTH], wi[:, 4 * DN_WIDTH + 2 * DN_HEADS:]], 1).astype(BF16)
        w_ba = jnp.pad(wi[:, 4 * DN_WIDTH:4 * DN_WIDTH + 2 * DN_HEADS],
                       ((0, 0), (0, LANES - 2 * DN_HEADS))).astype(BF16)
        qkv, zu, bg = _proj_all(xb, w_all, w_ba, conv_w[i], a_log[i], dt_bias[i], S)
        gt = bg[:, DN_HEADS:2 * DN_HEADS].reshape(B, S // CHUNK, CHUNK, DN_HEADS).transpose(0, 1, 3, 2)
        y_dn = _delta(qkv, zu, bg, gt, dn_norm_w[i], B, S)
        y_pool = _pool(zu, pool_w[i], pool_scale[i], S)
        x1, x1p, eidx, gate, rank, cnt = _outproj(i, y_dn, y_pool, w_out, xf, ln1_g[i], ln1_b[i],
                                                  w_router, b_router)
        pos_flat, block_e, nb, pad_start, pad_len = _route_plan(eidx, rank, cnt, n_rows // EXPERT_BLOCK)
        xs, inv = _dispatch(pos_flat, pad_start, pad_len, nb, x1p, n_rows)
        yk = _experts(i, block_e, nb, inv, xs, w_e_gate, w_e_up, w_e_down, T)
        xf, xb = _combine(i, yk, gate.T, x1, p.reshape(DEPTH * T, PLE_DIM), w_ple_gate, w_ple_proj[i],
                          ln2_g[i], ln2_b[i])
    return xf.reshape(B, S, D).astype(x.dtype)
```

```python
import functools

import jax
import jax.numpy as jnp
from jax import lax
from jax.experimental import pallas as pl
from jax.experimental.pallas import tpu as pltpu

F32 = jnp.float32
BF16 = jnp.bfloat16
I32 = jnp.int32
HIGHEST = lax.Precision.HIGHEST

D_MODEL = 2048
DN_HEADS = 8
HEAD_DIM = 128
DN_WIDTH = DN_HEADS * HEAD_DIM
CONV_WIDTH = 4
CHUNK = 64
POOL_WINDOWS = (2, 4, 8, 16)
POOL_GROUP_DIM = 256
POOL_WIDTH = 1024
N_EXPERTS = 16
N_GROUPS = 4
EXPERTS_PER_GROUP = 4
D_EXPERT = 1024
PLE_DIM = 256
DEPTH = 2
ALPHA = (2.0 * DEPTH) ** 0.25
LN_EPS = 1e-5
RMS_EPS = 1e-6

LANES = 128
MAIN_COLS = 4 * DN_WIDTH + POOL_WIDTH
CONV_HALO = 8
POOL_HALO = 16
EXPERT_BLOCK = 256
V7X_VMEM_MIB = 64
VMEM_MAX_MIB = V7X_VMEM_MIB - 8
VMEM_MID_MIB = V7X_VMEM_MIB - 16
VMEM_SMALL_MIB = V7X_VMEM_MIB // 2


def _cparams(sem, vmem_mib):
    return pltpu.CompilerParams(dimension_semantics=sem, vmem_limit_bytes=vmem_mib * 1024 * 1024)


def _dot(a, b):
    return jnp.dot(a, b, preferred_element_type=F32)


def _dot_hi(a, b):
    return jnp.dot(a, b, preferred_element_type=F32, precision=HIGHEST)


def _silu(x):
    h = 0.5 * x
    return h + h * jnp.tanh(h)


W_STAGE_ROWS = 256


def _load_weight_bf16(w_hbm, layer, w_ref, stage_ref, sem):
    rows = stage_ref.shape[1]
    n = w_ref.shape[0] // rows

    def chunk(c):
        return pltpu.make_async_copy(w_hbm.at[layer, pl.ds(c * rows, rows), :], stage_ref.at[c % 2], sem.at[c % 2])

    chunk(0).start()
    for c in range(n):
        if c + 1 < n:
            chunk(c + 1).start()
        chunk(c).wait()
        w_ref[c * rows:(c + 1) * rows, :] = stage_ref[c % 2].astype(BF16)


def _weight_scratch(rows, cols):
    return [pltpu.VMEM((rows, cols), BF16), pltpu.VMEM((2, W_STAGE_ROWS, cols), F32), pltpu.SemaphoreType.DMA((2,))]


def _layer_norm(h, g, b):
    mu = jnp.mean(h, -1, keepdims=True)
    d = h - mu
    var = jnp.mean(d * d, -1, keepdims=True)
    return d * lax.rsqrt(var + LN_EPS) * g + b


QKV_COLS = 3 * DN_WIDTH
ZU_COLS = DN_WIDTH + POOL_WIDTH


def _proj_all_body(tiles_per_seq, tm, x_ref, w_hbm, wba_ref, cw_ref, gp_ref, tri_ref,
                   qkv_ref, zu_ref, bg_ref, w_ref, halo_ref, raw_q, raw_k, raw_v, wsem):
    i = pl.program_id(0)

    @pl.when(i == 0)
    def _():
        cp = pltpu.make_async_copy(w_hbm, w_ref, wsem)
        cp.start()
        cp.wait()

    first = (i % tiles_per_seq) == 0
    x = x_ref[...].astype(BF16)
    raws = (raw_q, raw_k, raw_v)
    for j in range(3):
        raws[j][0:CONV_HALO, :] = jnp.where(first, 0.0, halo_ref[j])
        r = _dot(x, w_ref[:, j * DN_WIDTH:(j + 1) * DN_WIDTH])
        raws[j][CONV_HALO:, :] = r
        halo_ref[j] = r[tm - CONV_HALO:, :]
    for j in range(ZU_COLS // DN_WIDTH):
        zu_ref[:, j * DN_WIDTH:(j + 1) * DN_WIDTH] = _dot(x, w_ref[:, QKV_COLS + j * DN_WIDTH:QKV_COLS + (j + 1) * DN_WIDTH])
    ba = _dot(x, wba_ref[...])
    beta = jax.nn.sigmoid(ba)
    xx = ba + gp_ref[1:2, :]
    softplus = jnp.maximum(xx, 0.0) + jnp.log1p(jnp.exp(-jnp.abs(xx)))
    g = -jnp.exp(gp_ref[0:1, :]) * softplus
    tri = tri_ref[...]
    tb = tri.shape[0]
    gam = jnp.concatenate([_dot_hi(tri, g[r:r + tb, :]) for r in range(0, tm, tb)], 0)
    lane = lax.broadcasted_iota(I32, ba.shape, 1)
    bg_ref[...] = jnp.where(lane < DN_HEADS, beta, gam)

    for j in range(3):
        for cb in range(DN_HEADS):
            cs = slice(cb * HEAD_DIM, (cb + 1) * HEAD_DIM)
            blk = raws[j][:, cs]
            wcs = slice(j * DN_WIDTH + cb * HEAD_DIM, j * DN_WIDTH + (cb + 1) * HEAD_DIM)
            acc = blk * cw_ref[CONV_WIDTH - 1:CONV_WIDTH, wcs]
            for s in range(1, CONV_WIDTH):
                acc = acc + pltpu.roll(blk, s, 0) * cw_ref[CONV_WIDTH - 1 - s:CONV_WIDTH - s, wcs]
            y = _silu(acc[CONV_HALO:, :])
            if j < 2:
                y = y * lax.rsqrt(jnp.sum(y * y, -1, keepdims=True) + RMS_EPS)
            if j == 0:
                y = y * (HEAD_DIM ** -0.5)
            qkv_ref[:, wcs] = y


def _proj_all(x, w_all, w_ba, conv_w, a_log, dt_bias, S):
    T = x.shape[0]
    tm = min(512, S)
    tb = min(256, tm)
    pad = LANES - 2 * DN_HEADS
    gp = jnp.stack([
        jnp.pad(a_log.astype(F32), (DN_HEADS, pad)),
        jnp.pad(dt_bias.astype(F32), (DN_HEADS, pad)),
    ])
    r = jnp.arange(tb)
    tri = ((r[:, None] >= r[None, :]) & (r[:, None] // CHUNK == r[None, :] // CHUNK)).astype(F32)
    row = lambda i: (i, 0)
    const = lambda i: (0, 0)
    raw = pltpu.VMEM((tm + CONV_HALO, DN_WIDTH), F32)
    return pl.pallas_call(
        functools.partial(_proj_all_body, S // tm, tm),
        grid=(T // tm,),
        in_specs=[
            pl.BlockSpec((tm, D_MODEL), row),
            pl.BlockSpec(memory_space=pltpu.HBM),
            pl.BlockSpec((D_MODEL, LANES), const),
            pl.BlockSpec((CONV_WIDTH, QKV_COLS), const),
            pl.BlockSpec((2, LANES), const),
            pl.BlockSpec((tb, tb), const),
        ],
        out_specs=[
            pl.BlockSpec((tm, QKV_COLS), row),
            pl.BlockSpec((tm, ZU_COLS), row),
            pl.BlockSpec((tm, LANES), row),
        ],
        out_shape=[
            jax.ShapeDtypeStruct((T, QKV_COLS), F32),
            jax.ShapeDtypeStruct((T, ZU_COLS), F32),
            jax.ShapeDtypeStruct((T, LANES), F32),
        ],
        scratch_shapes=[pltpu.VMEM((D_MODEL, MAIN_COLS), BF16), pltpu.VMEM((3, CONV_HALO, DN_WIDTH), F32),
                        raw, raw, raw, pltpu.SemaphoreType.DMA],
        compiler_params=_cparams(("arbitrary",), VMEM_MAX_MIB),
        name="proj",
    )(x, w_all, w_ba, conv_w.astype(F32), gp, tri)


DELTA_BATCH = 2


def _delta_body(nb, nc, q_ref, k_ref, v_ref, z_ref, bg_ref, gt_ref, nw_ref, o_ref, state_ref):
    @pl.when(pl.program_id(1) == 0)
    def _():
        state_ref[...] = jnp.zeros(state_ref.shape, F32)

    ii = lax.broadcasted_iota(I32, (CHUNK, CHUNK), 0)
    jj = lax.broadcasted_iota(I32, (CHUNK, CHUNK), 1)
    incl = ii >= jj
    strict = ii > jj
    nt = (((1,), (1,)), ((), ()))
    chains = [(b, h) for b in range(nb) for h in range(DN_HEADS)]
    cs = [slice(h * HEAD_DIM, (h + 1) * HEAD_DIM) for _, h in chains]
    n = range(len(chains))

    def chunk(c, carry):
        r0 = pl.multiple_of(c * CHUNK, CHUNK)
        rows = pl.ds(r0, CHUNK)
        bg = [bg_ref[b, rows, :] for b in range(nb)]
        gt = [gt_ref[b, c] for b in range(nb)]
        kh = [k_ref[b, rows, cs[i]] for i, (b, _) in enumerate(chains)]
        qh = [q_ref[b, rows, cs[i]] for i, (b, _) in enumerate(chains)]
        bcol = [bg[b][:, h:h + 1] for b, h in chains]
        gcol = [bg[b][:, DN_HEADS + h:DN_HEADS + h + 1] for b, h in chains]
        grow = [gt[b][h:h + 1, :] for b, h in chains]
        glast = [grow[i][:, CHUNK - 1:CHUNK] for i in n]
        kb = [kh[i] * bcol[i] for i in n]
        s = [lax.dot_general(jnp.concatenate([kb[i], qh[i]], 0).astype(BF16), kh[i].astype(BF16), nt,
                             preferred_element_type=F32) for i in n]
        decay = [jnp.where(incl, jnp.exp(jnp.where(incl, gcol[i] - grow[i], 0.0)), 0.0) for i in n]
        aqk = [(s[i][CHUNK:] * decay[i]).astype(BF16) for i in n]
        pw = [jnp.where(strict, -s[i][:CHUNK] * decay[i], 0.0) for i in n]
        qs = pw
        pwb = [pw[i].astype(BF16) for i in n]
        pw = [_dot(pwb[i], pwb[i]) for i in n]
        for _ in range(4):
            pwb = [pw[i].astype(BF16) for i in n]
            both = [_dot(jnp.concatenate([pwb[i], qs[i].astype(BF16)], 0), pwb[i]) for i in n]
            qs = [qs[i] + pw[i] + both[i][CHUNK:] for i in n]
            pw = [both[i][:CHUNK] for i in n]
        qp = [_dot(qs[i].astype(BF16), pw[i].astype(BF16)) for i in n]
        qs = [qs[i] + pw[i] + qp[i] for i in n]
        eg = [jnp.exp(gcol[i]) for i in n]
        rhs = [jnp.concatenate([v_ref[b, rows, cs[i]] * bcol[i], kb[i] * eg[i]], 1)
               for i, (b, _) in enumerate(chains)]
        sol = [rhs[i] + _dot(qs[i].astype(BF16), rhs[i].astype(BF16)) for i in n]
        st = [state_ref[i] for i in n]
        r = [_dot(jnp.concatenate([sol[i][:, HEAD_DIM:], qh[i] * eg[i]], 0).astype(BF16), st[i].astype(BF16))
             for i in n]
        v_new = [(sol[i][:, :HEAD_DIM] - r[i][:CHUNK]).astype(BF16) for i in n]
        kdt = [(kh[i] * jnp.exp(glast[i] - gcol[i])).T.astype(BF16) for i in n]
        ou = [_dot(jnp.concatenate([aqk[i], kdt[i]], 0), v_new[i]) for i in n]
        for i, (b, _) in enumerate(chains):
            state_ref[i] = st[i] * jnp.exp(glast[i]) + ou[i][CHUNK:]
            zz = z_ref[b, rows, cs[i]]
            oi = r[i][CHUNK:] + ou[i][:CHUNK]
            y = oi * lax.rsqrt(jnp.mean(oi * oi, -1, keepdims=True) + RMS_EPS) * nw_ref[...]
            o_ref[b, rows, cs[i]] = (y * _silu(zz)).astype(o_ref.dtype)
        return carry

    lax.fori_loop(0, nc, chunk, 0)


def _delta(qkv, zu, bg, gt, norm_w, B, S):
    T = qkv.shape[0]
    nb = DELTA_BATCH if B % DELTA_BATCH == 0 else 1
    sblk = min(1024 // nb, S)
    nc = sblk // CHUNK
    seq = lambda a: a.reshape(B, S, a.shape[-1])
    blk = lambda b, s: (b, s, 0)
    y = pl.pallas_call(
        functools.partial(_delta_body, nb, nc),
        grid=(B // nb, S // sblk),
        in_specs=[
            pl.BlockSpec((nb, sblk, DN_WIDTH), blk),
            pl.BlockSpec((nb, sblk, DN_WIDTH), lambda b, s: (b, s, 1)),
            pl.BlockSpec((nb, sblk, DN_WIDTH), lambda b, s: (b, s, 2)),
            pl.BlockSpec((nb, sblk, DN_WIDTH), blk),
            pl.BlockSpec((nb, sblk, LANES), blk),
            pl.BlockSpec((nb, nc, DN_HEADS, CHUNK), lambda b, s: (b, s, 0, 0)),
            pl.BlockSpec((1, HEAD_DIM), lambda b, s: (0, 0)),
        ],
        out_specs=pl.BlockSpec((nb, sblk, DN_WIDTH), blk),
        out_shape=jax.ShapeDtypeStruct((B, S, DN_WIDTH), BF16),
        scratch_shapes=[pltpu.VMEM((nb * DN_HEADS, HEAD_DIM, HEAD_DIM), F32)],
        compiler_params=_cparams(("parallel", "arbitrary"), VMEM_MID_MIB),
        name="delta",
    )(seq(qkv), seq(qkv), seq(qkv), seq(zu), seq(bg), gt, norm_w.astype(F32).reshape(1, HEAD_DIM))
    return y.reshape(T, DN_WIDTH)


def _pool_body(tiles_per_seq, tm, u_ref, halo_ref, w_ref, sc_ref, o_ref, us_ref):
    t_in_seq = (pl.program_id(0) % tiles_per_seq) * tm
    first = t_in_seq == 0
    us_ref[0:POOL_HALO, :] = jnp.where(first, 0.0, halo_ref[...])
    us_ref[POOL_HALO:POOL_HALO + tm, :] = u_ref[...]
    tpos = (t_in_seq + lax.broadcasted_iota(I32, (tm, 1), 0) + 1).astype(F32)
    for gi, win in enumerate(POOL_WINDOWS):
        cs = slice(gi * POOL_GROUP_DIM, (gi + 1) * POOL_GROUP_DIM)
        blk = us_ref[:, cs]
        wsum = blk
        span = 1
        while span < win:
            wsum = wsum + pltpu.roll(wsum, span, 0)
            span *= 2
        cur = blk[POOL_HALO:, :]
        d = wsum[POOL_HALO:, :] / jnp.minimum(tpos, float(win)) - cur
        y = _dot(d.astype(BF16), w_ref[gi])
        o_ref[:, cs] = (y * sc_ref[:, cs]).astype(o_ref.dtype)


def _pool(proj, pool_w, pool_scale, S):
    T = proj.shape[0]
    tm = min(256, S)
    ucol = DN_WIDTH // POOL_WIDTH
    halo_blocks = tm // POOL_HALO
    return pl.pallas_call(
        functools.partial(_pool_body, S // tm, tm),
        grid=(T // tm,),
        in_specs=[
            pl.BlockSpec((tm, POOL_WIDTH), lambda i: (i, ucol)),
            pl.BlockSpec((POOL_HALO, POOL_WIDTH), lambda i: (jnp.maximum(i * halo_blocks - 1, 0), ucol)),
            pl.BlockSpec((len(POOL_WINDOWS), POOL_GROUP_DIM, POOL_GROUP_DIM), lambda i: (0, 0, 0)),
            pl.BlockSpec((1, POOL_WIDTH), lambda i: (0, 0)),
        ],
        out_specs=pl.BlockSpec((tm, POOL_WIDTH), lambda i: (i, 0)),
        out_shape=jax.ShapeDtypeStruct((T, POOL_WIDTH), BF16),
        scratch_shapes=[pltpu.VMEM((tm + POOL_HALO, POOL_WIDTH), F32)],
        compiler_params=_cparams(("parallel",), VMEM_SMALL_MIB),
        name="pool",
    )(proj, proj, pool_w.astype(BF16), pool_scale.astype(F32).reshape(1, POOL_WIDTH))


OUTPROJ_SPLIT = 2
PACK_COLS = D_MODEL // 2
U32 = jnp.uint32


def _pack_halves(x):
    lo = lax.bitcast_convert_type(x[:, :PACK_COLS].astype(BF16).astype(F32), U32)
    hi = lax.bitcast_convert_type(x[:, PACK_COLS:].astype(BF16).astype(F32), U32)
    return (lo >> 16) | hi


def _unpack_halves(p):
    lo = lax.bitcast_convert_type(p << 16, F32)
    hi = lax.bitcast_convert_type(p & jnp.uint32(0xFFFF0000), F32)
    return lo, hi


def _route_tile(tm, logits, upper_ref, eidx_ref, gate_ref, rank_ref, cnt_ref, carry_ref):
    m = jnp.max(logits, axis=0, keepdims=True)
    e = jnp.exp(logits - m)
    p = e / jnp.sum(e, axis=0, keepdims=True)
    rows = [p[i:i + 1, :] for i in range(N_EXPERTS)]

    scores = []
    for g in range(N_GROUPS):
        a, b, c, d = rows[EXPERTS_PER_GROUP * g:EXPERTS_PER_GROUP * (g + 1)]
        hi1, lo1 = jnp.maximum(a, b), jnp.minimum(a, b)
        hi2, lo2 = jnp.maximum(c, d), jnp.minimum(c, d)
        top1 = jnp.maximum(hi1, hi2)
        top2 = jnp.maximum(jnp.minimum(hi1, hi2), jnp.where(hi1 >= hi2, lo1, lo2))
        scores.append(top1 + top2)
    gsel = jnp.zeros((1, tm), I32)
    best = scores[0]
    for g in range(1, N_GROUPS):
        better = scores[g] > best
        gsel = jnp.where(better, g, gsel)
        best = jnp.where(better, scores[g], best)
    ing = []
    for j in range(EXPERTS_PER_GROUP):
        sel = rows[(N_GROUPS - 1) * EXPERTS_PER_GROUP + j]
        for g in range(N_GROUPS - 2, -1, -1):
            sel = jnp.where(gsel == g, rows[g * EXPERTS_PER_GROUP + j], sel)
        ing.append(sel)
    i1 = jnp.zeros((1, tm), I32)
    p1 = ing[0]
    for j in range(1, EXPERTS_PER_GROUP):
        better = ing[j] > p1
        i1 = jnp.where(better, j, i1)
        p1 = jnp.where(better, ing[j], p1)
    i2 = jnp.zeros((1, tm), I32)
    p2 = jnp.full((1, tm), -1.0, F32)
    for j in range(EXPERTS_PER_GROUP):
        cand = jnp.where(i1 == j, -1.0, ing[j])
        better = cand > p2
        i2 = jnp.where(better, j, i2)
        p2 = jnp.where(better, cand, p2)
    den = p1 + p2
    e0 = gsel * EXPERTS_PER_GROUP + i1
    e1 = gsel * EXPERTS_PER_GROUP + i2
    eidx_ref[0:1, :] = e0
    eidx_ref[1:2, :] = e1
    gate_ref[0:1, :] = p1 / den
    gate_ref[1:2, :] = p2 / den

    er = lax.broadcasted_iota(I32, (N_EXPERTS, tm), 0)
    oh0 = er == e0
    oh1 = er == e1
    oh = jnp.where(oh0 | oh1, 1.0, 0.0)
    before = carry_ref[:, 0:1] + _dot(oh.astype(BF16), upper_ref[...])
    rank_ref[0:1, :] = jnp.sum(jnp.where(oh0, before, 0.0), axis=0, keepdims=True).astype(I32)
    rank_ref[1:2, :] = jnp.sum(jnp.where(oh1, before, 0.0), axis=0, keepdims=True).astype(I32)
    total = carry_ref[...] + jnp.sum(oh, axis=1, keepdims=True)
    carry_ref[...] = total
    cnt_ref[...] = total.astype(I32)


def _outproj_body(layer, tm, ydn_ref, ypool_ref, w_hbm, x_ref, g_ref, b_ref, wrc_ref, br_ref, upper_ref,
                  o_ref, op_ref, eidx_ref, gate_ref, rank_ref, cnt_ref, carry_ref, w_ref, wstage_ref, wsem):
    @pl.when(pl.program_id(0) == 0)
    def _():
        carry_ref[...] = jnp.zeros(carry_ref.shape, F32)
        _load_weight_bf16(w_hbm, layer, w_ref, wstage_ref, wsem)

    subs = [slice(k * (tm // OUTPROJ_SPLIT), (k + 1) * (tm // OUTPROJ_SPLIT)) for k in range(OUTPROJ_SPLIT)]
    mix = [_dot(ydn_ref[r, :], w_ref[0:DN_WIDTH, :]) + _dot(ypool_ref[r, :], w_ref[DN_WIDTH:, :]) for r in subs]
    x1 = [_layer_norm(ALPHA * x_ref[r, :] + m, g_ref[...], b_ref[...]) for r, m in zip(subs, mix)]
    xh = [x.astype(BF16) for x in x1]
    xl = [(x - h.astype(F32)).astype(BF16) for x, h in zip(x1, xh)]
    lgh = [_dot(h, wrc_ref[...]) for h in xh]
    lgl = [_dot(l, wrc_ref[...]) for l in xl]
    for r, x in zip(subs, x1):
        o_ref[r, :] = x
        op_ref[r, :] = _pack_halves(x)
    lgh_t = jnp.concatenate(lgh, 0).T
    lgl_t = jnp.concatenate(lgl, 0).T
    logits = (lgh_t[0:N_EXPERTS, :] + (lgh_t[N_EXPERTS:2 * N_EXPERTS, :] + lgl_t[0:N_EXPERTS, :])
              + br_ref[:, 0:1])
    _route_tile(tm, logits, upper_ref, eidx_ref, gate_ref, rank_ref, cnt_ref, carry_ref)


def _outproj(layer, y_dn, y_pool, w_out, x, g, b, w_router, b_router):
    T = x.shape[0]
    tm = min(512, T)
    r = jnp.arange(tm)
    upper = (r[:, None] < r[None, :]).astype(BF16)
    wr = w_router.astype(F32)
    wr_hi = wr.astype(BF16)
    wr_lo = (wr - wr_hi.astype(F32)).astype(BF16)
    wr_cat = jnp.pad(jnp.concatenate([wr_hi, wr_lo], 1), ((0, 0), (0, LANES - 2 * N_EXPERTS)))
    row = lambda i: (i, 0)
    tok = lambda i: (0, i)
    const = lambda i: (0, 0)
    return pl.pallas_call(
        functools.partial(_outproj_body, layer, tm),
        grid=(T // tm,),
        in_specs=[
            pl.BlockSpec((tm, DN_WIDTH), row),
            pl.BlockSpec((tm, POOL_WIDTH), row),
            pl.BlockSpec(memory_space=pltpu.HBM),
            pl.BlockSpec((tm, D_MODEL), row),
            pl.BlockSpec((1, D_MODEL), const),
            pl.BlockSpec((1, D_MODEL), const),
            pl.BlockSpec((D_MODEL, LANES), const),
            pl.BlockSpec((N_EXPERTS, LANES), const),
            pl.BlockSpec((tm, tm), const),
        ],
        out_specs=[
            pl.BlockSpec((tm, D_MODEL), row),
            pl.BlockSpec((tm, PACK_COLS), row),
            pl.BlockSpec((2, tm), tok),
            pl.BlockSpec((2, tm), tok),
            pl.BlockSpec((2, tm), tok),
            pl.BlockSpec((N_EXPERTS, LANES), const),
        ],
        out_shape=[
            jax.ShapeDtypeStruct((T, D_MODEL), F32),
            jax.ShapeDtypeStruct((T, PACK_COLS), U32),
            jax.ShapeDtypeStruct((2, T), I32),
            jax.ShapeDtypeStruct((2, T), F32),
            jax.ShapeDtypeStruct((2, T), I32),
            jax.ShapeDtypeStruct((N_EXPERTS, LANES), I32),
        ],
        scratch_shapes=[pltpu.VMEM((N_EXPERTS, LANES), F32)] + _weight_scratch(D_MODEL, D_MODEL),
        compiler_params=_cparams(("arbitrary",), VMEM_MID_MIB),
        name="outproj",
    )(y_dn, y_pool, w_out.astype(F32), x, g.astype(F32).reshape(1, D_MODEL), b.astype(F32).reshape(1, D_MODEL),
      wr_cat, jnp.broadcast_to(b_router.astype(F32)[:, None], (N_EXPERTS, LANES)), upper)


DUMP_ROWS = 2 * EXPERT_BLOCK


def _dispatch_body(tm, T, n_blocks, pos_ref, pad_start_ref, pad_len_ref, nb_ref, x_ref, xs_ref, inv_ref,
                   zrow_ref, sem, zsem):
    step = pl.program_id(0)

    def zero_copy(dst):
        return pltpu.make_async_copy(zrow_ref.at[pl.ds(0, 1)], xs_ref.at[pl.ds(dst, 1)], zsem)

    def dump_row(r):
        return 2 * T + lax.rem(r, DUMP_ROWS)

    @pl.when(step == 0)
    def _():
        zrow_ref[...] = jnp.zeros(zrow_ref.shape, zrow_ref.dtype)
        for e in range(N_EXPERTS):
            def fill(r, carry, e=e):
                zero_copy(pad_start_ref[e] + r).start()
                inv_ref[pad_start_ref[e] + r] = dump_row(pad_start_ref[e] + r)
                return carry
            lax.fori_loop(0, pad_len_ref[e], fill, 0)

        def tail_inv(r, carry):
            inv_ref[r] = dump_row(r)
            return carry
        lax.fori_loop(nb_ref[0] * EXPERT_BLOCK, n_blocks * EXPERT_BLOCK, tail_inv, 0)
        for e in range(N_EXPERTS):
            def drain(r, carry):
                zero_copy(0).wait()
                return carry
            lax.fori_loop(0, pad_len_ref[e], drain, 0)

        def tail_copy(blk):
            return pltpu.make_async_copy(zrow_ref, xs_ref.at[pl.ds(blk * EXPERT_BLOCK, EXPERT_BLOCK)], zsem)

        def tail_fill(blk, carry):
            tail_copy(blk).start()
            return carry

        def tail_drain(blk, carry):
            tail_copy(blk).wait()
            return carry
        lax.fori_loop(nb_ref[0], n_blocks, tail_fill, 0)
        lax.fori_loop(nb_ref[0], n_blocks, tail_drain, 0)

    base = step * tm
    for r in range(tm):
        for k in range(2):
            dst = pos_ref[k * T + base + r]
            inv_ref[dst] = k * T + base + r
            pltpu.make_async_copy(x_ref.at[pl.ds(r, 1)], xs_ref.at[pl.ds(dst, 1)], sem).start(priority=k)
    for k in range(2):
        pltpu.make_async_copy(x_ref, xs_ref.at[pl.ds(0, tm)], sem).wait()


def _dispatch(pos_flat, pad_start, pad_len, nb, x1, n_rows):
    T = x1.shape[0]
    tm = min(512, T)
    return pl.pallas_call(
        functools.partial(_dispatch_body, tm, T, n_rows // EXPERT_BLOCK),
        grid_spec=pltpu.PrefetchScalarGridSpec(
            num_scalar_prefetch=4,
            grid=(T // tm,),
            in_specs=[pl.BlockSpec((tm, PACK_COLS), lambda i, *_: (i, 0))],
            out_specs=[pl.BlockSpec(memory_space=pltpu.HBM), pl.BlockSpec(memory_space=pltpu.SMEM)],
            scratch_shapes=[pltpu.VMEM((EXPERT_BLOCK, PACK_COLS), U32), pltpu.SemaphoreType.DMA,
                            pltpu.SemaphoreType.DMA],
        ),
        out_shape=[jax.ShapeDtypeStruct((n_rows, PACK_COLS), U32), jax.ShapeDtypeStruct((n_rows,), I32)],
        compiler_params=_cparams(("arbitrary",), VMEM_SMALL_MIB),
        name="dispatch",
    )(pos_flat, pad_start, pad_len, nb, x1)


EXPERT_SPLIT = 2
W_CHUNKS = 8
N_CHUNKS = 3 * W_CHUNKS
N_STAGE = 4
GU_ROWS = D_MODEL // W_CHUNKS
DN_ROWS = D_EXPERT // W_CHUNKS


def _experts_body(layer, be_ref, nb_ref, slot_ref, c0_ref, c1_ref, nxt_ref, first_ref, inv_ref,
                  xs_ref, wg_hbm, wu_hbm, wd_hbm, yk_ref, wgu_ref, wdn_ref, stg_a, stg_d, ybuf_ref,
                  sem_a, sem_d, osem):
    i = pl.program_id(0)

    def chunk_copy(kind, e, idx, b):
        if kind == 0:
            return pltpu.make_async_copy(wg_hbm.at[layer, e, pl.ds(idx * GU_ROWS, GU_ROWS), :], stg_a.at[b], sem_a.at[b])
        if kind == 1:
            return pltpu.make_async_copy(wu_hbm.at[layer, e, pl.ds(idx * GU_ROWS, GU_ROWS), :], stg_a.at[b], sem_a.at[b])
        return pltpu.make_async_copy(wd_hbm.at[layer, e, pl.ds(idx * DN_ROWS, DN_ROWS), :], stg_d.at[b], sem_d.at[b])

    def start_chunk(e, c):
        for kind in range(3):
            @pl.when(c // W_CHUNKS == kind)
            def _():
                chunk_copy(kind, e, c % W_CHUNKS, c % N_STAGE).start()

    def finish_chunk(e, c, slot):
        for kind in range(3):
            @pl.when(c // W_CHUNKS == kind)
            def _():
                idx = c % W_CHUNKS
                b = c % N_STAGE
                chunk_copy(kind, e, idx, b).wait()
                if kind < 2:
                    rows = pl.ds(pl.multiple_of(idx * GU_ROWS, GU_ROWS), GU_ROWS)
                    wgu_ref[slot, kind, rows, :] = stg_a[b].astype(BF16)
                else:
                    rows = pl.ds(pl.multiple_of(idx * DN_ROWS, DN_ROWS), DN_ROWS)
                    wdn_ref[slot, rows, :] = stg_d[b].astype(BF16)

    def stream(e, slot, lo, hi):
        def body(c, carry):
            finish_chunk(e, c, slot)

            @pl.when(c + N_STAGE < N_CHUNKS)
            def _():
                start_chunk(e, c + N_STAGE)
            return carry
        lax.fori_loop(lo, hi, body, 0)

    def prime(e):
        for c in range(N_STAGE):
            start_chunk(e, jnp.int32(c))

    dump = yk_ref.shape[0] - DUMP_ROWS

    def out_copy(slot, r, dst):
        return pltpu.make_async_copy(ybuf_ref.at[slot, pl.ds(r, 1)], yk_ref.at[pl.ds(dst, 1)], osem.at[slot])

    def wait_out(slot):
        pltpu.make_async_copy(ybuf_ref.at[slot], yk_ref.at[pl.ds(0, EXPERT_BLOCK)], osem.at[slot]).wait()

    @pl.when(i == 0)
    def _():
        prime(be_ref[0])
        stream(be_ref[0], 0, 0, N_CHUNKS)
        ybuf_ref[...] = jnp.zeros(ybuf_ref.shape, ybuf_ref.dtype)

        def fill(r, carry):
            out_copy(0, r, dump + r).start()
            return carry
        lax.fori_loop(0, EXPERT_BLOCK, fill, 0)

    @pl.when(i < nb_ref[0])
    def _():
        s = slot_ref[i]
        ne = nxt_ref[i]
        par = i % 2

        @pl.when((first_ref[i] == 1) & (ne >= 0))
        def _():
            prime(ne)

        stream(ne, 1 - s, c0_ref[i], c1_ref[i])
        wait_out(par)
        prev = jnp.maximum(i - 1, 0) * EXPERT_BLOCK
        for r in range(EXPERT_BLOCK):
            dst = jnp.where(i >= 1, inv_ref[prev + r], dump + EXPERT_BLOCK + r)
            out_copy(1 - par, r, dst).start(priority=r % 2)
        sub = EXPERT_BLOCK // EXPERT_SPLIT
        hid = []
        for k in range(EXPERT_SPLIT):
            lo, hi = _unpack_halves(xs_ref[k * sub:(k + 1) * sub, :])
            lo = lo.astype(BF16)
            hi = hi.astype(BF16)
            g = _dot(lo, wgu_ref[s, 0, 0:PACK_COLS, :]) + _dot(hi, wgu_ref[s, 0, PACK_COLS:, :])
            u = _dot(lo, wgu_ref[s, 1, 0:PACK_COLS, :]) + _dot(hi, wgu_ref[s, 1, PACK_COLS:, :])
            hid.append((_silu(g) * u).astype(BF16))
        y = [_dot(h, wdn_ref[s]) for h in hid]
        for k in range(EXPERT_SPLIT):
            ybuf_ref[par, k * sub:(k + 1) * sub, :] = _pack_halves(y[k])

    @pl.when(i == nb_ref[0])
    def _():
        par = i % 2
        wait_out(par)
        last = (i - 1) * EXPERT_BLOCK

        def send(r, carry):
            out_copy(1 - par, r, inv_ref[last + r]).start()
            return carry
        lax.fori_loop(0, EXPERT_BLOCK, send, 0)
        wait_out(1 - par)


def _expert_schedule(block_e, nb, n_blocks):
    idx = jnp.arange(n_blocks, dtype=I32)
    valid = idx < nb[0]
    prev = jnp.concatenate([block_e[:1] - 1, block_e[:-1]])
    first = valid & ((idx == 0) | (block_e != prev))
    run_id = jnp.cumsum(first.astype(I32)) - 1
    run_start = lax.cummax(jnp.where(first, idx, 0))
    run_len = jnp.sum((run_id[:, None] == run_id[None, :]) & valid[None, :], -1).astype(I32)
    j = idx - run_start
    nxt_idx = run_start + run_len
    nxt = jnp.where(valid & (nxt_idx < nb[0]), block_e[jnp.minimum(nxt_idx, n_blocks - 1)], -1).astype(I32)
    n = jnp.maximum(run_len, 1)
    has = nxt >= 0
    c0 = jnp.where(has, j * N_CHUNKS // n, 0).astype(I32)
    c1 = jnp.where(has, (j + 1) * N_CHUNKS // n, 0).astype(I32)
    return (run_id % 2).astype(I32), c0, c1, nxt, first.astype(I32)


def _experts(layer, block_e, nb, inv, xs, wg, wu, wd, T):
    n_rows = xs.shape[0]
    bm = EXPERT_BLOCK
    n_blocks = n_rows // bm
    slot, c0, c1, nxt, first = _expert_schedule(block_e, nb, n_blocks)
    blk = lambda i, be, nb, *_: (jnp.maximum(jnp.minimum(i, nb[0] - 1), 0), 0)
    hbm = pl.BlockSpec(memory_space=pltpu.HBM)
    return pl.pallas_call(
        functools.partial(_experts_body, layer),
        grid_spec=pltpu.PrefetchScalarGridSpec(
            num_scalar_prefetch=8,
            grid=(n_blocks,),
            in_specs=[pl.BlockSpec((bm, PACK_COLS), blk), hbm, hbm, hbm],
            out_specs=hbm,
            scratch_shapes=[
                pltpu.VMEM((2, 2, D_MODEL, D_EXPERT), BF16),
                pltpu.VMEM((2, D_EXPERT, D_MODEL), BF16),
                pltpu.VMEM((N_STAGE, GU_ROWS, D_EXPERT), F32),
                pltpu.VMEM((N_STAGE, DN_ROWS, D_MODEL), F32),
                pltpu.VMEM((2, bm, PACK_COLS), U32),
                pltpu.SemaphoreType.DMA((N_STAGE,)),
                pltpu.SemaphoreType.DMA((N_STAGE,)),
                pltpu.SemaphoreType.DMA((2,)),
            ],
        ),
        out_shape=jax.ShapeDtypeStruct((2 * T + DUMP_ROWS, PACK_COLS), U32),
        compiler_params=_cparams(("arbitrary",), VMEM_MID_MIB),
        name="experts",
    )(block_e, nb, slot, c0, c1, nxt, first, inv, xs, wg, wu, wd)


COMBINE_SPLIT = 2


def _combine_body(layer, tm, y0_ref, y1_ref, gate_ref, x1_ref, p_ref, wpg_hbm, wpp_ref, g_ref, b_ref, o_ref, ob_ref,
                  wpg_ref, wstage_ref, wsem):
    @pl.when(pl.program_id(0) == 0)
    def _():
        _load_weight_bf16(wpg_hbm, layer, wpg_ref, wstage_ref, wsem)

    sub = tm // COMBINE_SPLIT
    subs = [slice(h * sub, (h + 1) * sub) for h in range(COMBINE_SPLIT)]
    x2, gl = [], []
    for rows in subs:
        gate = gate_ref[rows, :]
        lo0, hi0 = _unpack_halves(y0_ref[rows, :])
        lo1, hi1 = _unpack_halves(y1_ref[rows, :])
        g0 = gate[:, 0:1]
        g1 = gate[:, 1:2]
        ffn = jnp.concatenate([lo0 * g0 + lo1 * g1, hi0 * g0 + hi1 * g1], 1)
        x2.append(_layer_norm(ALPHA * x1_ref[rows, :] + ffn, g_ref[...], b_ref[...]))
        gl.append(_dot(x2[-1].astype(BF16), wpg_ref[...]))
    pp = [_dot(p_ref[rows, :].astype(BF16), wpp_ref[...]) for rows in subs]
    for rows, x, a, b in zip(subs, x2, gl, pp):
        x3 = x + jax.nn.sigmoid(a) * b
        o_ref[rows, :] = x3
        ob_ref[rows, :] = x3.astype(BF16)


def _combine(layer, yk, gate_t, x1, p_all, w_ple_gate, w_ple_proj, g, b):
    T = x1.shape[0]
    tm = min(256, T)
    row = lambda i: (i, 0)
    const = lambda i: (0, 0)
    second = T // tm
    p_first = layer * (T // tm)
    return pl.pallas_call(
        functools.partial(_combine_body, layer, tm),
        grid=(T // tm,),
        in_specs=[
            pl.BlockSpec((tm, PACK_COLS), row),
            pl.BlockSpec((tm, PACK_COLS), lambda i: (second + i, 0)),
            pl.BlockSpec((tm, 2), row),
            pl.BlockSpec((tm, D_MODEL), row),
            pl.BlockSpec((tm, PLE_DIM), lambda i: (p_first + i, 0)),
            pl.BlockSpec(memory_space=pltpu.HBM),
            pl.BlockSpec((PLE_DIM, D_MODEL), const),
            pl.BlockSpec((1, D_MODEL), const),
            pl.BlockSpec((1, D_MODEL), const),
        ],
        out_specs=[
            pl.BlockSpec((tm, D_MODEL), row),
            pl.BlockSpec((tm, D_MODEL), row),
        ],
        out_shape=[
            jax.ShapeDtypeStruct((T, D_MODEL), F32),
            jax.ShapeDtypeStruct((T, D_MODEL), BF16),
        ],
        scratch_shapes=_weight_scratch(D_MODEL, D_MODEL),
        compiler_params=_cparams(("arbitrary",), VMEM_MID_MIB),
        name="combine",
    )(yk, yk, gate_t, x1, p_all, w_ple_gate.astype(F32), w_ple_proj.astype(BF16),
      g.astype(F32).reshape(1, D_MODEL), b.astype(F32).reshape(1, D_MODEL))


def _route_plan(eidx, rank, cnt, n_blocks):
    bm = EXPERT_BLOCK
    counts = cnt[:, 0]
    pcounts = (counts + bm - 1) // bm * bm
    pend = jnp.cumsum(pcounts)
    pstart = pend - pcounts
    onehot = eidx[:, :, None] == jnp.arange(N_EXPERTS, dtype=I32)
    pos = jnp.sum(jnp.where(onehot, pstart, 0), -1) + rank
    nb = (pend[-1] // bm).astype(I32).reshape(1)
    block_start = jnp.arange(n_blocks, dtype=I32) * bm
    block_e = jnp.minimum(jnp.sum(block_start[:, None] >= pend[None, :], -1), N_EXPERTS - 1).astype(I32)
    pad_start = (pstart + counts).astype(I32)
    pad_len = (pcounts - counts).astype(I32)
    return pos.reshape(-1).astype(I32), block_e, nb, pad_start, pad_len


def kernel(x, p, w_in, conv_w, a_log, dt_bias, dn_norm_w, pool_w, pool_scale, w_out, ln1_g, ln1_b,
           w_router, b_router, w_e_gate, w_e_up, w_e_down, ln2_g, ln2_b, w_ple_proj, w_ple_gate):
    B, S, D = x.shape
    T = B * S
    n_rows = 2 * T + N_EXPERTS * EXPERT_BLOCK
    xf = x.reshape(T, D).astype(F32)
    xb = xf
    for i in range(DEPTH):
        wi = w_in[i]
        w_all = jnp.concatenate([wi[:, :4 * DN_WIDTH], wi[:, 4 * DN_WIDTH + 2 * DN_HEADS:]], 1).astype(BF16)
        w_ba = jnp.pad(wi[:, 4 * DN_WIDTH:4 * DN_WIDTH + 2 * DN_HEADS],
                       ((0, 0), (0, LANES - 2 * DN_HEADS))).astype(BF16)
        qkv, zu, bg = _proj_all(xb, w_all, w_ba, conv_w[i], a_log[i], dt_bias[i], S)
        gt = bg[:, DN_HEADS:2 * DN_HEADS].reshape(B, S // CHUNK, CHUNK, DN_HEADS).transpose(0, 1, 3, 2)
        y_dn = _delta(qkv, zu, bg, gt, dn_norm_w[i], B, S)
        y_pool = _pool(zu, pool_w[i], pool_scale[i], S)
        x1, x1p, eidx, gate, rank, cnt = _outproj(i, y_dn, y_pool, w_out, xf, ln1_g[i], ln1_b[i],
                                                  w_router, b_router)
        pos_flat, block_e, nb, pad_start, pad_len = _route_plan(eidx, rank, cnt, n_rows // EXPERT_BLOCK)
        xs, inv = _dispatch(pos_flat, pad_start, pad_len, nb, x1p, n_rows)
        yk = _experts(i, block_e, nb, inv, xs, w_e_gate, w_e_up, w_e_down, T)
        xf, xb = _combine(i, yk, gate.T, x1, p.reshape(DEPTH * T, PLE_DIM), w_ple_gate, w_ple_proj[i],
                          ln2_g[i], ln2_b[i])
    return xf.reshape(B, S, D).astype(x.dtype)
```

```python
import functools

import jax
import jax.numpy as jnp
from jax import lax
from jax.experimental import pallas as pl
from jax.experimental.pallas import tpu as pltpu

F32 = jnp.float32
BF16 = jnp.bfloat16
I32 = jnp.int32
HIGHEST = lax.Precision.HIGHEST

D_MODEL = 2048
DN_HEADS = 8
HEAD_DIM = 128
DN_WIDTH = DN_HEADS * HEAD_DIM
CONV_WIDTH = 4
CHUNK = 64
POOL_WINDOWS = (2, 4, 8, 16)
POOL_GROUP_DIM = 256
POOL_WIDTH = 1024
N_EXPERTS = 16
N_GROUPS = 4
EXPERTS_PER_GROUP = 4
D_EXPERT = 1024
PLE_DIM = 256
DEPTH = 2
ALPHA = (2.0 * DEPTH) ** 0.25
LN_EPS = 1e-5
RMS_EPS = 1e-6

LANES = 128
MAIN_COLS = 4 * DN_WIDTH + POOL_WIDTH
CONV_HALO = 8
POOL_HALO = 16
EXPERT_BLOCK = 256
V7X_VMEM_MIB = 64
VMEM_MAX_MIB = V7X_VMEM_MIB - 8
VMEM_MID_MIB = V7X_VMEM_MIB - 16
VMEM_SMALL_MIB = V7X_VMEM_MIB // 2


def _cparams(sem, vmem_mib):
    return pltpu.CompilerParams(dimension_semantics=sem, vmem_limit_bytes=vmem_mib * 1024 * 1024)


def _dot(a, b):
    return jnp.dot(a, b, preferred_element_type=F32)


def _dot_hi(a, b):
    return jnp.dot(a, b, preferred_element_type=F32, precision=HIGHEST)


def _silu(x):
    h = 0.5 * x
    return h + h * jnp.tanh(h)


W_STAGE_ROWS = 256


def _load_weight_bf16(w_hbm, layer, w_ref, stage_ref, sem):
    rows = stage_ref.shape[1]
    n = w_ref.shape[0] // rows

    def chunk(c):
        return pltpu.make_async_copy(w_hbm.at[layer, pl.ds(c * rows, rows), :], stage_ref.at[c % 2], sem.at[c % 2])

    chunk(0).start()
    for c in range(n):
        if c + 1 < n:
            chunk(c + 1).start()
        chunk(c).wait()
        w_ref[c * rows:(c + 1) * rows, :] = stage_ref[c % 2].astype(BF16)


def _weight_scratch(rows, cols):
    return [pltpu.VMEM((rows, cols), BF16), pltpu.VMEM((2, W_STAGE_ROWS, cols), F32), pltpu.SemaphoreType.DMA((2,))]


def _layer_norm(h, g, b):
    mu = jnp.mean(h, -1, keepdims=True)
    d = h - mu
    var = jnp.mean(d * d, -1, keepdims=True)
    return d * lax.rsqrt(var + LN_EPS) * g + b


QKV_COLS = 3 * DN_WIDTH
ZU_COLS = DN_WIDTH + POOL_WIDTH


def _proj_all_body(tiles_per_seq, tm, x_ref, w_hbm, wba_ref, cw_ref, gp_ref, tri_ref,
                   qkv_ref, zu_ref, bg_ref, w_ref, halo_ref, raw_q, raw_k, raw_v, wsem):
    i = pl.program_id(0)

    @pl.when(i == 0)
    def _():
        cp = pltpu.make_async_copy(w_hbm, w_ref, wsem)
        cp.start()
        cp.wait()

    first = (i % tiles_per_seq) == 0
    x = x_ref[...].astype(BF16)
    raws = (raw_q, raw_k, raw_v)
    for j in range(3):
        raws[j][0:CONV_HALO, :] = jnp.where(first, 0.0, halo_ref[j])
        r = _dot(x, w_ref[:, j * DN_WIDTH:(j + 1) * DN_WIDTH])
        raws[j][CONV_HALO:, :] = r
        halo_ref[j] = r[tm - CONV_HALO:, :]
    for j in range(ZU_COLS // DN_WIDTH):
        zu_ref[:, j * DN_WIDTH:(j + 1) * DN_WIDTH] = _dot(x, w_ref[:, QKV_COLS + j * DN_WIDTH:QKV_COLS + (j + 1) * DN_WIDTH])
    ba = _dot(x, wba_ref[...])
    beta = jax.nn.sigmoid(ba)
    xx = ba + gp_ref[1:2, :]
    softplus = jnp.maximum(xx, 0.0) + jnp.log1p(jnp.exp(-jnp.abs(xx)))
    g = -jnp.exp(gp_ref[0:1, :]) * softplus
    tri = tri_ref[...]
    tb = tri.shape[0]
    gam = jnp.concatenate([_dot_hi(tri, g[r:r + tb, :]) for r in range(0, tm, tb)], 0)
    lane = lax.broadcasted_iota(I32, ba.shape, 1)
    bg_ref[...] = jnp.where(lane < DN_HEADS, beta, gam)

    for j in range(3):
        for cb in range(DN_HEADS):
            cs = slice(cb * HEAD_DIM, (cb + 1) * HEAD_DIM)
            blk = raws[j][:, cs]
            wcs = slice(j * DN_WIDTH + cb * HEAD_DIM, j * DN_WIDTH + (cb + 1) * HEAD_DIM)
            acc = blk * cw_ref[CONV_WIDTH - 1:CONV_WIDTH, wcs]
            for s in range(1, CONV_WIDTH):
                acc = acc + pltpu.roll(blk, s, 0) * cw_ref[CONV_WIDTH - 1 - s:CONV_WIDTH - s, wcs]
            y = _silu(acc[CONV_HALO:, :])
            if j < 2:
                y = y * lax.rsqrt(jnp.sum(y * y, -1, keepdims=True) + RMS_EPS)
            if j == 0:
                y = y * (HEAD_DIM ** -0.5)
            qkv_ref[:, wcs] = y


def _proj_all(x, w_all, w_ba, conv_w, a_log, dt_bias, S):
    T = x.shape[0]
    tm = min(512, S)
    tb = min(256, tm)
    pad = LANES - 2 * DN_HEADS
    gp = jnp.stack([
        jnp.pad(a_log.astype(F32), (DN_HEADS, pad)),
        jnp.pad(dt_bias.astype(F32), (DN_HEADS, pad)),
    ])
    r = jnp.arange(tb)
    tri = ((r[:, None] >= r[None, :]) & (r[:, None] // CHUNK == r[None, :] // CHUNK)).astype(F32)
    row = lambda i: (i, 0)
    const = lambda i: (0, 0)
    raw = pltpu.VMEM((tm + CONV_HALO, DN_WIDTH), F32)
    return pl.pallas_call(
        functools.partial(_proj_all_body, S // tm, tm),
        grid=(T // tm,),
        in_specs=[
            pl.BlockSpec((tm, D_MODEL), row),
            pl.BlockSpec(memory_space=pltpu.HBM),
            pl.BlockSpec((D_MODEL, LANES), const),
            pl.BlockSpec((CONV_WIDTH, QKV_COLS), const),
            pl.BlockSpec((2, LANES), const),
            pl.BlockSpec((tb, tb), const),
        ],
        out_specs=[
            pl.BlockSpec((tm, QKV_COLS), row),
            pl.BlockSpec((tm, ZU_COLS), row),
            pl.BlockSpec((tm, LANES), row),
        ],
        out_shape=[
            jax.ShapeDtypeStruct((T, QKV_COLS), F32),
            jax.ShapeDtypeStruct((T, ZU_COLS), F32),
            jax.ShapeDtypeStruct((T, LANES), F32),
        ],
        scratch_shapes=[pltpu.VMEM((D_MODEL, MAIN_COLS), BF16), pltpu.VMEM((3, CONV_HALO, DN_WIDTH), F32),
                        raw, raw, raw, pltpu.SemaphoreType.DMA],
        compiler_params=_cparams(("arbitrary",), VMEM_MAX_MIB),
        name="proj",
    )(x, w_all, w_ba, conv_w.astype(F32), gp, tri)


DELTA_BATCH = 2


def _delta_body(nb, nc, q_ref, k_ref, v_ref, z_ref, bg_ref, gt_ref, nw_ref, o_ref, state_ref):
    @pl.when(pl.program_id(1) == 0)
    def _():
        state_ref[...] = jnp.zeros(state_ref.shape, F32)

    ii = lax.broadcasted_iota(I32, (CHUNK, CHUNK), 0)
    jj = lax.broadcasted_iota(I32, (CHUNK, CHUNK), 1)
    incl = ii >= jj
    strict = ii > jj
    nt = (((1,), (1,)), ((), ()))
    chains = [(b, h) for b in range(nb) for h in range(DN_HEADS)]
    cs = [slice(h * HEAD_DIM, (h + 1) * HEAD_DIM) for _, h in chains]
    n = range(len(chains))

    def chunk(c, carry):
        r0 = pl.multiple_of(c * CHUNK, CHUNK)
        rows = pl.ds(r0, CHUNK)
        bg = [bg_ref[b, rows, :] for b in range(nb)]
        gt = [gt_ref[b, c] for b in range(nb)]
        kh = [k_ref[b, rows, cs[i]] for i, (b, _) in enumerate(chains)]
        qh = [q_ref[b, rows, cs[i]] for i, (b, _) in enumerate(chains)]
        bcol = [bg[b][:, h:h + 1] for b, h in chains]
        gcol = [bg[b][:, DN_HEADS + h:DN_HEADS + h + 1] for b, h in chains]
        grow = [gt[b][h:h + 1, :] for b, h in chains]
        glast = [grow[i][:, CHUNK - 1:CHUNK] for i in n]
        kb = [kh[i] * bcol[i] for i in n]
        s = [lax.dot_general(jnp.concatenate([kb[i], qh[i]], 0).astype(BF16), kh[i].astype(BF16), nt,
                             preferred_element_type=F32) for i in n]
        decay = [jnp.where(incl, jnp.exp(jnp.where(incl, gcol[i] - grow[i], 0.0)), 0.0) for i in n]
        aqk = [(s[i][CHUNK:] * decay[i]).astype(BF16) for i in n]
        pw = [jnp.where(strict, -s[i][:CHUNK] * decay[i], 0.0) for i in n]
        qs = pw
        pwb = [pw[i].astype(BF16) for i in n]
        pw = [_dot(pwb[i], pwb[i]) for i in n]
        for _ in range(4):
            pwb = [pw[i].astype(BF16) for i in n]
            both = [_dot(jnp.concatenate([pwb[i], qs[i].astype(BF16)], 0), pwb[i]) for i in n]
            qs = [qs[i] + pw[i] + both[i][CHUNK:] for i in n]
            pw = [both[i][:CHUNK] for i in n]
        qp = [_dot(qs[i].astype(BF16), pw[i].astype(BF16)) for i in n]
        qs = [qs[i] + pw[i] + qp[i] for i in n]
        eg = [jnp.exp(gcol[i]) for i in n]
        rhs = [jnp.concatenate([v_ref[b, rows, cs[i]] * bcol[i], kb[i] * eg[i]], 1)
               for i, (b, _) in enumerate(chains)]
        sol = [rhs[i] + _dot(qs[i].astype(BF16), rhs[i].astype(BF16)) for i in n]
        st = [state_ref[i] for i in n]
        r = [_dot(jnp.concatenate([sol[i][:, HEAD_DIM:], qh[i] * eg[i]], 0).astype(BF16), st[i].astype(BF16))
             for i in n]
        v_new = [(sol[i][:, :HEAD_DIM] - r[i][:CHUNK]).astype(BF16) for i in n]
        kdt = [(kh[i] * jnp.exp(glast[i] - gcol[i])).T.astype(BF16) for i in n]
        ou = [_dot(jnp.concatenate([aqk[i], kdt[i]], 0), v_new[i]) for i in n]
        for i, (b, _) in enumerate(chains):
            state_ref[i] = st[i] * jnp.exp(glast[i]) + ou[i][CHUNK:]
            zz = z_ref[b, rows, cs[i]]
            oi = r[i][CHUNK:] + ou[i][:CHUNK]
            y = oi * lax.rsqrt(jnp.mean(oi * oi, -1, keepdims=True) + RMS_EPS) * nw_ref[...]
            o_ref[b, rows, cs[i]] = (y * _silu(zz)).astype(o_ref.dtype)
        return carry

    lax.fori_loop(0, nc, chunk, 0)


def _delta(qkv, zu, bg, gt, norm_w, B, S):
    T = qkv.shape[0]
    nb = DELTA_BATCH if B % DELTA_BATCH == 0 else 1
    sblk = min(1024 // nb, S)
    nc = sblk // CHUNK
    seq = lambda a: a.reshape(B, S, a.shape[-1])
    blk = lambda b, s: (b, s, 0)
    y = pl.pallas_call(
        functools.partial(_delta_body, nb, nc),
        grid=(B // nb, S // sblk),
        in_specs=[
            pl.BlockSpec((nb, sblk, DN_WIDTH), blk),
            pl.BlockSpec((nb, sblk, DN_WIDTH), lambda b, s: (b, s, 1)),
            pl.BlockSpec((nb, sblk, DN_WIDTH), lambda b, s: (b, s, 2)),
            pl.BlockSpec((nb, sblk, DN_WIDTH), blk),
            pl.BlockSpec((nb, sblk, LANES), blk),
            pl.BlockSpec((nb, nc, DN_HEADS, CHUNK), lambda b, s: (b, s, 0, 0)),
            pl.BlockSpec((1, HEAD_DIM), lambda b, s: (0, 0)),
        ],
        out_specs=pl.BlockSpec((nb, sblk, DN_WIDTH), blk),
        out_shape=jax.ShapeDtypeStruct((B, S, DN_WIDTH), BF16),
        scratch_shapes=[pltpu.VMEM((nb * DN_HEADS, HEAD_DIM, HEAD_DIM), F32)],
        compiler_params=_cparams(("parallel", "arbitrary"), VMEM_MID_MIB),
        name="delta",
    )(seq(qkv), seq(qkv), seq(qkv), seq(zu), seq(bg), gt, norm_w.astype(F32).reshape(1, HEAD_DIM))
    return y.reshape(T, DN_WIDTH)


def _pool_body(tiles_per_seq, tm, u_ref, halo_ref, w_ref, sc_ref, o_ref, us_ref):
    t_in_seq = (pl.program_id(0) % tiles_per_seq) * tm
    first = t_in_seq == 0
    us_ref[0:POOL_HALO, :] = jnp.where(first, 0.0, halo_ref[...])
    us_ref[POOL_HALO:POOL_HALO + tm, :] = u_ref[...]
    tpos = (t_in_seq + lax.broadcasted_iota(I32, (tm, 1), 0) + 1).astype(F32)
    for gi, win in enumerate(POOL_WINDOWS):
        cs = slice(gi * POOL_GROUP_DIM, (gi + 1) * POOL_GROUP_DIM)
        blk = us_ref[:, cs]
        wsum = blk
        span = 1
        while span < win:
            wsum = wsum + pltpu.roll(wsum, span, 0)
            span *= 2
        cur = blk[POOL_HALO:, :]
        d = wsum[POOL_HALO:, :] / jnp.minimum(tpos, float(win)) - cur
        y = _dot(d.astype(BF16), w_ref[gi])
        o_ref[:, cs] = (y * sc_ref[:, cs]).astype(o_ref.dtype)


def _pool(proj, pool_w, pool_scale, S):
    T = proj.shape[0]
    tm = min(256, S)
    ucol = DN_WIDTH // POOL_WIDTH
    halo_blocks = tm // POOL_HALO
    return pl.pallas_call(
        functools.partial(_pool_body, S // tm, tm),
        grid=(T // tm,),
        in_specs=[
            pl.BlockSpec((tm, POOL_WIDTH), lambda i: (i, ucol)),
            pl.BlockSpec((POOL_HALO, POOL_WIDTH), lambda i: (jnp.maximum(i * halo_blocks - 1, 0), ucol)),
            pl.BlockSpec((len(POOL_WINDOWS), POOL_GROUP_DIM, POOL_GROUP_DIM), lambda i: (0, 0, 0)),
            pl.BlockSpec((1, POOL_WIDTH), lambda i: (0, 0)),
        ],
        out_specs=pl.BlockSpec((tm, POOL_WIDTH), lambda i: (i, 0)),
        out_shape=jax.ShapeDtypeStruct((T, POOL_WIDTH), BF16),
        scratch_shapes=[pltpu.VMEM((tm + POOL_HALO, POOL_WIDTH), F32)],
        compiler_params=_cparams(("parallel",), VMEM_SMALL_MIB),
        name="pool",
    )(proj, proj, pool_w.astype(BF16), pool_scale.astype(F32).reshape(1, POOL_WIDTH))


OUTPROJ_SPLIT = 2
PACK_COLS = D_MODEL // 2
U32 = jnp.uint32


def _pack_halves(x):
    lo = lax.bitcast_convert_type(x[:, :PACK_COLS].astype(BF16).astype(F32), U32)
    hi = lax.bitcast_convert_type(x[:, PACK_COLS:].astype(BF16).astype(F32), U32)
    return (lo >> 16) | hi


def _unpack_halves(p):
    lo = lax.bitcast_convert_type(p << 16, F32)
    hi = lax.bitcast_convert_type(p & jnp.uint32(0xFFFF0000), F32)
    return lo, hi


def _route_tile(tm, logits, upper_ref, eidx_ref, gate_ref, rank_ref, cnt_ref, carry_ref):
    m = jnp.max(logits, axis=0, keepdims=True)
    e = jnp.exp(logits - m)
    p = e / jnp.sum(e, axis=0, keepdims=True)
    rows = [p[i:i + 1, :] for i in range(N_EXPERTS)]

    scores = []
    for g in range(N_GROUPS):
        a, b, c, d = rows[EXPERTS_PER_GROUP * g:EXPERTS_PER_GROUP * (g + 1)]
        hi1, lo1 = jnp.maximum(a, b), jnp.minimum(a, b)
        hi2, lo2 = jnp.maximum(c, d), jnp.minimum(c, d)
        top1 = jnp.maximum(hi1, hi2)
        top2 = jnp.maximum(jnp.minimum(hi1, hi2), jnp.where(hi1 >= hi2, lo1, lo2))
        scores.append(top1 + top2)
    gsel = jnp.zeros((1, tm), I32)
    best = scores[0]
    for g in range(1, N_GROUPS):
        better = scores[g] > best
        gsel = jnp.where(better, g, gsel)
        best = jnp.where(better, scores[g], best)
    ing = []
    for j in range(EXPERTS_PER_GROUP):
        sel = rows[(N_GROUPS - 1) * EXPERTS_PER_GROUP + j]
        for g in range(N_GROUPS - 2, -1, -1):
            sel = jnp.where(gsel == g, rows[g * EXPERTS_PER_GROUP + j], sel)
        ing.append(sel)
    i1 = jnp.zeros((1, tm), I32)
    p1 = ing[0]
    for j in range(1, EXPERTS_PER_GROUP):
        better = ing[j] > p1
        i1 = jnp.where(better, j, i1)
        p1 = jnp.where(better, ing[j], p1)
    i2 = jnp.zeros((1, tm), I32)
    p2 = jnp.full((1, tm), -1.0, F32)
    for j in range(EXPERTS_PER_GROUP):
        cand = jnp.where(i1 == j, -1.0, ing[j])
        better = cand > p2
        i2 = jnp.where(better, j, i2)
        p2 = jnp.where(better, cand, p2)
    den = p1 + p2
    e0 = gsel * EXPERTS_PER_GROUP + i1
    e1 = gsel * EXPERTS_PER_GROUP + i2
    eidx_ref[0:1, :] = e0
    eidx_ref[1:2, :] = e1
    gate_ref[0:1, :] = p1 / den
    gate_ref[1:2, :] = p2 / den

    er = lax.broadcasted_iota(I32, (N_EXPERTS, tm), 0)
    oh0 = er == e0
    oh1 = er == e1
    oh = jnp.where(oh0 | oh1, 1.0, 0.0)
    before = carry_ref[:, 0:1] + _dot(oh.astype(BF16), upper_ref[...])
    rank_ref[0:1, :] = jnp.sum(jnp.where(oh0, before, 0.0), axis=0, keepdims=True).astype(I32)
    rank_ref[1:2, :] = jnp.sum(jnp.where(oh1, before, 0.0), axis=0, keepdims=True).astype(I32)
    total = carry_ref[...] + jnp.sum(oh, axis=1, keepdims=True)
    carry_ref[...] = total
    cnt_ref[...] = total.astype(I32)


def _outproj_body(layer, tm, ydn_ref, ypool_ref, w_hbm, x_ref, g_ref, b_ref, wrc_ref, br_ref, upper_ref,
                  o_ref, op_ref, eidx_ref, gate_ref, rank_ref, cnt_ref, carry_ref, w_ref, wstage_ref, wsem):
    @pl.when(pl.program_id(0) == 0)
    def _():
        carry_ref[...] = jnp.zeros(carry_ref.shape, F32)
        _load_weight_bf16(w_hbm, layer, w_ref, wstage_ref, wsem)

    subs = [slice(k * (tm // OUTPROJ_SPLIT), (k + 1) * (tm // OUTPROJ_SPLIT)) for k in range(OUTPROJ_SPLIT)]
    mix = [_dot(ydn_ref[r, :], w_ref[0:DN_WIDTH, :]) + _dot(ypool_ref[r, :], w_ref[DN_WIDTH:, :]) for r in subs]
    x1 = [_layer_norm(ALPHA * x_ref[r, :] + m, g_ref[...], b_ref[...]) for r, m in zip(subs, mix)]
    xh = [x.astype(BF16) for x in x1]
    xl = [(x - h.astype(F32)).astype(BF16) for x, h in zip(x1, xh)]
    lgh = [_dot(h, wrc_ref[...]) for h in xh]
    lgl = [_dot(l, wrc_ref[...]) for l in xl]
    for r, x in zip(subs, x1):
        o_ref[r, :] = x
        op_ref[r, :] = _pack_halves(x)
    lgh_t = jnp.concatenate(lgh, 0).T
    lgl_t = jnp.concatenate(lgl, 0).T
    logits = (lgh_t[0:N_EXPERTS, :] + (lgh_t[N_EXPERTS:2 * N_EXPERTS, :] + lgl_t[0:N_EXPERTS, :])
              + br_ref[:, 0:1])
    _route_tile(tm, logits, upper_ref, eidx_ref, gate_ref, rank_ref, cnt_ref, carry_ref)


def _outproj(layer, y_dn, y_pool, w_out, x, g, b, w_router, b_router):
    T = x.shape[0]
    tm = min(512, T)
    r = jnp.arange(tm)
    upper = (r[:, None] < r[None, :]).astype(BF16)
    wr = w_router.astype(F32)
    wr_hi = wr.astype(BF16)
    wr_lo = (wr - wr_hi.astype(F32)).astype(BF16)
    wr_cat = jnp.pad(jnp.concatenate([wr_hi, wr_lo], 1), ((0, 0), (0, LANES - 2 * N_EXPERTS)))
    row = lambda i: (i, 0)
    tok = lambda i: (0, i)
    const = lambda i: (0, 0)
    return pl.pallas_call(
        functools.partial(_outproj_body, layer, tm),
        grid=(T // tm,),
        in_specs=[
            pl.BlockSpec((tm, DN_WIDTH), row),
            pl.BlockSpec((tm, POOL_WIDTH), row),
            pl.BlockSpec(memory_space=pltpu.HBM),
            pl.BlockSpec((tm, D_MODEL), row),
            pl.BlockSpec((1, D_MODEL), const),
            pl.BlockSpec((1, D_MODEL), const),
            pl.BlockSpec((D_MODEL, LANES), const),
            pl.BlockSpec((N_EXPERTS, LANES), const),
            pl.BlockSpec((tm, tm), const),
        ],
        out_specs=[
            pl.BlockSpec((tm, D_MODEL), row),
            pl.BlockSpec((tm, PACK_COLS), row),
            pl.BlockSpec((2, tm), tok),
            pl.BlockSpec((2, tm), tok),
            pl.BlockSpec((2, tm), tok),
            pl.BlockSpec((N_EXPERTS, LANES), const),
        ],
        out_shape=[
            jax.ShapeDtypeStruct((T, D_MODEL), F32),
            jax.ShapeDtypeStruct((T, PACK_COLS), U32),
            jax.ShapeDtypeStruct((2, T), I32),
            jax.ShapeDtypeStruct((2, T), F32),
            jax.ShapeDtypeStruct((2, T), I32),
            jax.ShapeDtypeStruct((N_EXPERTS, LANES), I32),
        ],
        scratch_shapes=[pltpu.VMEM((N_EXPERTS, LANES), F32)] + _weight_scratch(D_MODEL, D_MODEL),
        compiler_params=_cparams(("arbitrary",), VMEM_MID_MIB),
        name="outproj",
    )(y_dn, y_pool, w_out.astype(F32), x, g.astype(F32).reshape(1, D_MODEL), b.astype(F32).reshape(1, D_MODEL),
      wr_cat, jnp.broadcast_to(b_router.astype(F32)[:, None], (N_EXPERTS, LANES)), upper)


DUMP_ROWS = 2 * EXPERT_BLOCK


def _dispatch_body(tm, T, n_blocks, pos_ref, pad_start_ref, pad_len_ref, nb_ref, x_ref, xs_ref, inv_ref,
                   zrow_ref, stage_ref, sem, zsem):
    step = pl.program_id(0)

    def zero_copy(dst):
        return pltpu.make_async_copy(zrow_ref.at[pl.ds(0, 1)], xs_ref.at[pl.ds(dst, 1)], zsem)

    def dump_row(r):
        return 2 * T + lax.rem(r, DUMP_ROWS)

    @pl.when(step == 0)
    def _():
        zrow_ref[...] = jnp.zeros(zrow_ref.shape, zrow_ref.dtype)
        for e in range(N_EXPERTS):
            def fill(r, carry, e=e):
                zero_copy(pad_start_ref[e] + r).start()
                inv_ref[pad_start_ref[e] + r] = dump_row(pad_start_ref[e] + r)
                return carry
            lax.fori_loop(0, pad_len_ref[e], fill, 0)

        def tail_inv(r, carry):
            inv_ref[r] = dump_row(r)
            return carry
        lax.fori_loop(nb_ref[0] * EXPERT_BLOCK, n_blocks * EXPERT_BLOCK, tail_inv, 0)
        for e in range(N_EXPERTS):
            def drain(r, carry):
                zero_copy(0).wait()
                return carry
            lax.fori_loop(0, pad_len_ref[e], drain, 0)

        def tail_copy(blk):
            return pltpu.make_async_copy(zrow_ref, xs_ref.at[pl.ds(blk * EXPERT_BLOCK, EXPERT_BLOCK)], zsem)

        def tail_fill(blk, carry):
            tail_copy(blk).start()
            return carry

        def tail_drain(blk, carry):
            tail_copy(blk).wait()
            return carry
        lax.fori_loop(nb_ref[0], n_blocks, tail_fill, 0)
        lax.fori_loop(nb_ref[0], n_blocks, tail_drain, 0)

    n_steps = T // tm
    slot = step % 2
    base = step * tm
    stage_ref[slot] = x_ref[...]
    for r in range(tm):
        for k in range(2):
            dst = pos_ref[k * T + base + r]
            inv_ref[dst] = k * T + base + r
            pltpu.make_async_copy(stage_ref.at[slot, pl.ds(r, 1)], xs_ref.at[pl.ds(dst, 1)],
                                  sem.at[slot]).start(priority=k)

    def wait_slot(s):
        for k in range(2):
            pltpu.make_async_copy(stage_ref.at[s], xs_ref.at[pl.ds(0, tm)], sem.at[s]).wait()

    @pl.when(step > 0)
    def _():
        wait_slot(1 - slot)

    @pl.when(step == n_steps - 1)
    def _():
        wait_slot(slot)


def _dispatch(pos_flat, pad_start, pad_len, nb, x1, n_rows):
    T = x1.shape[0]
    tm = min(512, T)
    return pl.pallas_call(
        functools.partial(_dispatch_body, tm, T, n_rows // EXPERT_BLOCK),
        grid_spec=pltpu.PrefetchScalarGridSpec(
            num_scalar_prefetch=4,
            grid=(T // tm,),
            in_specs=[pl.BlockSpec((tm, PACK_COLS), lambda i, *_: (i, 0))],
            out_specs=[pl.BlockSpec(memory_space=pltpu.HBM), pl.BlockSpec(memory_space=pltpu.SMEM)],
            scratch_shapes=[pltpu.VMEM((EXPERT_BLOCK, PACK_COLS), U32), pltpu.VMEM((2, tm, PACK_COLS), U32),
                            pltpu.SemaphoreType.DMA((2,)), pltpu.SemaphoreType.DMA],
        ),
        out_shape=[jax.ShapeDtypeStruct((n_rows, PACK_COLS), U32), jax.ShapeDtypeStruct((n_rows,), I32)],
        compiler_params=_cparams(("arbitrary",), VMEM_SMALL_MIB),
        name="dispatch",
    )(pos_flat, pad_start, pad_len, nb, x1)


EXPERT_SPLIT = 2
W_CHUNKS = 8
N_CHUNKS = 3 * W_CHUNKS
N_STAGE = 4
GU_ROWS = D_MODEL // W_CHUNKS
DN_ROWS = D_EXPERT // W_CHUNKS


def _experts_body(layer, be_ref, nb_ref, slot_ref, c0_ref, c1_ref, nxt_ref, first_ref, inv_ref,
                  xs_ref, wg_hbm, wu_hbm, wd_hbm, yk_ref, wgu_ref, wdn_ref, stg_a, stg_d, ybuf_ref,
                  sem_a, sem_d, osem):
    i = pl.program_id(0)

    def chunk_copy(kind, e, idx, b):
        if kind == 0:
            return pltpu.make_async_copy(wg_hbm.at[layer, e, pl.ds(idx * GU_ROWS, GU_ROWS), :], stg_a.at[b], sem_a.at[b])
        if kind == 1:
            return pltpu.make_async_copy(wu_hbm.at[layer, e, pl.ds(idx * GU_ROWS, GU_ROWS), :], stg_a.at[b], sem_a.at[b])
        return pltpu.make_async_copy(wd_hbm.at[layer, e, pl.ds(idx * DN_ROWS, DN_ROWS), :], stg_d.at[b], sem_d.at[b])

    def start_chunk(e, c):
        for kind in range(3):
            @pl.when(c // W_CHUNKS == kind)
            def _():
                chunk_copy(kind, e, c % W_CHUNKS, c % N_STAGE).start()

    def finish_chunk(e, c, slot):
        for kind in range(3):
            @pl.when(c // W_CHUNKS == kind)
            def _():
                idx = c % W_CHUNKS
                b = c % N_STAGE
                chunk_copy(kind, e, idx, b).wait()
                if kind < 2:
                    rows = pl.ds(pl.multiple_of(idx * GU_ROWS, GU_ROWS), GU_ROWS)
                    wgu_ref[slot, kind, rows, :] = stg_a[b].astype(BF16)
                else:
                    rows = pl.ds(pl.multiple_of(idx * DN_ROWS, DN_ROWS), DN_ROWS)
                    wdn_ref[slot, rows, :] = stg_d[b].astype(BF16)

    def stream(e, slot, lo, hi):
        def body(c, carry):
            finish_chunk(e, c, slot)

            @pl.when(c + N_STAGE < N_CHUNKS)
            def _():
                start_chunk(e, c + N_STAGE)
            return carry
        lax.fori_loop(lo, hi, body, 0)

    def prime(e):
        for c in range(N_STAGE):
            start_chunk(e, jnp.int32(c))

    dump = yk_ref.shape[0] - DUMP_ROWS

    def out_copy(slot, r, dst):
        return pltpu.make_async_copy(ybuf_ref.at[slot, pl.ds(r, 1)], yk_ref.at[pl.ds(dst, 1)], osem.at[slot])

    def wait_out(slot):
        pltpu.make_async_copy(ybuf_ref.at[slot], yk_ref.at[pl.ds(0, EXPERT_BLOCK)], osem.at[slot]).wait()

    @pl.when(i == 0)
    def _():
        prime(be_ref[0])
        stream(be_ref[0], 0, 0, N_CHUNKS)
        ybuf_ref[...] = jnp.zeros(ybuf_ref.shape, ybuf_ref.dtype)

        def fill(r, carry):
            out_copy(0, r, dump + r).start()
            return carry
        lax.fori_loop(0, EXPERT_BLOCK, fill, 0)

    @pl.when(i < nb_ref[0])
    def _():
        s = slot_ref[i]
        ne = nxt_ref[i]
        par = i % 2

        @pl.when((first_ref[i] == 1) & (ne >= 0))
        def _():
            prime(ne)

        stream(ne, 1 - s, c0_ref[i], c1_ref[i])
        wait_out(par)
        prev = jnp.maximum(i - 1, 0) * EXPERT_BLOCK
        for r in range(EXPERT_BLOCK):
            dst = jnp.where(i >= 1, inv_ref[prev + r], dump + EXPERT_BLOCK + r)
            out_copy(1 - par, r, dst).start(priority=r % 2)
        sub = EXPERT_BLOCK // EXPERT_SPLIT
        hid = []
        for k in range(EXPERT_SPLIT):
            lo, hi = _unpack_halves(xs_ref[k * sub:(k + 1) * sub, :])
            lo = lo.astype(BF16)
            hi = hi.astype(BF16)
            g = _dot(lo, wgu_ref[s, 0, 0:PACK_COLS, :]) + _dot(hi, wgu_ref[s, 0, PACK_COLS:, :])
            u = _dot(lo, wgu_ref[s, 1, 0:PACK_COLS, :]) + _dot(hi, wgu_ref[s, 1, PACK_COLS:, :])
            hid.append((_silu(g) * u).astype(BF16))
        y = [_dot(h, wdn_ref[s]) for h in hid]
        for k in range(EXPERT_SPLIT):
            ybuf_ref[par, k * sub:(k + 1) * sub, :] = _pack_halves(y[k])

    @pl.when(i == nb_ref[0])
    def _():
        par = i % 2
        wait_out(par)
        last = (i - 1) * EXPERT_BLOCK

        def send(r, carry):
            out_copy(1 - par, r, inv_ref[last + r]).start()
            return carry
        lax.fori_loop(0, EXPERT_BLOCK, send, 0)
        wait_out(1 - par)


def _expert_schedule(block_e, nb, n_blocks):
    idx = jnp.arange(n_blocks, dtype=I32)
    valid = idx < nb[0]
    prev = jnp.concatenate([block_e[:1] - 1, block_e[:-1]])
    first = valid & ((idx == 0) | (block_e != prev))
    run_id = jnp.cumsum(first.astype(I32)) - 1
    run_start = lax.cummax(jnp.where(first, idx, 0))
    run_len = jnp.sum((run_id[:, None] == run_id[None, :]) & valid[None, :], -1).astype(I32)
    j = idx - run_start
    nxt_idx = run_start + run_len
    nxt = jnp.where(valid & (nxt_idx < nb[0]), block_e[jnp.minimum(nxt_idx, n_blocks - 1)], -1).astype(I32)
    n = jnp.maximum(run_len, 1)
    has = nxt >= 0
    c0 = jnp.where(has, j * N_CHUNKS // n, 0).astype(I32)
    c1 = jnp.where(has, (j + 1) * N_CHUNKS // n, 0).astype(I32)
    return (run_id % 2).astype(I32), c0, c1, nxt, first.astype(I32)


def _experts(layer, block_e, nb, inv, xs, wg, wu, wd, T):
    n_rows = xs.shape[0]
    bm = EXPERT_BLOCK
    n_blocks = n_rows // bm
    slot, c0, c1, nxt, first = _expert_schedule(block_e, nb, n_blocks)
    blk = lambda i, be, nb, *_: (jnp.maximum(jnp.minimum(i, nb[0] - 1), 0), 0)
    hbm = pl.BlockSpec(memory_space=pltpu.HBM)
    return pl.pallas_call(
        functools.partial(_experts_body, layer),
        grid_spec=pltpu.PrefetchScalarGridSpec(
            num_scalar_prefetch=8,
            grid=(n_blocks,),
            in_specs=[pl.BlockSpec((bm, PACK_COLS), blk), hbm, hbm, hbm],
            out_specs=hbm,
            scratch_shapes=[
                pltpu.VMEM((2, 2, D_MODEL, D_EXPERT), BF16),
                pltpu.VMEM((2, D_EXPERT, D_MODEL), BF16),
                pltpu.VMEM((N_STAGE, GU_ROWS, D_EXPERT), F32),
                pltpu.VMEM((N_STAGE, DN_ROWS, D_MODEL), F32),
                pltpu.VMEM((2, bm, PACK_COLS), U32),
                pltpu.SemaphoreType.DMA((N_STAGE,)),
                pltpu.SemaphoreType.DMA((N_STAGE,)),
                pltpu.SemaphoreType.DMA((2,)),
            ],
        ),
        out_shape=jax.ShapeDtypeStruct((2 * T + DUMP_ROWS, PACK_COLS), U32),
        compiler_params=_cparams(("arbitrary",), VMEM_MID_MIB),
        name="experts",
    )(block_e, nb, slot, c0, c1, nxt, first, inv, xs, wg, wu, wd)


COMBINE_SPLIT = 2


def _combine_body(layer, tm, y0_ref, y1_ref, gate_ref, x1_ref, p_ref, wpg_hbm, wpp_ref, g_ref, b_ref, o_ref, ob_ref,
                  wpg_ref, wstage_ref, wsem):
    @pl.when(pl.program_id(0) == 0)
    def _():
        _load_weight_bf16(wpg_hbm, layer, wpg_ref, wstage_ref, wsem)

    sub = tm // COMBINE_SPLIT
    subs = [slice(h * sub, (h + 1) * sub) for h in range(COMBINE_SPLIT)]
    x2, gl = [], []
    for rows in subs:
        gate = gate_ref[rows, :]
        lo0, hi0 = _unpack_halves(y0_ref[rows, :])
        lo1, hi1 = _unpack_halves(y1_ref[rows, :])
        g0 = gate[:, 0:1]
        g1 = gate[:, 1:2]
        ffn = jnp.concatenate([lo0 * g0 + lo1 * g1, hi0 * g0 + hi1 * g1], 1)
        x2.append(_layer_norm(ALPHA * x1_ref[rows, :] + ffn, g_ref[...], b_ref[...]))
        gl.append(_dot(x2[-1].astype(BF16), wpg_ref[...]))
    pp = [_dot(p_ref[rows, :].astype(BF16), wpp_ref[...]) for rows in subs]
    for rows, x, a, b in zip(subs, x2, gl, pp):
        x3 = x + jax.nn.sigmoid(a) * b
        o_ref[rows, :] = x3
        ob_ref[rows, :] = x3.astype(BF16)


def _combine(layer, yk, gate_t, x1, p_all, w_ple_gate, w_ple_proj, g, b):
    T = x1.shape[0]
    tm = min(256, T)
    row = lambda i: (i, 0)
    const = lambda i: (0, 0)
    second = T // tm
    p_first = layer * (T // tm)
    return pl.pallas_call(
        functools.partial(_combine_body, layer, tm),
        grid=(T // tm,),
        in_specs=[
            pl.BlockSpec((tm, PACK_COLS), row),
            pl.BlockSpec((tm, PACK_COLS), lambda i: (second + i, 0)),
            pl.BlockSpec((tm, 2), row),
            pl.BlockSpec((tm, D_MODEL), row),
            pl.BlockSpec((tm, PLE_DIM), lambda i: (p_first + i, 0)),
            pl.BlockSpec(memory_space=pltpu.HBM),
            pl.BlockSpec((PLE_DIM, D_MODEL), const),
            pl.BlockSpec((1, D_MODEL), const),
            pl.BlockSpec((1, D_MODEL), const),
        ],
        out_specs=[
            pl.BlockSpec((tm, D_MODEL), row),
            pl.BlockSpec((tm, D_MODEL), row),
        ],
        out_shape=[
            jax.ShapeDtypeStruct((T, D_MODEL), F32),
            jax.ShapeDtypeStruct((T, D_MODEL), BF16),
        ],
        scratch_shapes=_weight_scratch(D_MODEL, D_MODEL),
        compiler_params=_cparams(("arbitrary",), VMEM_MID_MIB),
        name="combine",
    )(yk, yk, gate_t, x1, p_all, w_ple_gate.astype(F32), w_ple_proj.astype(BF16),
      g.astype(F32).reshape(1, D_MODEL), b.astype(F32).reshape(1, D_MODEL))


def _route_plan(eidx, rank, cnt, n_blocks):
    bm = EXPERT_BLOCK
    counts = cnt[:, 0]
    pcounts = (counts + bm - 1) // bm * bm
    pend = jnp.cumsum(pcounts)
    pstart = pend - pcounts
    onehot = eidx[:, :, None] == jnp.arange(N_EXPERTS, dtype=I32)
    pos = jnp.sum(jnp.where(onehot, pstart, 0), -1) + rank
    nb = (pend[-1] // bm).astype(I32).reshape(1)
    block_start = jnp.arange(n_blocks, dtype=I32) * bm
    block_e = jnp.minimum(jnp.sum(block_start[:, None] >= pend[None, :], -1), N_EXPERTS - 1).astype(I32)
    pad_start = (pstart + counts).astype(I32)
    pad_len = (pcounts - counts).astype(I32)
    return pos.reshape(-1).astype(I32), block_e, nb, pad_start, pad_len


def kernel(x, p, w_in, conv_w, a_log, dt_bias, dn_norm_w, pool_w, pool_scale, w_out, ln1_g, ln1_b,
           w_router, b_router, w_e_gate, w_e_up, w_e_down, ln2_g, ln2_b, w_ple_proj, w_ple_gate):
    B, S, D = x.shape
    T = B * S
    n_rows = 2 * T + N_EXPERTS * EXPERT_BLOCK
    xf = x.reshape(T, D).astype(F32)
    xb = xf
    for i in range(DEPTH):
        wi = w_in[i]
        w_all = jnp.concatenate([wi[:, :4 * DN_WIDTH], wi[:, 4 * DN_WIDTH + 2 * DN_HEADS:]], 1).astype(BF16)
        w_ba = jnp.pad(wi[:, 4 * DN_WIDTH:4 * DN_WIDTH + 2 * DN_HEADS],
                       ((0, 0), (0, LANES - 2 * DN_HEADS))).astype(BF16)
        qkv, zu, bg = _proj_all(xb, w_all, w_ba, conv_w[i], a_log[i], dt_bias[i], S)
        gt = bg[:, DN_HEADS:2 * DN_HEADS].reshape(B, S // CHUNK, CHUNK, DN_HEADS).transpose(0, 1, 3, 2)
        y_dn = _delta(qkv, zu, bg, gt, dn_norm_w[i], B, S)
        y_pool = _pool(zu, pool_w[i], pool_scale[i], S)
        x1, x1p, eidx, gate, rank, cnt = _outproj(i, y_dn, y_pool, w_out, xf, ln1_g[i], ln1_b[i],
                                                  w_router, b_router)
        pos_flat, block_e, nb, pad_start, pad_len = _route_plan(eidx, rank, cnt, n_rows // EXPERT_BLOCK)
        xs, inv = _dispatch(pos_flat, pad_start, pad_len, nb, x1p, n_rows)
        yk = _experts(i, block_e, nb, inv, xs, w_e_gate, w_e_up, w_e_down, T)
        xf, xb = _combine(i, yk, gate.T, x1, p.reshape(DEPTH * T, PLE_DIM), w_ple_gate, w_ple_proj[i],
                          ln2_g[i], ln2_b[i])
    return xf.reshape(B, S, D).astype(x.dtype)
```

```python
import functools

import jax
import jax.numpy as jnp
from jax import lax
from jax.experimental import pallas as pl
from jax.experimental.pallas import tpu as pltpu

F32 = jnp.float32
BF16 = jnp.bfloat16
I32 = jnp.int32
HIGHEST = lax.Precision.HIGHEST

D_MODEL = 2048
DN_HEADS = 8
HEAD_DIM = 128
DN_WIDTH = DN_HEADS * HEAD_DIM
CONV_WIDTH = 4
CHUNK = 64
POOL_WINDOWS = (2, 4, 8, 16)
POOL_GROUP_DIM = 256
POOL_WIDTH = 1024
N_EXPERTS = 16
N_GROUPS = 4
EXPERTS_PER_GROUP = 4
D_EXPERT = 1024
PLE_DIM = 256
DEPTH = 2
ALPHA = (2.0 * DEPTH) ** 0.25
LN_EPS = 1e-5
RMS_EPS = 1e-6

LANES = 128
MAIN_COLS = 4 * DN_WIDTH + POOL_WIDTH
CONV_HALO = 8
POOL_HALO = 16
EXPERT_BLOCK = 256
V7X_VMEM_MIB = 64
VMEM_MAX_MIB = V7X_VMEM_MIB - 8
VMEM_MID_MIB = V7X_VMEM_MIB - 16
VMEM_SMALL_MIB = V7X_VMEM_MIB // 2


def _cparams(sem, vmem_mib):
    return pltpu.CompilerParams(dimension_semantics=sem, vmem_limit_bytes=vmem_mib * 1024 * 1024)


def _dot(a, b):
    return jnp.dot(a, b, preferred_element_type=F32)


def _dot_hi(a, b):
    return jnp.dot(a, b, preferred_element_type=F32, precision=HIGHEST)


def _silu(x):
    h = 0.5 * x
    return h + h * jnp.tanh(h)


W_STAGE_ROWS = 256


def _load_weight_bf16(w_hbm, layer, w_ref, stage_ref, sem):
    rows = stage_ref.shape[1]
    n = w_ref.shape[0] // rows

    def chunk(c):
        return pltpu.make_async_copy(w_hbm.at[layer, pl.ds(c * rows, rows), :], stage_ref.at[c % 2], sem.at[c % 2])

    chunk(0).start()
    for c in range(n):
        if c + 1 < n:
            chunk(c + 1).start()
        chunk(c).wait()
        w_ref[c * rows:(c + 1) * rows, :] = stage_ref[c % 2].astype(BF16)


def _weight_scratch(rows, cols):
    return [pltpu.VMEM((rows, cols), BF16), pltpu.VMEM((2, W_STAGE_ROWS, cols), F32), pltpu.SemaphoreType.DMA((2,))]


def _layer_norm(h, g, b):
    mu = jnp.mean(h, -1, keepdims=True)
    d = h - mu
    var = jnp.mean(d * d, -1, keepdims=True)
    return d * lax.rsqrt(var + LN_EPS) * g + b


QKV_COLS = 3 * DN_WIDTH
ZU_COLS = DN_WIDTH + POOL_WIDTH


def _proj_all_body(tiles_per_seq, tm, x_ref, w_hbm, wba_ref, cw_ref, gp_ref, tri_ref,
                   qkv_ref, zu_ref, bg_ref, w_ref, halo_ref, raw_q, raw_k, raw_v, wsem):
    i = pl.program_id(0)

    @pl.when(i == 0)
    def _():
        cp = pltpu.make_async_copy(w_hbm, w_ref, wsem)
        cp.start()
        cp.wait()

    first = (i % tiles_per_seq) == 0
    x = x_ref[...].astype(BF16)
    raws = (raw_q, raw_k, raw_v)
    for j in range(3):
        raws[j][0:CONV_HALO, :] = jnp.where(first, 0.0, halo_ref[j])
        r = _dot(x, w_ref[:, j * DN_WIDTH:(j + 1) * DN_WIDTH])
        raws[j][CONV_HALO:, :] = r
        halo_ref[j] = r[tm - CONV_HALO:, :]
    for j in range(ZU_COLS // DN_WIDTH):
        zu_ref[:, j * DN_WIDTH:(j + 1) * DN_WIDTH] = _dot(x, w_ref[:, QKV_COLS + j * DN_WIDTH:QKV_COLS + (j + 1) * DN_WIDTH])
    ba = _dot(x, wba_ref[...])
    beta = jax.nn.sigmoid(ba)
    xx = ba + gp_ref[1:2, :]
    softplus = jnp.maximum(xx, 0.0) + jnp.log1p(jnp.exp(-jnp.abs(xx)))
    g = -jnp.exp(gp_ref[0:1, :]) * softplus
    tri = tri_ref[...]
    tb = tri.shape[0]
    gam = jnp.concatenate([_dot_hi(tri, g[r:r + tb, :]) for r in range(0, tm, tb)], 0)
    lane = lax.broadcasted_iota(I32, ba.shape, 1)
    bg_ref[...] = jnp.where(lane < DN_HEADS, beta, gam)

    for j in range(3):
        for cb in range(DN_HEADS):
            cs = slice(cb * HEAD_DIM, (cb + 1) * HEAD_DIM)
            blk = raws[j][:, cs]
            wcs = slice(j * DN_WIDTH + cb * HEAD_DIM, j * DN_WIDTH + (cb + 1) * HEAD_DIM)
            acc = blk * cw_ref[CONV_WIDTH - 1:CONV_WIDTH, wcs]
            for s in range(1, CONV_WIDTH):
                acc = acc + pltpu.roll(blk, s, 0) * cw_ref[CONV_WIDTH - 1 - s:CONV_WIDTH - s, wcs]
            y = _silu(acc[CONV_HALO:, :])
            if j < 2:
                y = y * lax.rsqrt(jnp.sum(y * y, -1, keepdims=True) + RMS_EPS)
            if j == 0:
                y = y * (HEAD_DIM ** -0.5)
            qkv_ref[:, wcs] = y


def _proj_all(x, w_all, w_ba, conv_w, a_log, dt_bias, S):
    T = x.shape[0]
    tm = min(512, S)
    tb = min(256, tm)
    pad = LANES - 2 * DN_HEADS
    gp = jnp.stack([
        jnp.pad(a_log.astype(F32), (DN_HEADS, pad)),
        jnp.pad(dt_bias.astype(F32), (DN_HEADS, pad)),
    ])
    r = jnp.arange(tb)
    tri = ((r[:, None] >= r[None, :]) & (r[:, None] // CHUNK == r[None, :] // CHUNK)).astype(F32)
    row = lambda i: (i, 0)
    const = lambda i: (0, 0)
    raw = pltpu.VMEM((tm + CONV_HALO, DN_WIDTH), F32)
    return pl.pallas_call(
        functools.partial(_proj_all_body, S // tm, tm),
        grid=(T // tm,),
        in_specs=[
            pl.BlockSpec((tm, D_MODEL), row),
            pl.BlockSpec(memory_space=pltpu.HBM),
            pl.BlockSpec((D_MODEL, LANES), const),
            pl.BlockSpec((CONV_WIDTH, QKV_COLS), const),
            pl.BlockSpec((2, LANES), const),
            pl.BlockSpec((tb, tb), const),
        ],
        out_specs=[
            pl.BlockSpec((tm, QKV_COLS), row),
            pl.BlockSpec((tm, ZU_COLS), row),
            pl.BlockSpec((tm, LANES), row),
        ],
        out_shape=[
            jax.ShapeDtypeStruct((T, QKV_COLS), F32),
            jax.ShapeDtypeStruct((T, ZU_COLS), F32),
            jax.ShapeDtypeStruct((T, LANES), F32),
        ],
        scratch_shapes=[pltpu.VMEM((D_MODEL, MAIN_COLS), BF16), pltpu.VMEM((3, CONV_HALO, DN_WIDTH), F32),
                        raw, raw, raw, pltpu.SemaphoreType.DMA],
        compiler_params=_cparams(("arbitrary",), VMEM_MAX_MIB),
        name="proj",
    )(x, w_all, w_ba, conv_w.astype(F32), gp, tri)


DELTA_BATCH = 2


def _delta_body(nb, nc, q_ref, k_ref, v_ref, z_ref, bg_ref, gt_ref, nw_ref, o_ref, state_ref):
    @pl.when(pl.program_id(1) == 0)
    def _():
        state_ref[...] = jnp.zeros(state_ref.shape, F32)

    ii = lax.broadcasted_iota(I32, (CHUNK, CHUNK), 0)
    jj = lax.broadcasted_iota(I32, (CHUNK, CHUNK), 1)
    incl = ii >= jj
    strict = ii > jj
    nt = (((1,), (1,)), ((), ()))
    chains = [(b, h) for b in range(nb) for h in range(DN_HEADS)]
    cs = [slice(h * HEAD_DIM, (h + 1) * HEAD_DIM) for _, h in chains]
    n = range(len(chains))

    def chunk(c, carry):
        r0 = pl.multiple_of(c * CHUNK, CHUNK)
        rows = pl.ds(r0, CHUNK)
        bg = [bg_ref[b, rows, :] for b in range(nb)]
        gt = [gt_ref[b, c] for b in range(nb)]
        kh = [k_ref[b, rows, cs[i]] for i, (b, _) in enumerate(chains)]
        qh = [q_ref[b, rows, cs[i]] for i, (b, _) in enumerate(chains)]
        bcol = [bg[b][:, h:h + 1] for b, h in chains]
        gcol = [bg[b][:, DN_HEADS + h:DN_HEADS + h + 1] for b, h in chains]
        grow = [gt[b][h:h + 1, :] for b, h in chains]
        glast = [grow[i][:, CHUNK - 1:CHUNK] for i in n]
        kb = [kh[i] * bcol[i] for i in n]
        s = [lax.dot_general(jnp.concatenate([kb[i], qh[i]], 0).astype(BF16), kh[i].astype(BF16), nt,
                             preferred_element_type=F32) for i in n]
        decay = [jnp.where(incl, jnp.exp(jnp.where(incl, gcol[i] - grow[i], 0.0)), 0.0) for i in n]
        aqk = [(s[i][CHUNK:] * decay[i]).astype(BF16) for i in n]
        pw = [jnp.where(strict, -s[i][:CHUNK] * decay[i], 0.0) for i in n]
        qs = pw
        pwb = [pw[i].astype(BF16) for i in n]
        pw = [_dot(pwb[i], pwb[i]) for i in n]
        for _ in range(4):
            pwb = [pw[i].astype(BF16) for i in n]
            both = [_dot(jnp.concatenate([pwb[i], qs[i].astype(BF16)], 0), pwb[i]) for i in n]
            qs = [qs[i] + pw[i] + both[i][CHUNK:] for i in n]
            pw = [both[i][:CHUNK] for i in n]
        qp = [_dot(qs[i].astype(BF16), pw[i].astype(BF16)) for i in n]
        qs = [qs[i] + pw[i] + qp[i] for i in n]
        eg = [jnp.exp(gcol[i]) for i in n]
        rhs = [jnp.concatenate([v_ref[b, rows, cs[i]] * bcol[i], kb[i] * eg[i]], 1)
               for i, (b, _) in enumerate(chains)]
        sol = [rhs[i] + _dot(qs[i].astype(BF16), rhs[i].astype(BF16)) for i in n]
        st = [state_ref[i] for i in n]
        r = [_dot(jnp.concatenate([sol[i][:, HEAD_DIM:], qh[i] * eg[i]], 0).astype(BF16), st[i].astype(BF16))
             for i in n]
        v_new = [(sol[i][:, :HEAD_DIM] - r[i][:CHUNK]).astype(BF16) for i in n]
        kdt = [(kh[i] * jnp.exp(glast[i] - gcol[i])).T.astype(BF16) for i in n]
        ou = [_dot(jnp.concatenate([aqk[i], kdt[i]], 0), v_new[i]) for i in n]
        for i, (b, _) in enumerate(chains):
            state_ref[i] = st[i] * jnp.exp(glast[i]) + ou[i][CHUNK:]
            zz = z_ref[b, rows, cs[i]]
            oi = r[i][CHUNK:] + ou[i][:CHUNK]
            y = oi * lax.rsqrt(jnp.mean(oi * oi, -1, keepdims=True) + RMS_EPS) * nw_ref[...]
            o_ref[b, rows, cs[i]] = (y * _silu(zz)).astype(o_ref.dtype)
        return carry

    lax.fori_loop(0, nc, chunk, 0)


def _delta(qkv, zu, bg, gt, norm_w, B, S):
    T = qkv.shape[0]
    nb = DELTA_BATCH if B % DELTA_BATCH == 0 else 1
    sblk = min(1024 // nb, S)
    nc = sblk // CHUNK
    seq = lambda a: a.reshape(B, S, a.shape[-1])
    blk = lambda b, s: (b, s, 0)
    y = pl.pallas_call(
        functools.partial(_delta_body, nb, nc),
        grid=(B // nb, S // sblk),
        in_specs=[
            pl.BlockSpec((nb, sblk, DN_WIDTH), blk),
            pl.BlockSpec((nb, sblk, DN_WIDTH), lambda b, s: (b, s, 1)),
            pl.BlockSpec((nb, sblk, DN_WIDTH), lambda b, s: (b, s, 2)),
            pl.BlockSpec((nb, sblk, DN_WIDTH), blk),
            pl.BlockSpec((nb, sblk, LANES), blk),
            pl.BlockSpec((nb, nc, DN_HEADS, CHUNK), lambda b, s: (b, s, 0, 0)),
            pl.BlockSpec((1, HEAD_DIM), lambda b, s: (0, 0)),
        ],
        out_specs=pl.BlockSpec((nb, sblk, DN_WIDTH), blk),
        out_shape=jax.ShapeDtypeStruct((B, S, DN_WIDTH), BF16),
        scratch_shapes=[pltpu.VMEM((nb * DN_HEADS, HEAD_DIM, HEAD_DIM), F32)],
        compiler_params=_cparams(("parallel", "arbitrary"), VMEM_MID_MIB),
        name="delta",
    )(seq(qkv), seq(qkv), seq(qkv), seq(zu), seq(bg), gt, norm_w.astype(F32).reshape(1, HEAD_DIM))
    return y.reshape(T, DN_WIDTH)


def _pool_body(tiles_per_seq, tm, u_ref, halo_ref, w_ref, sc_ref, o_ref, us_ref):
    t_in_seq = (pl.program_id(0) % tiles_per_seq) * tm
    first = t_in_seq == 0
    us_ref[0:POOL_HALO, :] = jnp.where(first, 0.0, halo_ref[...])
    us_ref[POOL_HALO:POOL_HALO + tm, :] = u_ref[...]
    tpos = (t_in_seq + lax.broadcasted_iota(I32, (tm, 1), 0) + 1).astype(F32)
    for gi, win in enumerate(POOL_WINDOWS):
        cs = slice(gi * POOL_GROUP_DIM, (gi + 1) * POOL_GROUP_DIM)
        blk = us_ref[:, cs]
        wsum = blk
        span = 1
        while span < win:
            wsum = wsum + pltpu.roll(wsum, span, 0)
            span *= 2
        cur = blk[POOL_HALO:, :]
        d = wsum[POOL_HALO:, :] / jnp.minimum(tpos, float(win)) - cur
        y = _dot(d.astype(BF16), w_ref[gi])
        o_ref[:, cs] = (y * sc_ref[:, cs]).astype(o_ref.dtype)


def _pool(proj, pool_w, pool_scale, S):
    T = proj.shape[0]
    tm = min(256, S)
    ucol = DN_WIDTH // POOL_WIDTH
    halo_blocks = tm // POOL_HALO
    return pl.pallas_call(
        functools.partial(_pool_body, S // tm, tm),
        grid=(T // tm,),
        in_specs=[
            pl.BlockSpec((tm, POOL_WIDTH), lambda i: (i, ucol)),
            pl.BlockSpec((POOL_HALO, POOL_WIDTH), lambda i: (jnp.maximum(i * halo_blocks - 1, 0), ucol)),
            pl.BlockSpec((len(POOL_WINDOWS), POOL_GROUP_DIM, POOL_GROUP_DIM), lambda i: (0, 0, 0)),
            pl.BlockSpec((1, POOL_WIDTH), lambda i: (0, 0)),
        ],
        out_specs=pl.BlockSpec((tm, POOL_WIDTH), lambda i: (i, 0)),
        out_shape=jax.ShapeDtypeStruct((T, POOL_WIDTH), BF16),
        scratch_shapes=[pltpu.VMEM((tm + POOL_HALO, POOL_WIDTH), F32)],
        compiler_params=_cparams(("parallel",), VMEM_SMALL_MIB),
        name="pool",
    )(proj, proj, pool_w.astype(BF16), pool_scale.astype(F32).reshape(1, POOL_WIDTH))


OUTPROJ_SPLIT = 2
PACK_COLS = D_MODEL // 2
U32 = jnp.uint32


def _pack_halves(x):
    lo = lax.bitcast_convert_type(x[:, :PACK_COLS].astype(BF16).astype(F32), U32)
    hi = lax.bitcast_convert_type(x[:, PACK_COLS:].astype(BF16).astype(F32), U32)
    return (lo >> 16) | hi


def _unpack_halves(p):
    lo = lax.bitcast_convert_type(p << 16, F32)
    hi = lax.bitcast_convert_type(p & jnp.uint32(0xFFFF0000), F32)
    return lo, hi


def _route_tile(tm, logits, upper_ref, eidx_ref, gate_ref, rank_ref, cnt_ref, carry_ref):
    m = jnp.max(logits, axis=0, keepdims=True)
    e = jnp.exp(logits - m)
    p = e / jnp.sum(e, axis=0, keepdims=True)
    rows = [p[i:i + 1, :] for i in range(N_EXPERTS)]

    scores = []
    for g in range(N_GROUPS):
        a, b, c, d = rows[EXPERTS_PER_GROUP * g:EXPERTS_PER_GROUP * (g + 1)]
        hi1, lo1 = jnp.maximum(a, b), jnp.minimum(a, b)
        hi2, lo2 = jnp.maximum(c, d), jnp.minimum(c, d)
        top1 = jnp.maximum(hi1, hi2)
        top2 = jnp.maximum(jnp.minimum(hi1, hi2), jnp.where(hi1 >= hi2, lo1, lo2))
        scores.append(top1 + top2)
    gsel = jnp.zeros((1, tm), I32)
    best = scores[0]
    for g in range(1, N_GROUPS):
        better = scores[g] > best
        gsel = jnp.where(better, g, gsel)
        best = jnp.where(better, scores[g], best)
    ing = []
    for j in range(EXPERTS_PER_GROUP):
        sel = rows[(N_GROUPS - 1) * EXPERTS_PER_GROUP + j]
        for g in range(N_GROUPS - 2, -1, -1):
            sel = jnp.where(gsel == g, rows[g * EXPERTS_PER_GROUP + j], sel)
        ing.append(sel)
    i1 = jnp.zeros((1, tm), I32)
    p1 = ing[0]
    for j in range(1, EXPERTS_PER_GROUP):
        better = ing[j] > p1
        i1 = jnp.where(better, j, i1)
        p1 = jnp.where(better, ing[j], p1)
    i2 = jnp.zeros((1, tm), I32)
    p2 = jnp.full((1, tm), -1.0, F32)
    for j in range(EXPERTS_PER_GROUP):
        cand = jnp.where(i1 == j, -1.0, ing[j])
        better = cand > p2
        i2 = jnp.where(better, j, i2)
        p2 = jnp.where(better, cand, p2)
    den = p1 + p2
    e0 = gsel * EXPERTS_PER_GROUP + i1
    e1 = gsel * EXPERTS_PER_GROUP + i2
    eidx_ref[0:1, :] = e0
    eidx_ref[1:2, :] = e1
    gate_ref[0:1, :] = p1 / den
    gate_ref[1:2, :] = p2 / den

    er = lax.broadcasted_iota(I32, (N_EXPERTS, tm), 0)
    oh0 = er == e0
    oh1 = er == e1
    oh = jnp.where(oh0 | oh1, 1.0, 0.0)
    before = carry_ref[:, 0:1] + _dot(oh.astype(BF16), upper_ref[...])
    rank_ref[0:1, :] = jnp.sum(jnp.where(oh0, before, 0.0), axis=0, keepdims=True).astype(I32)
    rank_ref[1:2, :] = jnp.sum(jnp.where(oh1, before, 0.0), axis=0, keepdims=True).astype(I32)
    total = carry_ref[...] + jnp.sum(oh, axis=1, keepdims=True)
    carry_ref[...] = total
    cnt_ref[...] = total.astype(I32)


def _outproj_body(layer, tm, ydn_ref, ypool_ref, w_hbm, x_ref, g_ref, b_ref, wrc_ref, br_ref, upper_ref,
                  o_ref, op_ref, eidx_ref, gate_ref, rank_ref, cnt_ref, carry_ref, w_ref, wstage_ref, wsem):
    @pl.when(pl.program_id(0) == 0)
    def _():
        carry_ref[...] = jnp.zeros(carry_ref.shape, F32)
        _load_weight_bf16(w_hbm, layer, w_ref, wstage_ref, wsem)

    subs = [slice(k * (tm // OUTPROJ_SPLIT), (k + 1) * (tm // OUTPROJ_SPLIT)) for k in range(OUTPROJ_SPLIT)]
    mix = [_dot(ydn_ref[r, :], w_ref[0:DN_WIDTH, :]) + _dot(ypool_ref[r, :], w_ref[DN_WIDTH:, :]) for r in subs]
    x1 = [_layer_norm(ALPHA * x_ref[r, :] + m, g_ref[...], b_ref[...]) for r, m in zip(subs, mix)]
    xh = [x.astype(BF16) for x in x1]
    xl = [(x - h.astype(F32)).astype(BF16) for x, h in zip(x1, xh)]
    lgh = [_dot(h, wrc_ref[...]) for h in xh]
    lgl = [_dot(l, wrc_ref[...]) for l in xl]
    for r, x in zip(subs, x1):
        o_ref[r, :] = x
        op_ref[r, :] = _pack_halves(x)
    lgh_t = jnp.concatenate(lgh, 0).T
    lgl_t = jnp.concatenate(lgl, 0).T
    logits = (lgh_t[0:N_EXPERTS, :] + (lgh_t[N_EXPERTS:2 * N_EXPERTS, :] + lgl_t[0:N_EXPERTS, :])
              + br_ref[:, 0:1])
    _route_tile(tm, logits, upper_ref, eidx_ref, gate_ref, rank_ref, cnt_ref, carry_ref)


def _outproj(layer, y_dn, y_pool, w_out, x, g, b, w_router, b_router):
    T = x.shape[0]
    tm = min(512, T)
    r = jnp.arange(tm)
    upper = (r[:, None] < r[None, :]).astype(BF16)
    wr = w_router.astype(F32)
    wr_hi = wr.astype(BF16)
    wr_lo = (wr - wr_hi.astype(F32)).astype(BF16)
    wr_cat = jnp.pad(jnp.concatenate([wr_hi, wr_lo], 1), ((0, 0), (0, LANES - 2 * N_EXPERTS)))
    row = lambda i: (i, 0)
    tok = lambda i: (0, i)
    const = lambda i: (0, 0)
    return pl.pallas_call(
        functools.partial(_outproj_body, layer, tm),
        grid=(T // tm,),
        in_specs=[
            pl.BlockSpec((tm, DN_WIDTH), row),
            pl.BlockSpec((tm, POOL_WIDTH), row),
            pl.BlockSpec(memory_space=pltpu.HBM),
            pl.BlockSpec((tm, D_MODEL), row),
            pl.BlockSpec((1, D_MODEL), const),
            pl.BlockSpec((1, D_MODEL), const),
            pl.BlockSpec((D_MODEL, LANES), const),
            pl.BlockSpec((N_EXPERTS, LANES), const),
            pl.BlockSpec((tm, tm), const),
        ],
        out_specs=[
            pl.BlockSpec((tm, D_MODEL), row),
            pl.BlockSpec((tm, PACK_COLS), row),
            pl.BlockSpec((2, tm), tok),
            pl.BlockSpec((2, tm), tok),
            pl.BlockSpec((2, tm), tok),
            pl.BlockSpec((N_EXPERTS, LANES), const),
        ],
        out_shape=[
            jax.ShapeDtypeStruct((T, D_MODEL), F32),
            jax.ShapeDtypeStruct((T, PACK_COLS), U32),
            jax.ShapeDtypeStruct((2, T), I32),
            jax.ShapeDtypeStruct((2, T), F32),
            jax.ShapeDtypeStruct((2, T), I32),
            jax.ShapeDtypeStruct((N_EXPERTS, LANES), I32),
        ],
        scratch_shapes=[pltpu.VMEM((N_EXPERTS, LANES), F32)] + _weight_scratch(D_MODEL, D_MODEL),
        compiler_params=_cparams(("arbitrary",), VMEM_MID_MIB),
        name="outproj",
    )(y_dn, y_pool, w_out.astype(F32), x, g.astype(F32).reshape(1, D_MODEL), b.astype(F32).reshape(1, D_MODEL),
      wr_cat, jnp.broadcast_to(b_router.astype(F32)[:, None], (N_EXPERTS, LANES)), upper)


DUMP_ROWS = 2 * EXPERT_BLOCK


def _dispatch_body(tm, T, n_blocks, pos_ref, pad_start_ref, pad_len_ref, nb_ref, x_ref, xs_ref, inv_ref,
                   zrow_ref, sem, zsem):
    step = pl.program_id(0)

    def zero_copy(dst):
        return pltpu.make_async_copy(zrow_ref.at[pl.ds(0, 1)], xs_ref.at[pl.ds(dst, 1)], zsem)

    def dump_row(r):
        return 2 * T + lax.rem(r, DUMP_ROWS)

    @pl.when(step == 0)
    def _():
        zrow_ref[...] = jnp.zeros(zrow_ref.shape, zrow_ref.dtype)
        for e in range(N_EXPERTS):
            def fill(r, carry, e=e):
                zero_copy(pad_start_ref[e] + r).start()
                inv_ref[pad_start_ref[e] + r] = dump_row(pad_start_ref[e] + r)
                return carry
            lax.fori_loop(0, pad_len_ref[e], fill, 0)

        def tail_inv(r, carry):
            inv_ref[r] = dump_row(r)
            return carry
        lax.fori_loop(nb_ref[0] * EXPERT_BLOCK, n_blocks * EXPERT_BLOCK, tail_inv, 0)
        for e in range(N_EXPERTS):
            def drain(r, carry):
                zero_copy(0).wait()
                return carry
            lax.fori_loop(0, pad_len_ref[e], drain, 0)

        def tail_copy(blk):
            return pltpu.make_async_copy(zrow_ref, xs_ref.at[pl.ds(blk * EXPERT_BLOCK, EXPERT_BLOCK)], zsem)

        def tail_fill(blk, carry):
            tail_copy(blk).start()
            return carry

        def tail_drain(blk, carry):
            tail_copy(blk).wait()
            return carry
        lax.fori_loop(nb_ref[0], n_blocks, tail_fill, 0)
        lax.fori_loop(nb_ref[0], n_blocks, tail_drain, 0)

    base = step * tm
    for r in range(tm):
        for k in range(2):
            dst = pos_ref[k * T + base + r]
            inv_ref[dst] = k * T + base + r
            pltpu.make_async_copy(x_ref.at[pl.ds(r, 1)], xs_ref.at[pl.ds(dst, 1)], sem).start(priority=k)
    for k in range(2):
        pltpu.make_async_copy(x_ref, xs_ref.at[pl.ds(0, tm)], sem).wait()


def _dispatch(pos_flat, pad_start, pad_len, nb, x1, n_rows):
    T = x1.shape[0]
    tm = min(512, T)
    return pl.pallas_call(
        functools.partial(_dispatch_body, tm, T, n_rows // EXPERT_BLOCK),
        grid_spec=pltpu.PrefetchScalarGridSpec(
            num_scalar_prefetch=4,
            grid=(T // tm,),
            in_specs=[pl.BlockSpec((tm, PACK_COLS), lambda i, *_: (i, 0))],
            out_specs=[pl.BlockSpec(memory_space=pltpu.HBM), pl.BlockSpec(memory_space=pltpu.SMEM)],
            scratch_shapes=[pltpu.VMEM((EXPERT_BLOCK, PACK_COLS), U32), pltpu.SemaphoreType.DMA,
                            pltpu.SemaphoreType.DMA],
        ),
        out_shape=[jax.ShapeDtypeStruct((n_rows, PACK_COLS), U32), jax.ShapeDtypeStruct((n_rows,), I32)],
        compiler_params=_cparams(("arbitrary",), VMEM_SMALL_MIB),
        name="dispatch",
    )(pos_flat, pad_start, pad_len, nb, x1)


EXPERT_SPLIT = 2
W_CHUNKS = 8
N_CHUNKS = 3 * W_CHUNKS
N_STAGE = 4
GU_ROWS = D_MODEL // W_CHUNKS
DN_ROWS = D_EXPERT // W_CHUNKS


def _experts_body(layer, be_ref, nb_ref, slot_ref, c0_ref, c1_ref, nxt_ref, first_ref, inv_ref,
                  xs_ref, wg_hbm, wu_hbm, wd_hbm, yk_ref, wgu_ref, wdn_ref, stg_a, stg_d, ybuf_ref,
                  sem_a, sem_d, osem):
    i = pl.program_id(0)

    def chunk_copy(kind, e, idx, b):
        if kind == 0:
            return pltpu.make_async_copy(wg_hbm.at[layer, e, pl.ds(idx * GU_ROWS, GU_ROWS), :], stg_a.at[b], sem_a.at[b])
        if kind == 1:
            return pltpu.make_async_copy(wu_hbm.at[layer, e, pl.ds(idx * GU_ROWS, GU_ROWS), :], stg_a.at[b], sem_a.at[b])
        return pltpu.make_async_copy(wd_hbm.at[layer, e, pl.ds(idx * DN_ROWS, DN_ROWS), :], stg_d.at[b], sem_d.at[b])

    def start_chunk(e, c):
        for kind in range(3):
            @pl.when(c // W_CHUNKS == kind)
            def _():
                chunk_copy(kind, e, c % W_CHUNKS, c % N_STAGE).start()

    def finish_chunk(e, c, slot):
        for kind in range(3):
            @pl.when(c // W_CHUNKS == kind)
            def _():
                idx = c % W_CHUNKS
                b = c % N_STAGE
                chunk_copy(kind, e, idx, b).wait()
                if kind < 2:
                    rows = pl.ds(pl.multiple_of(idx * GU_ROWS, GU_ROWS), GU_ROWS)
                    wgu_ref[slot, kind, rows, :] = stg_a[b].astype(BF16)
                else:
                    rows = pl.ds(pl.multiple_of(idx * DN_ROWS, DN_ROWS), DN_ROWS)
                    wdn_ref[slot, rows, :] = stg_d[b].astype(BF16)

    def stream(e, slot, lo, hi):
        def body(c, carry):
            finish_chunk(e, c, slot)

            @pl.when(c + N_STAGE < N_CHUNKS)
            def _():
                start_chunk(e, c + N_STAGE)
            return carry
        lax.fori_loop(lo, hi, body, 0)

    def prime(e):
        for c in range(N_STAGE):
            start_chunk(e, jnp.int32(c))

    dump = yk_ref.shape[0] - DUMP_ROWS

    def out_copy(slot, r, dst):
        return pltpu.make_async_copy(ybuf_ref.at[slot, pl.ds(r, 1)], yk_ref.at[pl.ds(dst, 1)], osem.at[slot])

    def wait_out(slot):
        pltpu.make_async_copy(ybuf_ref.at[slot], yk_ref.at[pl.ds(0, EXPERT_BLOCK)], osem.at[slot]).wait()

    @pl.when(i == 0)
    def _():
        prime(be_ref[0])
        stream(be_ref[0], 0, 0, N_CHUNKS)
        ybuf_ref[...] = jnp.zeros(ybuf_ref.shape, ybuf_ref.dtype)

        def fill(r, carry):
            out_copy(0, r, dump + r).start()
            return carry
        lax.fori_loop(0, EXPERT_BLOCK, fill, 0)

    @pl.when(i < nb_ref[0])
    def _():
        s = slot_ref[i]
        ne = nxt_ref[i]
        par = i % 2

        @pl.when((first_ref[i] == 1) & (ne >= 0))
        def _():
            prime(ne)

        stream(ne, 1 - s, c0_ref[i], c1_ref[i])
        wait_out(par)
        prev = jnp.maximum(i - 1, 0) * EXPERT_BLOCK
        for r in range(EXPERT_BLOCK):
            dst = jnp.where(i >= 1, inv_ref[prev + r], dump + EXPERT_BLOCK + r)
            out_copy(1 - par, r, dst).start(priority=r % 2)
        sub = EXPERT_BLOCK // EXPERT_SPLIT
        hid = []
        for k in range(EXPERT_SPLIT):
            lo, hi = _unpack_halves(xs_ref[k * sub:(k + 1) * sub, :])
            lo = lo.astype(BF16)
            hi = hi.astype(BF16)
            g = _dot(lo, wgu_ref[s, 0, 0:PACK_COLS, :]) + _dot(hi, wgu_ref[s, 0, PACK_COLS:, :])
            u = _dot(lo, wgu_ref[s, 1, 0:PACK_COLS, :]) + _dot(hi, wgu_ref[s, 1, PACK_COLS:, :])
            hid.append((_silu(g) * u).astype(BF16))
        y = [_dot(h, wdn_ref[s]) for h in hid]
        for k in range(EXPERT_SPLIT):
            ybuf_ref[par, k * sub:(k + 1) * sub, :] = _pack_halves(y[k])

    @pl.when(i == nb_ref[0])
    def _():
        par = i % 2
        wait_out(par)
        last = (i - 1) * EXPERT_BLOCK

        def send(r, carry):
            out_copy(1 - par, r, inv_ref[last + r]).start()
            return carry
        lax.fori_loop(0, EXPERT_BLOCK, send, 0)
        wait_out(1 - par)


def _expert_schedule(block_e, nb, n_blocks):
    idx = jnp.arange(n_blocks, dtype=I32)
    valid = idx < nb[0]
    prev = jnp.concatenate([block_e[:1] - 1, block_e[:-1]])
    first = valid & ((idx == 0) | (block_e != prev))
    run_id = jnp.cumsum(first.astype(I32)) - 1
    run_start = lax.cummax(jnp.where(first, idx, 0))
    run_len = jnp.sum((run_id[:, None] == run_id[None, :]) & valid[None, :], -1).astype(I32)
    j = idx - run_start
    nxt_idx = run_start + run_len
    nxt = jnp.where(valid & (nxt_idx < nb[0]), block_e[jnp.minimum(nxt_idx, n_blocks - 1)], -1).astype(I32)
    n = jnp.maximum(run_len, 1)
    has = nxt >= 0
    c0 = jnp.where(has, j * N_CHUNKS // n, 0).astype(I32)
    c1 = jnp.where(has, (j + 1) * N_CHUNKS // n, 0).astype(I32)
    return (run_id % 2).astype(I32), c0, c1, nxt, first.astype(I32)


def _experts(layer, block_e, nb, inv, xs, wg, wu, wd, T):
    n_rows = xs.shape[0]
    bm = EXPERT_BLOCK
    n_blocks = n_rows // bm
    slot, c0, c1, nxt, first = _expert_schedule(block_e, nb, n_blocks)
    blk = lambda i, be, nb, *_: (jnp.maximum(jnp.minimum(i, nb[0] - 1), 0), 0)
    hbm = pl.BlockSpec(memory_space=pltpu.HBM)
    return pl.pallas_call(
        functools.partial(_experts_body, layer),
        grid_spec=pltpu.PrefetchScalarGridSpec(
            num_scalar_prefetch=8,
            grid=(n_blocks,),
            in_specs=[pl.BlockSpec((bm, PACK_COLS), blk), hbm, hbm, hbm],
            out_specs=hbm,
            scratch_shapes=[
                pltpu.VMEM((2, 2, D_MODEL, D_EXPERT), BF16),
                pltpu.VMEM((2, D_EXPERT, D_MODEL), BF16),
                pltpu.VMEM((N_STAGE, GU_ROWS, D_EXPERT), F32),
                pltpu.VMEM((N_STAGE, DN_ROWS, D_MODEL), F32),
                pltpu.VMEM((2, bm, PACK_COLS), U32),
                pltpu.SemaphoreType.DMA((N_STAGE,)),
                pltpu.SemaphoreType.DMA((N_STAGE,)),
                pltpu.SemaphoreType.DMA((2,)),
            ],
        ),
        out_shape=jax.ShapeDtypeStruct((2 * T + DUMP_ROWS, PACK_COLS), U32),
        compiler_params=_cparams(("arbitrary",), VMEM_MID_MIB),
        name="experts",
    )(block_e, nb, slot, c0, c1, nxt, first, inv, xs, wg, wu, wd)


COMBINE_SPLIT = 2


def _combine_body(layer, tm, y0_ref, y1_ref, gate_ref, x1_ref, p_ref, wpg_hbm, wpp_ref, g_ref, b_ref, o_ref, ob_ref,
                  wpg_ref, wstage_ref, wsem):
    @pl.when(pl.program_id(0) == 0)
    def _():
        _load_weight_bf16(wpg_hbm, layer, wpg_ref, wstage_ref, wsem)

    sub = tm // COMBINE_SPLIT
    subs = [slice(h * sub, (h + 1) * sub) for h in range(COMBINE_SPLIT)]
    x2, gl = [], []
    for rows in subs:
        gate = gate_ref[rows, :]
        lo0, hi0 = _unpack_halves(y0_ref[rows, :])
        lo1, hi1 = _unpack_halves(y1_ref[rows, :])
        g0 = gate[:, 0:1]
        g1 = gate[:, 1:2]
        ffn = jnp.concatenate([lo0 * g0 + lo1 * g1, hi0 * g0 + hi1 * g1], 1)
        x2.append(_layer_norm(ALPHA * x1_ref[rows, :] + ffn, g_ref[...], b_ref[...]))
        gl.append(_dot(x2[-1].astype(BF16), wpg_ref[...]))
    pp = [_dot(p_ref[rows, :].astype(BF16), wpp_ref[...]) for rows in subs]
    for rows, x, a, b in zip(subs, x2, gl, pp):
        x3 = x + jax.nn.sigmoid(a) * b
        o_ref[rows, :] = x3
        ob_ref[rows, :] = x3.astype(BF16)


def _combine(layer, yk, gate_t, x1, p_all, w_ple_gate, w_ple_proj, g, b):
    T = x1.shape[0]
    tm = min(256, T)
    row = lambda i: (i, 0)
    const = lambda i: (0, 0)
    second = T // tm
    p_first = layer * (T // tm)
    return pl.pallas_call(
        functools.partial(_combine_body, layer, tm),
        grid=(T // tm,),
        in_specs=[
            pl.BlockSpec((tm, PACK_COLS), row),
            pl.BlockSpec((tm, PACK_COLS), lambda i: (second + i, 0)),
            pl.BlockSpec((tm, 2), row),
            pl.BlockSpec((tm, D_MODEL), row),
            pl.BlockSpec((tm, PLE_DIM), lambda i: (p_first + i, 0)),
            pl.BlockSpec(memory_space=pltpu.HBM),
            pl.BlockSpec((PLE_DIM, D_MODEL), const),
            pl.BlockSpec((1, D_MODEL), const),
            pl.BlockSpec((1, D_MODEL), const),
        ],
        out_specs=[
            pl.BlockSpec((tm, D_MODEL), row),
            pl.BlockSpec((tm, D_MODEL), row),
        ],
        out_shape=[
            jax.ShapeDtypeStruct((T, D_MODEL), F32),
            jax.ShapeDtypeStruct((T, D_MODEL), BF16),
        ],
        scratch_shapes=_weight_scratch(D_MODEL, D_MODEL),
        compiler_params=_cparams(("arbitrary",), VMEM_MID_MIB),
        name="combine",
    )(yk, yk, gate_t, x1, p_all, w_ple_gate.astype(F32), w_ple_proj.astype(BF16),
      g.astype(F32).reshape(1, D_MODEL), b.astype(F32).reshape(1, D_MODEL))


def _route_plan(eidx, rank, cnt, n_blocks):
    bm = EXPERT_BLOCK
    counts = cnt[:, 0]
    pcounts = (counts + bm - 1) // bm * bm
    pend = jnp.cumsum(pcounts)
    pstart = pend - pcounts
    onehot = eidx[:, :, None] == jnp.arange(N_EXPERTS, dtype=I32)
    pos = jnp.sum(jnp.where(onehot, pstart, 0), -1) + rank
    nb = (pend[-1] // bm).astype(I32).reshape(1)
    block_start = jnp.arange(n_blocks, dtype=I32) * bm
    block_e = jnp.minimum(jnp.sum(block_start[:, None] >= pend[None, :], -1), N_EXPERTS - 1).astype(I32)
    pad_start = (pstart + counts).astype(I32)
    pad_len = (pcounts - counts).astype(I32)
    return pos.reshape(-1).astype(I32), block_e, nb, pad_start, pad_len


def kernel(x, p, w_in, conv_w, a_log, dt_bias, dn_norm_w, pool_w, pool_scale, w_out, ln1_g, ln1_b,
           w_router, b_router, w_e_gate, w_e_up, w_e_down, ln2_g, ln2_b, w_ple_proj, w_ple_gate):
    B, S, D = x.shape
    T = B * S
    n_rows = 2 * T + N_EXPERTS * EXPERT_BLOCK
    xf = x.reshape(T, D).astype(F32)
    xb = xf
    for i in range(DEPTH):
        wi = w_in[i]
        w_all = jnp.concatenate([wi[:, :4 * DN_WIDTH], wi[:, 4 * DN_WIDTH + 2 * DN_HEADS:]], 1).astype(BF16)
        w_ba = jnp.pad(wi[:, 4 * DN_WIDTH:4 * DN_WIDTH + 2 * DN_HEADS],
                       ((0, 0), (0, LANES - 2 * DN_HEADS))).astype(BF16)
        qkv, zu, bg = _proj_all(xb, w_all, w_ba, conv_w[i], a_log[i], dt_bias[i], S)
        gt = bg[:, DN_HEADS:2 * DN_HEADS].reshape(B, S // CHUNK, CHUNK, DN_HEADS).transpose(0, 1, 3, 2)
        y_dn = _delta(qkv, zu, bg, gt, dn_norm_w[i], B, S)
        y_pool = _pool(zu, pool_w[i], pool_scale[i], S)
        x1, x1p, eidx, gate, rank, cnt = _outproj(i, y_dn, y_pool, w_out, xf, ln1_g[i], ln1_b[i],
                                                  w_router, b_router)
        pos_flat, block_e, nb, pad_start, pad_len = _route_plan(eidx, rank, cnt, n_rows // EXPERT_BLOCK)
        xs, inv = _dispatch(pos_flat, pad_start, pad_len, nb, x1p, n_rows)
        yk = _experts(i, block_e, nb, inv, xs, w_e_gate, w_e_up, w_e_down, T)
        xf, xb = _combine(i, yk, gate.T, x1, p.reshape(DEPTH * T, PLE_DIM), w_ple_gate, w_ple_proj[i],
                          ln2_g[i], ln2_b[i])
    return xf.reshape(B, S, D).astype(x.dtype)
```

```python
import functools

import jax
import jax.numpy as jnp
from jax import lax
from jax.experimental import pallas as pl
from jax.experimental.pallas import tpu as pltpu

F32 = jnp.float32
BF16 = jnp.bfloat16
I32 = jnp.int32
HIGHEST = lax.Precision.HIGHEST

D_MODEL = 2048
DN_HEADS = 8
HEAD_DIM = 128
DN_WIDTH = DN_HEADS * HEAD_DIM
CONV_WIDTH = 4
CHUNK = 64
POOL_WINDOWS = (2, 4, 8, 16)
POOL_GROUP_DIM = 256
POOL_WIDTH = 1024
N_EXPERTS = 16
N_GROUPS = 4
EXPERTS_PER_GROUP = 4
D_EXPERT = 1024
PLE_DIM = 256
DEPTH = 2
ALPHA = (2.0 * DEPTH) ** 0.25
LN_EPS = 1e-5
RMS_EPS = 1e-6

LANES = 128
MAIN_COLS = 4 * DN_WIDTH + POOL_WIDTH
CONV_HALO = 8
POOL_HALO = 16
EXPERT_BLOCK = 256
V7X_VMEM_MIB = 64
VMEM_MAX_MIB = V7X_VMEM_MIB - 8
VMEM_MID_MIB = V7X_VMEM_MIB - 16
VMEM_SMALL_MIB = V7X_VMEM_MIB // 2


def _cparams(sem, vmem_mib):
    return pltpu.CompilerParams(dimension_semantics=sem, vmem_limit_bytes=vmem_mib * 1024 * 1024)


def _dot(a, b):
    return jnp.dot(a, b, preferred_element_type=F32)


def _dot_hi(a, b):
    return jnp.dot(a, b, preferred_element_type=F32, precision=HIGHEST)


def _silu(x):
    h = 0.5 * x
    return h + h * jnp.tanh(h)


W_STAGE_ROWS = 256


def _load_weight_bf16(w_hbm, layer, w_ref, stage_ref, sem):
    rows = stage_ref.shape[1]
    n = w_ref.shape[0] // rows

    def chunk(c):
        return pltpu.make_async_copy(w_hbm.at[layer, pl.ds(c * rows, rows), :], stage_ref.at[c % 2], sem.at[c % 2])

    chunk(0).start()
    for c in range(n):
        if c + 1 < n:
            chunk(c + 1).start()
        chunk(c).wait()
        w_ref[c * rows:(c + 1) * rows, :] = stage_ref[c % 2].astype(BF16)


def _weight_scratch(rows, cols):
    return [pltpu.VMEM((rows, cols), BF16), pltpu.VMEM((2, W_STAGE_ROWS, cols), F32), pltpu.SemaphoreType.DMA((2,))]


def _layer_norm(h, g, b):
    mu = jnp.mean(h, -1, keepdims=True)
    d = h - mu
    var = jnp.mean(d * d, -1, keepdims=True)
    return d * lax.rsqrt(var + LN_EPS) * g + b


QKV_COLS = 3 * DN_WIDTH
ZU_COLS = DN_WIDTH + POOL_WIDTH


def _proj_all_body(tiles_per_seq, tm, x_ref, w_hbm, wba_ref, cw_ref, gp_ref, tri_ref,
                   qkv_ref, zu_ref, bg_ref, w_ref, halo_ref, raw_q, raw_k, raw_v, wsem):
    i = pl.program_id(0)

    @pl.when(i == 0)
    def _():
        cp = pltpu.make_async_copy(w_hbm, w_ref, wsem)
        cp.start()
        cp.wait()

    first = (i % tiles_per_seq) == 0
    x = x_ref[...].astype(BF16)
    raws = (raw_q, raw_k, raw_v)
    for j in range(3):
        raws[j][0:CONV_HALO, :] = jnp.where(first, 0.0, halo_ref[j])
        r = _dot(x, w_ref[:, j * DN_WIDTH:(j + 1) * DN_WIDTH])
        raws[j][CONV_HALO:, :] = r
        halo_ref[j] = r[tm - CONV_HALO:, :]
    for j in range(ZU_COLS // DN_WIDTH):
        zu_ref[:, j * DN_WIDTH:(j + 1) * DN_WIDTH] = _dot(
            x, w_ref[:, QKV_COLS + j * DN_WIDTH:QKV_COLS + (j + 1) * DN_WIDTH]).astype(zu_ref.dtype)
    ba = _dot(x, wba_ref[...])
    beta = jax.nn.sigmoid(ba)
    xx = ba + gp_ref[1:2, :]
    softplus = jnp.maximum(xx, 0.0) + jnp.log1p(jnp.exp(-jnp.abs(xx)))
    g = -jnp.exp(gp_ref[0:1, :]) * softplus
    tri = tri_ref[...]
    tb = tri.shape[0]
    gam = jnp.concatenate([_dot_hi(tri, g[r:r + tb, :]) for r in range(0, tm, tb)], 0)
    lane = lax.broadcasted_iota(I32, ba.shape, 1)
    bg_ref[...] = jnp.where(lane < DN_HEADS, beta, gam)

    for j in range(3):
        for cb in range(DN_HEADS):
            cs = slice(cb * HEAD_DIM, (cb + 1) * HEAD_DIM)
            blk = raws[j][:, cs]
            wcs = slice(j * DN_WIDTH + cb * HEAD_DIM, j * DN_WIDTH + (cb + 1) * HEAD_DIM)
            acc = blk * cw_ref[CONV_WIDTH - 1:CONV_WIDTH, wcs]
            for s in range(1, CONV_WIDTH):
                acc = acc + pltpu.roll(blk, s, 0) * cw_ref[CONV_WIDTH - 1 - s:CONV_WIDTH - s, wcs]
            y = _silu(acc[CONV_HALO:, :])
            if j < 2:
                y = y * lax.rsqrt(jnp.sum(y * y, -1, keepdims=True) + RMS_EPS)
            if j == 0:
                y = y * (HEAD_DIM ** -0.5)
            qkv_ref[:, wcs] = y


def _proj_all(x, w_all, w_ba, conv_w, a_log, dt_bias, S):
    T = x.shape[0]
    tm = min(512, S)
    tb = min(256, tm)
    pad = LANES - 2 * DN_HEADS
    gp = jnp.stack([
        jnp.pad(a_log.astype(F32), (DN_HEADS, pad)),
        jnp.pad(dt_bias.astype(F32), (DN_HEADS, pad)),
    ])
    r = jnp.arange(tb)
    tri = ((r[:, None] >= r[None, :]) & (r[:, None] // CHUNK == r[None, :] // CHUNK)).astype(F32)
    row = lambda i: (i, 0)
    const = lambda i: (0, 0)
    raw = pltpu.VMEM((tm + CONV_HALO, DN_WIDTH), F32)
    return pl.pallas_call(
        functools.partial(_proj_all_body, S // tm, tm),
        grid=(T // tm,),
        in_specs=[
            pl.BlockSpec((tm, D_MODEL), row),
            pl.BlockSpec(memory_space=pltpu.HBM),
            pl.BlockSpec((D_MODEL, LANES), const),
            pl.BlockSpec((CONV_WIDTH, QKV_COLS), const),
            pl.BlockSpec((2, LANES), const),
            pl.BlockSpec((tb, tb), const),
        ],
        out_specs=[
            pl.BlockSpec((tm, QKV_COLS), row),
            pl.BlockSpec((tm, ZU_COLS), row),
            pl.BlockSpec((tm, LANES), row),
        ],
        out_shape=[
            jax.ShapeDtypeStruct((T, QKV_COLS), F32),
            jax.ShapeDtypeStruct((T, ZU_COLS), BF16),
            jax.ShapeDtypeStruct((T, LANES), F32),
        ],
        scratch_shapes=[pltpu.VMEM((D_MODEL, MAIN_COLS), BF16), pltpu.VMEM((3, CONV_HALO, DN_WIDTH), F32),
                        raw, raw, raw, pltpu.SemaphoreType.DMA],
        compiler_params=_cparams(("arbitrary",), VMEM_MAX_MIB),
        name="proj",
    )(x, w_all, w_ba, conv_w.astype(F32), gp, tri)


DELTA_BATCH = 2


def _delta_body(nb, nc, q_ref, k_ref, v_ref, z_ref, bg_ref, gt_ref, nw_ref, o_ref, state_ref):
    @pl.when(pl.program_id(1) == 0)
    def _():
        state_ref[...] = jnp.zeros(state_ref.shape, F32)

    ii = lax.broadcasted_iota(I32, (CHUNK, CHUNK), 0)
    jj = lax.broadcasted_iota(I32, (CHUNK, CHUNK), 1)
    incl = ii >= jj
    strict = ii > jj
    nt = (((1,), (1,)), ((), ()))
    chains = [(b, h) for b in range(nb) for h in range(DN_HEADS)]
    cs = [slice(h * HEAD_DIM, (h + 1) * HEAD_DIM) for _, h in chains]
    n = range(len(chains))

    def chunk(c, carry):
        r0 = pl.multiple_of(c * CHUNK, CHUNK)
        rows = pl.ds(r0, CHUNK)
        bg = [bg_ref[b, rows, :] for b in range(nb)]
        gt = [gt_ref[b, c] for b in range(nb)]
        kh = [k_ref[b, rows, cs[i]] for i, (b, _) in enumerate(chains)]
        qh = [q_ref[b, rows, cs[i]] for i, (b, _) in enumerate(chains)]
        bcol = [bg[b][:, h:h + 1] for b, h in chains]
        gcol = [bg[b][:, DN_HEADS + h:DN_HEADS + h + 1] for b, h in chains]
        grow = [gt[b][h:h + 1, :] for b, h in chains]
        glast = [grow[i][:, CHUNK - 1:CHUNK] for i in n]
        kb = [kh[i] * bcol[i] for i in n]
        s = [lax.dot_general(jnp.concatenate([kb[i], qh[i]], 0).astype(BF16), kh[i].astype(BF16), nt,
                             preferred_element_type=F32) for i in n]
        decay = [jnp.where(incl, jnp.exp(jnp.where(incl, gcol[i] - grow[i], 0.0)), 0.0) for i in n]
        aqk = [(s[i][CHUNK:] * decay[i]).astype(BF16) for i in n]
        pw = [jnp.where(strict, -s[i][:CHUNK] * decay[i], 0.0) for i in n]
        qs = pw
        pwb = [pw[i].astype(BF16) for i in n]
        pw = [_dot(pwb[i], pwb[i]) for i in n]
        for _ in range(4):
            pwb = [pw[i].astype(BF16) for i in n]
            both = [_dot(jnp.concatenate([pwb[i], qs[i].astype(BF16)], 0), pwb[i]) for i in n]
            qs = [qs[i] + pw[i] + both[i][CHUNK:] for i in n]
            pw = [both[i][:CHUNK] for i in n]
        qp = [_dot(qs[i].astype(BF16), pw[i].astype(BF16)) for i in n]
        qs = [qs[i] + pw[i] + qp[i] for i in n]
        eg = [jnp.exp(gcol[i]) for i in n]
        rhs = [jnp.concatenate([v_ref[b, rows, cs[i]] * bcol[i], kb[i] * eg[i]], 1)
               for i, (b, _) in enumerate(chains)]
        sol = [rhs[i] + _dot(qs[i].astype(BF16), rhs[i].astype(BF16)) for i in n]
        st = [state_ref[i] for i in n]
        r = [_dot(jnp.concatenate([sol[i][:, HEAD_DIM:], qh[i] * eg[i]], 0).astype(BF16), st[i].astype(BF16))
             for i in n]
        v_new = [(sol[i][:, :HEAD_DIM] - r[i][:CHUNK]).astype(BF16) for i in n]
        kdt = [(kh[i] * jnp.exp(glast[i] - gcol[i])).T.astype(BF16) for i in n]
        ou = [_dot(jnp.concatenate([aqk[i], kdt[i]], 0), v_new[i]) for i in n]
        for i, (b, _) in enumerate(chains):
            state_ref[i] = st[i] * jnp.exp(glast[i]) + ou[i][CHUNK:]
            zz = z_ref[b, rows, cs[i]].astype(F32)
            oi = r[i][CHUNK:] + ou[i][:CHUNK]
            y = oi * lax.rsqrt(jnp.mean(oi * oi, -1, keepdims=True) + RMS_EPS) * nw_ref[...]
            o_ref[b, rows, cs[i]] = (y * _silu(zz)).astype(o_ref.dtype)
        return carry

    lax.fori_loop(0, nc, chunk, 0)


def _delta(qkv, zu, bg, gt, norm_w, B, S):
    T = qkv.shape[0]
    nb = DELTA_BATCH if B % DELTA_BATCH == 0 else 1
    sblk = min(1024 // nb, S)
    nc = sblk // CHUNK
    seq = lambda a: a.reshape(B, S, a.shape[-1])
    blk = lambda b, s: (b, s, 0)
    y = pl.pallas_call(
        functools.partial(_delta_body, nb, nc),
        grid=(B // nb, S // sblk),
        in_specs=[
            pl.BlockSpec((nb, sblk, DN_WIDTH), blk),
            pl.BlockSpec((nb, sblk, DN_WIDTH), lambda b, s: (b, s, 1)),
            pl.BlockSpec((nb, sblk, DN_WIDTH), lambda b, s: (b, s, 2)),
            pl.BlockSpec((nb, sblk, DN_WIDTH), blk),
            pl.BlockSpec((nb, sblk, LANES), blk),
            pl.BlockSpec((nb, nc, DN_HEADS, CHUNK), lambda b, s: (b, s, 0, 0)),
            pl.BlockSpec((1, HEAD_DIM), lambda b, s: (0, 0)),
        ],
        out_specs=pl.BlockSpec((nb, sblk, DN_WIDTH), blk),
        out_shape=jax.ShapeDtypeStruct((B, S, DN_WIDTH), BF16),
        scratch_shapes=[pltpu.VMEM((nb * DN_HEADS, HEAD_DIM, HEAD_DIM), F32)],
        compiler_params=_cparams(("parallel", "arbitrary"), VMEM_MID_MIB),
        name="delta",
    )(seq(qkv), seq(qkv), seq(qkv), seq(zu), seq(bg), gt, norm_w.astype(F32).reshape(1, HEAD_DIM))
    return y.reshape(T, DN_WIDTH)


def _pool_body(tiles_per_seq, tm, u_ref, halo_ref, w_ref, sc_ref, o_ref, us_ref):
    t_in_seq = (pl.program_id(0) % tiles_per_seq) * tm
    first = t_in_seq == 0
    us_ref[0:POOL_HALO, :] = jnp.where(first, 0.0, halo_ref[...].astype(F32))
    us_ref[POOL_HALO:POOL_HALO + tm, :] = u_ref[...].astype(F32)
    tpos = (t_in_seq + lax.broadcasted_iota(I32, (tm, 1), 0) + 1).astype(F32)
    for gi, win in enumerate(POOL_WINDOWS):
        cs = slice(gi * POOL_GROUP_DIM, (gi + 1) * POOL_GROUP_DIM)
        blk = us_ref[:, cs]
        wsum = blk
        span = 1
        while span < win:
            wsum = wsum + pltpu.roll(wsum, span, 0)
            span *= 2
        cur = blk[POOL_HALO:, :]
        d = wsum[POOL_HALO:, :] / jnp.minimum(tpos, float(win)) - cur
        y = _dot(d.astype(BF16), w_ref[gi])
        o_ref[:, cs] = (y * sc_ref[:, cs]).astype(o_ref.dtype)


def _pool(proj, pool_w, pool_scale, S):
    T = proj.shape[0]
    tm = min(256, S)
    ucol = DN_WIDTH // POOL_WIDTH
    halo_blocks = tm // POOL_HALO
    return pl.pallas_call(
        functools.partial(_pool_body, S // tm, tm),
        grid=(T // tm,),
        in_specs=[
            pl.BlockSpec((tm, POOL_WIDTH), lambda i: (i, ucol)),
            pl.BlockSpec((POOL_HALO, POOL_WIDTH), lambda i: (jnp.maximum(i * halo_blocks - 1, 0), ucol)),
            pl.BlockSpec((len(POOL_WINDOWS), POOL_GROUP_DIM, POOL_GROUP_DIM), lambda i: (0, 0, 0)),
            pl.BlockSpec((1, POOL_WIDTH), lambda i: (0, 0)),
        ],
        out_specs=pl.BlockSpec((tm, POOL_WIDTH), lambda i: (i, 0)),
        out_shape=jax.ShapeDtypeStruct((T, POOL_WIDTH), BF16),
        scratch_shapes=[pltpu.VMEM((tm + POOL_HALO, POOL_WIDTH), F32)],
        compiler_params=_cparams(("parallel",), VMEM_SMALL_MIB),
        name="pool",
    )(proj, proj, pool_w.astype(BF16), pool_scale.astype(F32).reshape(1, POOL_WIDTH))


OUTPROJ_SPLIT = 2
PACK_COLS = D_MODEL // 2
U32 = jnp.uint32


def _pack_halves(x):
    lo = lax.bitcast_convert_type(x[:, :PACK_COLS].astype(BF16).astype(F32), U32)
    hi = lax.bitcast_convert_type(x[:, PACK_COLS:].astype(BF16).astype(F32), U32)
    return (lo >> 16) | hi


def _unpack_halves(p):
    lo = lax.bitcast_convert_type(p << 16, F32)
    hi = lax.bitcast_convert_type(p & jnp.uint32(0xFFFF0000), F32)
    return lo, hi


def _route_tile(tm, logits, upper_ref, eidx_ref, gate_ref, rank_ref, cnt_ref, carry_ref):
    m = jnp.max(logits, axis=0, keepdims=True)
    e = jnp.exp(logits - m)
    p = e / jnp.sum(e, axis=0, keepdims=True)
    rows = [p[i:i + 1, :] for i in range(N_EXPERTS)]

    scores = []
    for g in range(N_GROUPS):
        a, b, c, d = rows[EXPERTS_PER_GROUP * g:EXPERTS_PER_GROUP * (g + 1)]
        hi1, lo1 = jnp.maximum(a, b), jnp.minimum(a, b)
        hi2, lo2 = jnp.maximum(c, d), jnp.minimum(c, d)
        top1 = jnp.maximum(hi1, hi2)
        top2 = jnp.maximum(jnp.minimum(hi1, hi2), jnp.where(hi1 >= hi2, lo1, lo2))
        scores.append(top1 + top2)
    gsel = jnp.zeros((1, tm), I32)
    best = scores[0]
    for g in range(1, N_GROUPS):
        better = scores[g] > best
        gsel = jnp.where(better, g, gsel)
        best = jnp.where(better, scores[g], best)
    ing = []
    for j in range(EXPERTS_PER_GROUP):
        sel = rows[(N_GROUPS - 1) * EXPERTS_PER_GROUP + j]
        for g in range(N_GROUPS - 2, -1, -1):
            sel = jnp.where(gsel == g, rows[g * EXPERTS_PER_GROUP + j], sel)
        ing.append(sel)
    i1 = jnp.zeros((1, tm), I32)
    p1 = ing[0]
    for j in range(1, EXPERTS_PER_GROUP):
        better = ing[j] > p1
        i1 = jnp.where(better, j, i1)
        p1 = jnp.where(better, ing[j], p1)
    i2 = jnp.zeros((1, tm), I32)
    p2 = jnp.full((1, tm), -1.0, F32)
    for j in range(EXPERTS_PER_GROUP):
        cand = jnp.where(i1 == j, -1.0, ing[j])
        better = cand > p2
        i2 = jnp.where(better, j, i2)
        p2 = jnp.where(better, cand, p2)
    den = p1 + p2
    e0 = gsel * EXPERTS_PER_GROUP + i1
    e1 = gsel * EXPERTS_PER_GROUP + i2
    eidx_ref[0:1, :] = e0
    eidx_ref[1:2, :] = e1
    gate_ref[0:1, :] = p1 / den
    gate_ref[1:2, :] = p2 / den

    er = lax.broadcasted_iota(I32, (N_EXPERTS, tm), 0)
    oh0 = er == e0
    oh1 = er == e1
    oh = jnp.where(oh0 | oh1, 1.0, 0.0)
    before = carry_ref[:, 0:1] + _dot(oh.astype(BF16), upper_ref[...])
    rank_ref[0:1, :] = jnp.sum(jnp.where(oh0, before, 0.0), axis=0, keepdims=True).astype(I32)
    rank_ref[1:2, :] = jnp.sum(jnp.where(oh1, before, 0.0), axis=0, keepdims=True).astype(I32)
    total = carry_ref[...] + jnp.sum(oh, axis=1, keepdims=True)
    carry_ref[...] = total
    cnt_ref[...] = total.astype(I32)


def _outproj_body(layer, tm, ydn_ref, ypool_ref, w_hbm, x_ref, g_ref, b_ref, wrc_ref, br_ref, upper_ref,
                  o_ref, op_ref, eidx_ref, gate_ref, rank_ref, cnt_ref, carry_ref, w_ref, wstage_ref, wsem):
    @pl.when(pl.program_id(0) == 0)
    def _():
        carry_ref[...] = jnp.zeros(carry_ref.shape, F32)
        _load_weight_bf16(w_hbm, layer, w_ref, wstage_ref, wsem)

    subs = [slice(k * (tm // OUTPROJ_SPLIT), (k + 1) * (tm // OUTPROJ_SPLIT)) for k in range(OUTPROJ_SPLIT)]
    mix = [_dot(ydn_ref[r, :], w_ref[0:DN_WIDTH, :]) + _dot(ypool_ref[r, :], w_ref[DN_WIDTH:, :]) for r in subs]
    x1 = [_layer_norm(ALPHA * x_ref[r, :] + m, g_ref[...], b_ref[...]) for r, m in zip(subs, mix)]
    xh = [x.astype(BF16) for x in x1]
    xl = [(x - h.astype(F32)).astype(BF16) for x, h in zip(x1, xh)]
    lgh = [_dot(h, wrc_ref[...]) for h in xh]
    lgl = [_dot(l, wrc_ref[...]) for l in xl]
    for r, x in zip(subs, x1):
        o_ref[r, :] = x
        op_ref[r, :] = _pack_halves(x)
    lgh_t = jnp.concatenate(lgh, 0).T
    lgl_t = jnp.concatenate(lgl, 0).T
    logits = (lgh_t[0:N_EXPERTS, :] + (lgh_t[N_EXPERTS:2 * N_EXPERTS, :] + lgl_t[0:N_EXPERTS, :])
              + br_ref[:, 0:1])
    _route_tile(tm, logits, upper_ref, eidx_ref, gate_ref, rank_ref, cnt_ref, carry_ref)


def _outproj(layer, y_dn, y_pool, w_out, x, g, b, w_router, b_router):
    T = x.shape[0]
    tm = min(512, T)
    r = jnp.arange(tm)
    upper = (r[:, None] < r[None, :]).astype(BF16)
    wr = w_router.astype(F32)
    wr_hi = wr.astype(BF16)
    wr_lo = (wr - wr_hi.astype(F32)).astype(BF16)
    wr_cat = jnp.pad(jnp.concatenate([wr_hi, wr_lo], 1), ((0, 0), (0, LANES - 2 * N_EXPERTS)))
    row = lambda i: (i, 0)
    tok = lambda i: (0, i)
    const = lambda i: (0, 0)
    return pl.pallas_call(
        functools.partial(_outproj_body, layer, tm),
        grid=(T // tm,),
        in_specs=[
            pl.BlockSpec((tm, DN_WIDTH), row),
            pl.BlockSpec((tm, POOL_WIDTH), row),
            pl.BlockSpec(memory_space=pltpu.HBM),
            pl.BlockSpec((tm, D_MODEL), row),
            pl.BlockSpec((1, D_MODEL), const),
            pl.BlockSpec((1, D_MODEL), const),
            pl.BlockSpec((D_MODEL, LANES), const),
            pl.BlockSpec((N_EXPERTS, LANES), const),
            pl.BlockSpec((tm, tm), const),
        ],
        out_specs=[
            pl.BlockSpec((tm, D_MODEL), row),
            pl.BlockSpec((tm, PACK_COLS), row),
            pl.BlockSpec((2, tm), tok),
            pl.BlockSpec((2, tm), tok),
            pl.BlockSpec((2, tm), tok),
            pl.BlockSpec((N_EXPERTS, LANES), const),
        ],
        out_shape=[
            jax.ShapeDtypeStruct((T, D_MODEL), F32),
            jax.ShapeDtypeStruct((T, PACK_COLS), U32),
            jax.ShapeDtypeStruct((2, T), I32),
            jax.ShapeDtypeStruct((2, T), F32),
            jax.ShapeDtypeStruct((2, T), I32),
            jax.ShapeDtypeStruct((N_EXPERTS, LANES), I32),
        ],
        scratch_shapes=[pltpu.VMEM((N_EXPERTS, LANES), F32)] + _weight_scratch(D_MODEL, D_MODEL),
        compiler_params=_cparams(("arbitrary",), VMEM_MID_MIB),
        name="outproj",
    )(y_dn, y_pool, w_out.astype(F32), x, g.astype(F32).reshape(1, D_MODEL), b.astype(F32).reshape(1, D_MODEL),
      wr_cat, jnp.broadcast_to(b_router.astype(F32)[:, None], (N_EXPERTS, LANES)), upper)


DUMP_ROWS = 2 * EXPERT_BLOCK


def _dispatch_body(tm, T, n_blocks, pos_ref, pad_start_ref, pad_len_ref, nb_ref, x_ref, xs_ref, inv_ref,
                   zrow_ref, sem, zsem):
    step = pl.program_id(0)

    def zero_copy(dst):
        return pltpu.make_async_copy(zrow_ref.at[pl.ds(0, 1)], xs_ref.at[pl.ds(dst, 1)], zsem)

    def dump_row(r):
        return 2 * T + lax.rem(r, DUMP_ROWS)

    @pl.when(step == 0)
    def _():
        zrow_ref[...] = jnp.zeros(zrow_ref.shape, zrow_ref.dtype)
        for e in range(N_EXPERTS):
            def fill(r, carry, e=e):
                zero_copy(pad_start_ref[e] + r).start()
                inv_ref[pad_start_ref[e] + r] = dump_row(pad_start_ref[e] + r)
                return carry
            lax.fori_loop(0, pad_len_ref[e], fill, 0)

        def tail_inv(r, carry):
            inv_ref[r] = dump_row(r)
            return carry
        lax.fori_loop(nb_ref[0] * EXPERT_BLOCK, n_blocks * EXPERT_BLOCK, tail_inv, 0)
        for e in range(N_EXPERTS):
            def drain(r, carry):
                zero_copy(0).wait()
                return carry
            lax.fori_loop(0, pad_len_ref[e], drain, 0)

        def tail_copy(blk):
            return pltpu.make_async_copy(zrow_ref, xs_ref.at[pl.ds(blk * EXPERT_BLOCK, EXPERT_BLOCK)], zsem)

        def tail_fill(blk, carry):
            tail_copy(blk).start()
            return carry

        def tail_drain(blk, carry):
            tail_copy(blk).wait()
            return carry
        lax.fori_loop(nb_ref[0], n_blocks, tail_fill, 0)
        lax.fori_loop(nb_ref[0], n_blocks, tail_drain, 0)

    base = step * tm
    for r in range(tm):
        for k in range(2):
            dst = pos_ref[k * T + base + r]
            inv_ref[dst] = k * T + base + r
            pltpu.make_async_copy(x_ref.at[pl.ds(r, 1)], xs_ref.at[pl.ds(dst, 1)], sem).start(priority=k)
    for k in range(2):
        pltpu.make_async_copy(x_ref, xs_ref.at[pl.ds(0, tm)], sem).wait()


def _dispatch(pos_flat, pad_start, pad_len, nb, x1, n_rows):
    T = x1.shape[0]
    tm = min(512, T)
    return pl.pallas_call(
        functools.partial(_dispatch_body, tm, T, n_rows // EXPERT_BLOCK),
        grid_spec=pltpu.PrefetchScalarGridSpec(
            num_scalar_prefetch=4,
            grid=(T // tm,),
            in_specs=[pl.BlockSpec((tm, PACK_COLS), lambda i, *_: (i, 0))],
            out_specs=[pl.BlockSpec(memory_space=pltpu.HBM), pl.BlockSpec(memory_space=pltpu.SMEM)],
            scratch_shapes=[pltpu.VMEM((EXPERT_BLOCK, PACK_COLS), U32), pltpu.SemaphoreType.DMA,
                            pltpu.SemaphoreType.DMA],
        ),
        out_shape=[jax.ShapeDtypeStruct((n_rows, PACK_COLS), U32), jax.ShapeDtypeStruct((n_rows,), I32)],
        compiler_params=_cparams(("arbitrary",), VMEM_SMALL_MIB),
        name="dispatch",
    )(pos_flat, pad_start, pad_len, nb, x1)


EXPERT_SPLIT = 2
W_CHUNKS = 8
N_CHUNKS = 3 * W_CHUNKS
N_STAGE = 4
GU_ROWS = D_MODEL // W_CHUNKS
DN_ROWS = D_EXPERT // W_CHUNKS


def _experts_body(layer, be_ref, nb_ref, slot_ref, c0_ref, c1_ref, nxt_ref, first_ref, inv_ref,
                  xs_ref, wg_hbm, wu_hbm, wd_hbm, yk_ref, wgu_ref, wdn_ref, stg_a, stg_d, ybuf_ref,
                  sem_a, sem_d, osem):
    i = pl.program_id(0)

    def chunk_copy(kind, e, idx, b):
        if kind == 0:
            return pltpu.make_async_copy(wg_hbm.at[layer, e, pl.ds(idx * GU_ROWS, GU_ROWS), :], stg_a.at[b], sem_a.at[b])
        if kind == 1:
            return pltpu.make_async_copy(wu_hbm.at[layer, e, pl.ds(idx * GU_ROWS, GU_ROWS), :], stg_a.at[b], sem_a.at[b])
        return pltpu.make_async_copy(wd_hbm.at[layer, e, pl.ds(idx * DN_ROWS, DN_ROWS), :], stg_d.at[b], sem_d.at[b])

    def start_chunk(e, c):
        for kind in range(3):
            @pl.when(c // W_CHUNKS == kind)
            def _():
                chunk_copy(kind, e, c % W_CHUNKS, c % N_STAGE).start()

    def finish_chunk(e, c, slot):
        for kind in range(3):
            @pl.when(c // W_CHUNKS == kind)
            def _():
                idx = c % W_CHUNKS
                b = c % N_STAGE
                chunk_copy(kind, e, idx, b).wait()
                if kind < 2:
                    rows = pl.ds(pl.multiple_of(idx * GU_ROWS, GU_ROWS), GU_ROWS)
                    wgu_ref[slot, kind, rows, :] = stg_a[b].astype(BF16)
                else:
                    rows = pl.ds(pl.multiple_of(idx * DN_ROWS, DN_ROWS), DN_ROWS)
                    wdn_ref[slot, rows, :] = stg_d[b].astype(BF16)

    def stream(e, slot, lo, hi):
        def body(c, carry):
            finish_chunk(e, c, slot)

            @pl.when(c + N_STAGE < N_CHUNKS)
            def _():
                start_chunk(e, c + N_STAGE)
            return carry
        lax.fori_loop(lo, hi, body, 0)

    def prime(e):
        for c in range(N_STAGE):
            start_chunk(e, jnp.int32(c))

    dump = yk_ref.shape[0] - DUMP_ROWS

    def out_copy(slot, r, dst):
        return pltpu.make_async_copy(ybuf_ref.at[slot, pl.ds(r, 1)], yk_ref.at[pl.ds(dst, 1)], osem.at[slot])

    def wait_out(slot):
        pltpu.make_async_copy(ybuf_ref.at[slot], yk_ref.at[pl.ds(0, EXPERT_BLOCK)], osem.at[slot]).wait()

    @pl.when(i == 0)
    def _():
        prime(be_ref[0])
        stream(be_ref[0], 0, 0, N_CHUNKS)
        ybuf_ref[...] = jnp.zeros(ybuf_ref.shape, ybuf_ref.dtype)

        def fill(r, carry):
            out_copy(0, r, dump + r).start()
            return carry
        lax.fori_loop(0, EXPERT_BLOCK, fill, 0)

    @pl.when(i < nb_ref[0])
    def _():
        s = slot_ref[i]
        ne = nxt_ref[i]
        par = i % 2

        @pl.when((first_ref[i] == 1) & (ne >= 0))
        def _():
            prime(ne)

        stream(ne, 1 - s, c0_ref[i], c1_ref[i])
        wait_out(par)
        prev = jnp.maximum(i - 1, 0) * EXPERT_BLOCK
        for r in range(EXPERT_BLOCK):
            dst = jnp.where(i >= 1, inv_ref[prev + r], dump + EXPERT_BLOCK + r)
            out_copy(1 - par, r, dst).start(priority=r % 2)
        sub = EXPERT_BLOCK // EXPERT_SPLIT
        hid = []
        for k in range(EXPERT_SPLIT):
            lo, hi = _unpack_halves(xs_ref[k * sub:(k + 1) * sub, :])
            lo = lo.astype(BF16)
            hi = hi.astype(BF16)
            g = _dot(lo, wgu_ref[s, 0, 0:PACK_COLS, :]) + _dot(hi, wgu_ref[s, 0, PACK_COLS:, :])
            u = _dot(lo, wgu_ref[s, 1, 0:PACK_COLS, :]) + _dot(hi, wgu_ref[s, 1, PACK_COLS:, :])
            hid.append((_silu(g) * u).astype(BF16))
        y = [_dot(h, wdn_ref[s]) for h in hid]
        for k in range(EXPERT_SPLIT):
            ybuf_ref[par, k * sub:(k + 1) * sub, :] = _pack_halves(y[k])

    @pl.when(i == nb_ref[0])
    def _():
        par = i % 2
        wait_out(par)
        last = (i - 1) * EXPERT_BLOCK

        def send(r, carry):
            out_copy(1 - par, r, inv_ref[last + r]).start()
            return carry
        lax.fori_loop(0, EXPERT_BLOCK, send, 0)
        wait_out(1 - par)


def _expert_schedule(block_e, nb, n_blocks):
    idx = jnp.arange(n_blocks, dtype=I32)
    valid = idx < nb[0]
    prev = jnp.concatenate([block_e[:1] - 1, block_e[:-1]])
    first = valid & ((idx == 0) | (block_e != prev))
    run_id = jnp.cumsum(first.astype(I32)) - 1
    run_start = lax.cummax(jnp.where(first, idx, 0))
    run_len = jnp.sum((run_id[:, None] == run_id[None, :]) & valid[None, :], -1).astype(I32)
    j = idx - run_start
    nxt_idx = run_start + run_len
    nxt = jnp.where(valid & (nxt_idx < nb[0]), block_e[jnp.minimum(nxt_idx, n_blocks - 1)], -1).astype(I32)
    n = jnp.maximum(run_len, 1)
    has = nxt >= 0
    c0 = jnp.where(has, j * N_CHUNKS // n, 0).astype(I32)
    c1 = jnp.where(has, (j + 1) * N_CHUNKS // n, 0).astype(I32)
    return (run_id % 2).astype(I32), c0, c1, nxt, first.astype(I32)


def _experts(layer, block_e, nb, inv, xs, wg, wu, wd, T):
    n_rows = xs.shape[0]
    bm = EXPERT_BLOCK
    n_blocks = n_rows // bm
    slot, c0, c1, nxt, first = _expert_schedule(block_e, nb, n_blocks)
    blk = lambda i, be, nb, *_: (jnp.maximum(jnp.minimum(i, nb[0] - 1), 0), 0)
    hbm = pl.BlockSpec(memory_space=pltpu.HBM)
    return pl.pallas_call(
        functools.partial(_experts_body, layer),
        grid_spec=pltpu.PrefetchScalarGridSpec(
            num_scalar_prefetch=8,
            grid=(n_blocks,),
            in_specs=[pl.BlockSpec((bm, PACK_COLS), blk), hbm, hbm, hbm],
            out_specs=hbm,
            scratch_shapes=[
                pltpu.VMEM((2, 2, D_MODEL, D_EXPERT), BF16),
                pltpu.VMEM((2, D_EXPERT, D_MODEL), BF16),
                pltpu.VMEM((N_STAGE, GU_ROWS, D_EXPERT), F32),
                pltpu.VMEM((N_STAGE, DN_ROWS, D_MODEL), F32),
                pltpu.VMEM((2, bm, PACK_COLS), U32),
                pltpu.SemaphoreType.DMA((N_STAGE,)),
                pltpu.SemaphoreType.DMA((N_STAGE,)),
                pltpu.SemaphoreType.DMA((2,)),
            ],
        ),
        out_shape=jax.ShapeDtypeStruct((2 * T + DUMP_ROWS, PACK_COLS), U32),
        compiler_params=_cparams(("arbitrary",), VMEM_MID_MIB),
        name="experts",
    )(block_e, nb, slot, c0, c1, nxt, first, inv, xs, wg, wu, wd)


COMBINE_SPLIT = 2


def _combine_body(layer, tm, y0_ref, y1_ref, gate_ref, x1_ref, p_ref, wpg_hbm, wpp_ref, g_ref, b_ref, o_ref, ob_ref,
                  wpg_ref, wstage_ref, wsem):
    @pl.when(pl.program_id(0) == 0)
    def _():
        _load_weight_bf16(wpg_hbm, layer, wpg_ref, wstage_ref, wsem)

    sub = tm // COMBINE_SPLIT
    subs = [slice(h * sub, (h + 1) * sub) for h in range(COMBINE_SPLIT)]
    x2, gl = [], []
    for rows in subs:
        gate = gate_ref[rows, :]
        lo0, hi0 = _unpack_halves(y0_ref[rows, :])
        lo1, hi1 = _unpack_halves(y1_ref[rows, :])
        g0 = gate[:, 0:1]
        g1 = gate[:, 1:2]
        ffn = jnp.concatenate([lo0 * g0 + lo1 * g1, hi0 * g0 + hi1 * g1], 1)
        x2.append(_layer_norm(ALPHA * x1_ref[rows, :] + ffn, g_ref[...], b_ref[...]))
        gl.append(_dot(x2[-1].astype(BF16), wpg_ref[...]))
    pp = [_dot(p_ref[rows, :].astype(BF16), wpp_ref[...]) for rows in subs]
    for rows, x, a, b in zip(subs, x2, gl, pp):
        x3 = x + jax.nn.sigmoid(a) * b
        o_ref[rows, :] = x3
        ob_ref[rows, :] = x3.astype(BF16)


def _combine(layer, yk, gate_t, x1, p_all, w_ple_gate, w_ple_proj, g, b):
    T = x1.shape[0]
    tm = min(256, T)
    row = lambda i: (i, 0)
    const = lambda i: (0, 0)
    second = T // tm
    p_first = layer * (T // tm)
    return pl.pallas_call(
        functools.partial(_combine_body, layer, tm),
        grid=(T // tm,),
        in_specs=[
            pl.BlockSpec((tm, PACK_COLS), row),
            pl.BlockSpec((tm, PACK_COLS), lambda i: (second + i, 0)),
            pl.BlockSpec((tm, 2), row),
            pl.BlockSpec((tm, D_MODEL), row),
            pl.BlockSpec((tm, PLE_DIM), lambda i: (p_first + i, 0)),
            pl.BlockSpec(memory_space=pltpu.HBM),
            pl.BlockSpec((PLE_DIM, D_MODEL), const),
            pl.BlockSpec((1, D_MODEL), const),
            pl.BlockSpec((1, D_MODEL), const),
        ],
        out_specs=[
            pl.BlockSpec((tm, D_MODEL), row),
            pl.BlockSpec((tm, D_MODEL), row),
        ],
        out_shape=[
            jax.ShapeDtypeStruct((T, D_MODEL), F32),
            jax.ShapeDtypeStruct((T, D_MODEL), BF16),
        ],
        scratch_shapes=_weight_scratch(D_MODEL, D_MODEL),
        compiler_params=_cparams(("arbitrary",), VMEM_MID_MIB),
        name="combine",
    )(yk, yk, gate_t, x1, p_all, w_ple_gate.astype(F32), w_ple_proj.astype(BF16),
      g.astype(F32).reshape(1, D_MODEL), b.astype(F32).reshape(1, D_MODEL))


def _route_plan(eidx, rank, cnt, n_blocks):
    bm = EXPERT_BLOCK
    counts = cnt[:, 0]
    pcounts = (counts + bm - 1) // bm * bm
    pend = jnp.cumsum(pcounts)
    pstart = pend - pcounts
    onehot = eidx[:, :, None] == jnp.arange(N_EXPERTS, dtype=I32)
    pos = jnp.sum(jnp.where(onehot, pstart, 0), -1) + rank
    nb = (pend[-1] // bm).astype(I32).reshape(1)
    block_start = jnp.arange(n_blocks, dtype=I32) * bm
    block_e = jnp.minimum(jnp.sum(block_start[:, None] >= pend[None, :], -1), N_EXPERTS - 1).astype(I32)
    pad_start = (pstart + counts).astype(I32)
    pad_len = (pcounts - counts).astype(I32)
    return pos.reshape(-1).astype(I32), block_e, nb, pad_start, pad_len


def kernel(x, p, w_in, conv_w, a_log, dt_bias, dn_norm_w, pool_w, pool_scale, w_out, ln1_g, ln1_b,
           w_router, b_router, w_e_gate, w_e_up, w_e_down, ln2_g, ln2_b, w_ple_proj, w_ple_gate):
    B, S, D = x.shape
    T = B * S
    n_rows = 2 * T + N_EXPERTS * EXPERT_BLOCK
    xf = x.reshape(T, D).astype(F32)
    xb = xf
    for i in range(DEPTH):
        wi = w_in[i]
        w_all = jnp.concatenate([wi[:, :4 * DN_WIDTH], wi[:, 4 * DN_WIDTH + 2 * DN_HEADS:]], 1).astype(BF16)
        w_ba = jnp.pad(wi[:, 4 * DN_WIDTH:4 * DN_WIDTH + 2 * DN_HEADS],
                       ((0, 0), (0, LANES - 2 * DN_HEADS))).astype(BF16)
        qkv, zu, bg = _proj_all(xb, w_all, w_ba, conv_w[i], a_log[i], dt_bias[i], S)
        gt = bg[:, DN_HEADS:2 * DN_HEADS].reshape(B, S // CHUNK, CHUNK, DN_HEADS).transpose(0, 1, 3, 2)
        y_dn = _delta(qkv, zu, bg, gt, dn_norm_w[i], B, S)
        y_pool = _pool(zu, pool_w[i], pool_scale[i], S)
        x1, x1p, eidx, gate, rank, cnt = _outproj(i, y_dn, y_pool, w_out, xf, ln1_g[i], ln1_b[i],
                                                  w_router, b_router)
        pos_flat, block_e, nb, pad_start, pad_len = _route_plan(eidx, rank, cnt, n_rows // EXPERT_BLOCK)
        xs, inv = _dispatch(pos_flat, pad_start, pad_len, nb, x1p, n_rows)
        yk = _experts(i, block_e, nb, inv, xs, w_e_gate, w_e_up, w_e_down, T)
        xf, xb = _combine(i, yk, gate.T, x1, p.reshape(DEPTH * T, PLE_DIM), w_ple_gate, w_ple_proj[i],
                          ln2_g[i], ln2_b[i])
    return xf.reshape(B, S, D).astype(x.dtype)
```
